```python
import math
import jax, jax.numpy as jnp
from jax import lax
import numpy as np

D_MODEL = 2048
BATCH = 4
SEQ = 2048
DEPTH = 2
DEC_BATCH = 32
DEC_SEQ = 16
PAST_LEN = 1024

CHUNK = 64
Q_BLOCK = 128
A_HEADS = 4
A_KV_HEADS = 2
A_HEAD_DIM = 128
IDX_HEADS = 8
IDX_DIM = 64
TOPK_MAX = 256
B_HEADS = 4
B_QK_DIM = 64
B_V_DIM = 128
C_HEADS = 8
C_K_DIM = 128
C_V_DIM = 128
MIX_WIDTH = A_HEADS * A_HEAD_DIM + B_HEADS * B_V_DIM + C_HEADS * C_V_DIM
D_FF = 5632
NORM_EPS = 1e-6
NEG_BIG = -1e30
LB_FLOOR = 1e-30
IN_SIZES = (A_HEADS * A_HEAD_DIM, A_KV_HEADS * A_HEAD_DIM, A_KV_HEADS * A_HEAD_DIM,
            IDX_HEADS * IDX_DIM, IDX_DIM, IDX_HEADS,
            B_HEADS * 2 * B_QK_DIM, B_HEADS * 2 * B_QK_DIM, B_HEADS * B_V_DIM,
            C_HEADS * C_K_DIM, C_HEADS * C_K_DIM, C_HEADS * C_V_DIM, C_HEADS * C_V_DIM)
IN_WIDTH = sum(IN_SIZES)

kernel_name = "hybrid_stream_dsa_diff_hgrn2_step"


def rmsnorm(x, g):
    x32 = x.astype(jnp.float32)
    y = x32 * lax.rsqrt(jnp.mean(x32 * x32, axis=-1, keepdims=True) + NORM_EPS)
    return (y * g.astype(jnp.float32)).astype(x.dtype)


def swiglu_ffn(x, w_in, w_out):
    gate, up = jnp.split(x @ w_in, 2, axis=-1)
    return (jax.nn.silu(gate) * up) @ w_out


def visible(q_pos, k_pos):
    return k_pos[None, :] < (q_pos[:, None] // CHUNK + 1) * CHUNK


def dsa_block(q, qi, wi, qpos, k, v, kidx, topk):
    bsz, t = q.shape[0], q.shape[1]
    s = k.shape[1]
    mask = visible(qpos, jnp.arange(s))
    logits = jnp.einsum('bthd,bsd->bths', qi, kidx).astype(jnp.float32) * (IDX_DIM ** -0.5)
    score = jnp.einsum('bths,bth->bts', jax.nn.relu(logits), wi.astype(jnp.float32) * (IDX_HEADS ** -0.5))
    score = jnp.where(mask[None], score, NEG_BIG)
    _, sel = lax.top_k(score, topk)
    valid = mask[jnp.arange(t)[None, :, None], sel]
    bidx = jnp.arange(bsz)[:, None, None]
    kg = k[bidx, sel]
    vg = v[bidx, sel]
    qg = q.reshape(bsz, t, A_KV_HEADS, A_HEADS // A_KV_HEADS, A_HEAD_DIM)
    att = jnp.einsum('btgrd,btkgd->btgrk', qg, kg).astype(jnp.float32) * (A_HEAD_DIM ** -0.5)
    att = jnp.where(valid[:, :, None, None, :], att, NEG_BIG)
    p = jax.nn.softmax(att, axis=-1).astype(vg.dtype)
    o = jnp.einsum('btgrk,btkgd->btgrd', p, vg)
    return o.reshape(bsz, t, A_HEADS * A_HEAD_DIM)


def diff_block(q, qpos, k, v, lam, g_diff, lam_init):
    s = k.shape[1]
    mask = visible(qpos, jnp.arange(s))
    logits = jnp.einsum('bthcd,bshcd->bhcts', q, k).astype(jnp.float32) * (B_QK_DIM ** -0.5)
    logits = jnp.where(mask, logits, NEG_BIG)
    p = jax.nn.softmax(logits, axis=-1)
    w = p[:, :, 0] - lam * p[:, :, 1]
    o = jnp.einsum('bhts,bshd->bthd', w.astype(v.dtype), v)
    o = rmsnorm(o, g_diff) * (1.0 - lam_init)
    return o.reshape(o.shape[0], o.shape[1], B_HEADS * B_V_DIM)


def sweep_query_blocks(fn, qs, qpos):
    bsz, t = qs[0].shape[0], qs[0].shape[1]
    nb = t // Q_BLOCK
    blocked = tuple(jnp.moveaxis(a.reshape(bsz, nb, Q_BLOCK, *a.shape[2:]), 1, 0) for a in qs)
    out = lax.map(lambda args: fn(*args[:-1], args[-1]), blocked + (qpos.reshape(nb, Q_BLOCK),))
    return jnp.moveaxis(out, 0, 1).reshape(bsz, t, out.shape[-1])


def hgrn2_scan(q, k, v, logf, s0):
    bsz, l = q.shape[0], q.shape[1]
    blk = min(CHUNK, l)
    nb = l // blk

    def to_blocks(a):
        return jnp.moveaxis(a.reshape(bsz, nb, blk, *a.shape[2:]), 1, 0)

    causal = jnp.tril(jnp.ones((blk, blk), dtype=bool))[None, :, :, None, None]

    def step(S, inp):
        qb, kb, vb, lfb = inp
        b = jnp.cumsum(lfb.astype(jnp.float32), axis=1)
        o_inter = jnp.einsum('bthk,bhkv->bthv', qb * jnp.exp(b), S)
        diff = b[:, :, None] - b[:, None, :]
        decay = jnp.where(causal, jnp.exp(jnp.where(causal, diff, 0.0)), 0.0)
        a = jnp.einsum('bthk,bshk,btshk->bhts', qb, kb, decay)
        o_intra = jnp.einsum('bhts,bshv->bthv', a, vb)
        b_last = b[:, -1]
        S_new = jnp.exp(b_last)[..., None] * S + jnp.einsum(
            'bshk,bshv->bhkv', kb * jnp.exp(b_last[:, None] - b), vb)
        return S_new, o_inter + o_intra

    S_fin, o = lax.scan(step, s0.astype(jnp.float32),
                        (to_blocks(q), to_blocks(k), to_blocks(v), to_blocks(logf)))
    return jnp.moveaxis(o, 0, 1).reshape(bsz, l, v.shape[2], v.shape[3]), S_fin


def token_mixing(u, past, w_in, lam_vecs, g_diff, g_hgrn, lb, layer_idx):
    bsz, t = u.shape[0], u.shape[1]
    split_points = np.cumsum(IN_SIZES)[:-1].tolist()
    (a_q, a_k, a_v, a_qi, a_ki, a_w, b_q, b_k, b_v,
     c_q, c_f, c_i, c_g) = jnp.split(u @ w_in, split_points, axis=-1)
    a_q = a_q.reshape(bsz, t, A_HEADS, A_HEAD_DIM)
    a_k = a_k.reshape(bsz, t, A_KV_HEADS, A_HEAD_DIM)
    a_v = a_v.reshape(bsz, t, A_KV_HEADS, A_HEAD_DIM)
    a_qi = a_qi.reshape(bsz, t, IDX_HEADS, IDX_DIM)
    b_q = b_q.reshape(bsz, t, B_HEADS, 2, B_QK_DIM)
    b_k = b_k.reshape(bsz, t, B_HEADS, 2 * B_QK_DIM)
    b_v = b_v.reshape(bsz, t, B_HEADS, B_V_DIM)
    c_q = c_q.reshape(bsz, t, C_HEADS, C_K_DIM)
    c_f = c_f.reshape(bsz, t, C_HEADS, C_K_DIM)
    c_i = c_i.reshape(bsz, t, C_HEADS, C_V_DIM)
    c_g = c_g.reshape(bsz, t, C_HEADS, C_V_DIM)

    if past is None:
        offset = 0
        ka, va, kia, kb, vb = a_k, a_v, a_ki, b_k, b_v
        s0 = jnp.zeros((bsz, C_HEADS, C_K_DIM, C_V_DIM), jnp.float32)
    else:
        pa_k, pa_v, pa_ki, pb_k, pb_v, s0 = past
        offset = pa_k.shape[1]
        ka = jnp.concatenate([pa_k, a_k], axis=1)
        va = jnp.concatenate([pa_v, a_v], axis=1)
        kia = jnp.concatenate([pa_ki, a_ki], axis=1)
        kb = jnp.concatenate([pb_k, b_k], axis=1)
        vb = jnp.concatenate([pb_v, b_v], axis=1)
    n_keys = offset + t
    topk = min(TOPK_MAX, n_keys // 4)
    qpos = offset + jnp.arange(t)

    lam_init = 0.8 - 0.6 * math.exp(-0.3 * layer_idx)
    lv = lam_vecs.astype(jnp.float32)
    lam = jnp.exp(jnp.sum(lv[0] * lv[1])) - jnp.exp(jnp.sum(lv[2] * lv[3])) + lam_init
    kb_split = kb.reshape(bsz, n_keys, B_HEADS, 2, B_QK_DIM)

    attn_a = lambda q, qi, wi, pos: dsa_block(q, qi, wi, pos, ka, va, kia, topk)
    attn_b = lambda q, pos: diff_block(q, pos, kb_split, vb, lam, g_diff, lam_init)
    if past is None:
        o_a = sweep_query_blocks(attn_a, (a_q, a_qi, a_w), qpos)
        o_b = sweep_query_blocks(attn_b, (b_q,), qpos)
    else:
        o_a = attn_a(a_q, a_qi, a_w, qpos)
        o_b = attn_b(b_q, qpos)

    lb = lb.reshape(C_HEADS, C_K_DIM)
    zf = c_f.astype(jnp.float32)
    log_f = jnp.logaddexp(jnp.log(jnp.maximum(lb, LB_FLOOR)), jnp.log1p(-lb) + jax.nn.log_sigmoid(zf))
    k_c = (1.0 - lb) * jax.nn.sigmoid(-zf)
    o_c, s_c = hgrn2_scan(c_q, k_c, c_i, log_f, s0)
    o_c = rmsnorm(o_c, g_hgrn) * jax.nn.silu(c_g.astype(jnp.float32))
    o_c = o_c.reshape(bsz, t, C_HEADS * C_V_DIM).astype(o_a.dtype)

    mix = jnp.concatenate([o_a, o_b, o_c], axis=-1)
    return mix, (a_k, a_v, a_ki, b_k, b_v, s_c)


def setup_inputs(seed: int = 0) -> dict:
    key = jax.random.key(seed)
    ks = jax.random.split(key, 24)

    def nrm(k, shape, scale):
        return jax.random.normal(k, shape, jnp.float32) * scale

    def gain(k, shape):
        return 1.0 + 0.02 * jax.random.normal(k, shape, jnp.float32)

    return {
        "x_prompt": nrm(ks[0], (BATCH, SEQ, D_MODEL), 1.0),
        "x_sample": nrm(ks[1], (DEC_BATCH, DEC_SEQ, D_MODEL), 1.0),
        "cache_a_k": nrm(ks[2], (DEPTH, DEC_BATCH, PAST_LEN, A_KV_HEADS, A_HEAD_DIM), 1.0),
        "cache_a_v": nrm(ks[3], (DEPTH, DEC_BATCH, PAST_LEN, A_KV_HEADS, A_HEAD_DIM), 1.0),
        "cache_a_kidx": nrm(ks[4], (DEPTH, DEC_BATCH, PAST_LEN, IDX_DIM), 1.0),
        "cache_b_k": nrm(ks[5], (DEPTH, DEC_BATCH, PAST_LEN, B_HEADS, 2 * B_QK_DIM), 1.0),
        "cache_b_v": nrm(ks[6], (DEPTH, DEC_BATCH, PAST_LEN, B_HEADS, B_V_DIM), 1.0),
        "state_c": nrm(ks[7], (DEPTH, DEC_BATCH, C_HEADS, C_K_DIM, C_V_DIM), 0.5),
        "g_ffn1": gain(ks[8], (DEPTH, D_MODEL)),
        "w_ffn1_in": nrm(ks[9], (DEPTH, D_MODEL, 2 * D_FF), D_MODEL ** -0.5),
        "w_ffn1_out": nrm(ks[10], (DEPTH, D_FF, D_MODEL), D_FF ** -0.5),
        "g_mix": gain(ks[11], (DEPTH, D_MODEL)),
        "w_in": nrm(ks[12], (DEPTH, D_MODEL, IN_WIDTH), D_MODEL ** -0.5),
        "diff_lambda": nrm(ks[13], (DEPTH, 4, B_QK_DIM), 0.1),
        "g_diff": gain(ks[14], (DEPTH, B_V_DIM)),
        "gamma_lb": nrm(ks[15], (DEPTH, C_HEADS * C_K_DIM), 0.5),
        "g_hgrn": gain(ks[16], (DEPTH, C_V_DIM)),
        "w_out": nrm(ks[17], (DEPTH, MIX_WIDTH, D_MODEL), MIX_WIDTH ** -0.5),
        "g_ffn2": gain(ks[18], (DEPTH, D_MODEL)),
        "w_ffn2_in": nrm(ks[19], (DEPTH, D_MODEL, 2 * D_FF), D_MODEL ** -0.5),
        "w_ffn2_out": nrm(ks[20], (DEPTH, D_FF, D_MODEL), D_FF ** -0.5),
        "g_final": gain(ks[21], (D_MODEL,)),
    }


def reference(x_prompt, x_sample, cache_a_k, cache_a_v, cache_a_kidx, cache_b_k, cache_b_v, state_c,
              g_ffn1, w_ffn1_in, w_ffn1_out, g_mix, w_in, diff_lambda, g_diff, gamma_lb, g_hgrn, w_out,
              g_ffn2, w_ffn2_in, w_ffn2_out, g_final):
    p_lb = jax.nn.softmax(gamma_lb.astype(jnp.float32), axis=0)
    lower_bounds = jnp.cumsum(p_lb, axis=0) - p_lb[0]

    def run(x, past):
        new = []
        for l in range(DEPTH):
            layer_past = None if past is None else tuple(c[l] for c in past)
            h = x + 0.5 * swiglu_ffn(rmsnorm(x, g_ffn1[l]), w_ffn1_in[l], w_ffn1_out[l])
            m, st = token_mixing(rmsnorm(h, g_mix[l]), layer_past, w_in[l], diff_lambda[l],
                                 g_diff[l], g_hgrn[l], lower_bounds[l], l)
            h = h + m @ w_out[l]
            x = h + 0.5 * swiglu_ffn(rmsnorm(h, g_ffn2[l]), w_ffn2_in[l], w_ffn2_out[l])
            new.append(st)
        stacked = [jnp.stack([s[i] for s in new]) for i in range(6)]
        return rmsnorm(x, g_final), stacked

    y_prompt, (p_a_k, p_a_v, p_a_kidx, p_b_k, p_b_v, p_c_state) = run(x_prompt, None)
    y_sample, (s_a_k, s_a_v, s_a_kidx, s_b_k, s_b_v, s_c_state) = run(
        x_sample, (cache_a_k, cache_a_v, cache_a_kidx, cache_b_k, cache_b_v, state_c))
    return (y_prompt, y_sample, p_a_k, p_a_v, p_a_kidx, p_b_k, p_b_v, p_c_state,
            s_a_k, s_a_v, s_a_kidx, s_b_k, s_b_v, s_c_state)
```

```python
import functools
import math

import jax
import jax.numpy as jnp
from jax import lax
from jax.experimental import pallas as pl
from jax.experimental.pallas import tpu as pltpu

F32 = jnp.float32
BF16 = jnp.bfloat16

CHUNK = 64
A_HEADS, A_KV_HEADS, A_HEAD_DIM = 4, 2, 128
IDX_HEADS, IDX_DIM = 8, 64
TOPK_MAX = 256
B_HEADS, B_QK_DIM, B_V_DIM = 4, 64, 128
C_HEADS, C_K_DIM, C_V_DIM = 8, 128, 128
NORM_EPS = 1e-6
NEG_BIG = -1e30
LB_FLOOR = 1e-30

A_Q_W = A_HEADS * A_HEAD_DIM
A_KV_W = A_KV_HEADS * A_HEAD_DIM
A_QI_W = IDX_HEADS * IDX_DIM
B_QK_W = B_HEADS * 2 * B_QK_DIM
B_V_W = B_HEADS * B_V_DIM
C_W = C_HEADS * C_K_DIM
IN_SIZES = (A_Q_W, A_KV_W, A_KV_W, A_QI_W, IDX_DIM, IDX_HEADS, B_QK_W, B_QK_W, B_V_W, C_W, C_W, C_W, C_W)
MAIN_W = A_Q_W + 2 * A_KV_W + A_QI_W + 2 * B_QK_W + B_V_W + 4 * C_W
SMALL_W = 128
COL_A_Q, COL_A_K, COL_A_V, COL_A_QI = 0, 512, 768, 1024
COL_B_Q, COL_B_K, COL_B_V = 1536, 2048, 2560
COL_C_Q, COL_C_F, COL_C_I, COL_C_G = 3072, 4096, 5120, 6144

V7X_LANES = 128
V7X_VMEM_LIMIT = 56 * 1024 * 1024


def _cparams(semantics):
    return pltpu.CompilerParams(dimension_semantics=semantics, vmem_limit_bytes=V7X_VMEM_LIMIT)


def _rmsnorm(x, g):
    return x * lax.rsqrt(jnp.mean(x * x, axis=-1, keepdims=True) + NORM_EPS) * g


def _dot(a, b):
    return jnp.dot(a, b, preferred_element_type=F32)


def _dot_nt(a, b):
    return lax.dot_general(a, b, (((1,), (1,)), ((), ())), preferred_element_type=F32)


def _dot_tn(a, b):
    return lax.dot_general(a, b, (((0,), (0,)), ((), ())), preferred_element_type=F32)


def _row_tile(n, pref):
    t = min(n, pref)
    while n % t:
        t //= 2
    return t


def _ffn_body(*refs, n_ff_steps, final_norm):
    if final_norm:
        x_ref, g_ref, wg_ref, wu_ref, wo_ref, gf_ref, o_ref, xn_ref = refs
    else:
        x_ref, g_ref, wg_ref, wu_ref, wo_ref, o_ref, xn_ref = refs
    j = pl.program_id(1)

    @pl.when(j == 0)
    def _():
        x = x_ref[...]
        xn_ref[...] = _rmsnorm(x, g_ref[...]).astype(BF16)
        o_ref[...] = x

    xn = xn_ref[...]
    gate = _dot(xn, wg_ref[...])
    up = _dot(xn, wu_ref[...])
    act = (gate * jax.nn.sigmoid(gate)) * up * 0.5
    o_ref[...] += _dot(act.astype(BF16), wo_ref[...])

    if final_norm:
        @pl.when(j == n_ff_steps - 1)
        def _():
            o_ref[...] = _rmsnorm(o_ref[...], gf_ref[...])


def _ffn(x, g, w_in_bf, w_out_bf, g_final=None):
    n, d = x.shape
    f = w_out_bf.shape[0]
    tm = _row_tile(n, 512)
    tf = 512 if f % 512 == 0 else f
    nj = f // tf
    final_norm = g_final is not None
    in_specs = [
        pl.BlockSpec((tm, d), lambda i, j: (i, 0)),
        pl.BlockSpec((1, d), lambda i, j: (0, 0)),
        pl.BlockSpec((d, tf), lambda i, j: (0, j)),
        pl.BlockSpec((d, tf), lambda i, j: (0, j + nj)),
        pl.BlockSpec((tf, d), lambda i, j: (j, 0)),
    ]
    args = [x, g.reshape(1, d), w_in_bf, w_in_bf, w_out_bf]
    if final_norm:
        in_specs.append(pl.BlockSpec((1, d), lambda i, j: (0, 0)))
        args.append(g_final.reshape(1, d))
    return pl.pallas_call(
        functools.partial(_ffn_body, n_ff_steps=nj, final_norm=final_norm),
        grid=(n // tm, nj),
        in_specs=in_specs,
        out_specs=pl.BlockSpec((tm, d), lambda i, j: (i, 0)),
        out_shape=jax.ShapeDtypeStruct((n, d), F32),
        scratch_shapes=[pltpu.VMEM((tm, d), BF16)],
        compiler_params=_cparams(("parallel", "arbitrary")),
        name="ffn",
    )(*args)


def _proj_body(x_ref, g_ref, wm_ref, ws_ref, om_ref, os_ref, xn_ref):
    j = pl.program_id(1)

    @pl.when(j == 0)
    def _():
        xn = _rmsnorm(x_ref[...], g_ref[...]).astype(BF16)
        xn_ref[...] = xn
        os_ref[...] = _dot(xn, ws_ref[...])

    om_ref[...] = _dot(xn_ref[...], wm_ref[...])


def _proj(x, g, wm_bf, ws_bf):
    n, d = x.shape
    tm = _row_tile(n, 512)
    tn = 1024
    return pl.pallas_call(
        _proj_body,
        grid=(n // tm, MAIN_W // tn),
        in_specs=[
            pl.BlockSpec((tm, d), lambda i, j: (i, 0)),
            pl.BlockSpec((1, d), lambda i, j: (0, 0)),
            pl.BlockSpec((d, tn), lambda i, j: (0, j)),
            pl.BlockSpec((d, SMALL_W), lambda i, j: (0, 0)),
        ],
        out_specs=[
            pl.BlockSpec((tm, tn), lambda i, j: (i, j)),
            pl.BlockSpec((tm, SMALL_W), lambda i, j: (i, 0)),
        ],
        out_shape=[jax.ShapeDtypeStruct((n, MAIN_W), F32), jax.ShapeDtypeStruct((n, SMALL_W), F32)],
        scratch_shapes=[pltpu.VMEM((tm, d), BF16)],
        compiler_params=_cparams(("parallel", "arbitrary")),
        name="in_proj",
    )(x, g.reshape(1, d), wm_bf, ws_bf)


def _sortable_key(x):
    bits = lax.bitcast_convert_type(x, jnp.int32)
    return jnp.where(bits < 0, bits ^ jnp.int32(0x7FFFFFFF), bits)


def _kth_largest(key, k):
    rows = key.shape[0]

    def body(it, t):
        cand = t + lax.shift_left(jnp.int32(1), jnp.int32(31) - it)
        cnt = jnp.sum(jnp.where(key >= cand, 1.0, 0.0), axis=1, keepdims=True)
        return jnp.where(cnt >= k, cand, t)

    t0 = jnp.full((rows, 1), jnp.iinfo(jnp.int32).min, jnp.int32)
    return lax.fori_loop(0, 32, body, t0)


def _tie_cut(tie, kpos, need, nbits):
    rows = tie.shape[0]

    def body(it, t):
        cand = t + lax.shift_left(jnp.int32(1), jnp.int32(nbits - 1) - it)
        cnt = jnp.sum(jnp.where(tie & (kpos < cand), 1.0, 0.0), axis=1, keepdims=True)
        return jnp.where(cnt < need, cand, t)

    return lax.fori_loop(0, nbits, body, jnp.zeros((rows, 1), jnp.int32))


def _visible(tq, s_pad, q_pos0, n_keys):
    qpos = q_pos0 + pl.program_id(1) * tq + lax.broadcasted_iota(jnp.int32, (tq, 1), 0)
    limit = jnp.minimum(((qpos >> 6) + 1) << 6, n_keys)
    kpos = lax.broadcasted_iota(jnp.int32, (tq, s_pad), 1)
    return kpos < limit, kpos


def _dsa_body(q_ref, qi_ref, wq_ref, k_ref, v_ref, ki_ref, o_ref, *, tq, s_pad, n_keys, q_pos0, topk):
    vis, kpos = _visible(tq, s_pad, q_pos0, n_keys)

    kidx = ki_ref[:, 0:IDX_DIM].astype(BF16)
    wq = wq_ref[:, IDX_DIM:IDX_DIM + IDX_HEADS] * (IDX_HEADS ** -0.5)
    score = jnp.zeros((tq, s_pad), F32)
    for h in range(IDX_HEADS):
        qi_h = qi_ref[:, h * IDX_DIM:(h + 1) * IDX_DIM].astype(BF16)
        logit = _dot_nt(qi_h, kidx) * (IDX_DIM ** -0.5)
        score = score + jnp.maximum(logit, 0.0) * wq[:, h:h + 1]
    score = jnp.where(vis, score + 0.0, NEG_BIG)

    key = _sortable_key(score)
    thr = _kth_largest(key, float(topk))
    n_gt = jnp.sum(jnp.where(key > thr, 1.0, 0.0), axis=1, keepdims=True)
    tie = key == thr
    cut = _tie_cut(tie, kpos, float(topk) - n_gt, max(1, (s_pad - 1).bit_length()))
    sel = ((key > thr) | (tie & (kpos <= cut))) & vis

    kb = k_ref[...].astype(BF16)
    vb = v_ref[...].astype(BF16)
    rep = A_HEADS // A_KV_HEADS
    for h in range(A_HEADS):
        g = h // rep
        q_h = q_ref[:, h * A_HEAD_DIM:(h + 1) * A_HEAD_DIM].astype(BF16)
        k_g = kb[:, g * A_HEAD_DIM:(g + 1) * A_HEAD_DIM]
        v_g = vb[:, g * A_HEAD_DIM:(g + 1) * A_HEAD_DIM]
        att = jnp.where(sel, _dot_nt(q_h, k_g) * (A_HEAD_DIM ** -0.5), NEG_BIG)
        p = jnp.exp(att - jnp.max(att, axis=1, keepdims=True))
        l = jnp.sum(p, axis=1, keepdims=True)
        o = _dot(p.astype(BF16), v_g) / l
        o_ref[:, h * A_HEAD_DIM:(h + 1) * A_HEAD_DIM] = o.astype(o_ref.dtype)


def _blk(arr_col, rows, width, kind, nq):
    _, col = arr_col
    cb = col // width
    assert col % width == 0
    if kind == "q":
        return pl.BlockSpec((rows, width), lambda b, i: (b * nq + i, cb))
    return pl.BlockSpec((rows, width), lambda b, i: (b, cb))


def _dsa(q, qi, wq, k, v, ki, *, bsz, t, s_pad, n_keys, q_pos0, tq):
    nq = t // tq
    topk = min(TOPK_MAX, n_keys // 4)
    ki_w = ki[0].shape[1]
    body = functools.partial(_dsa_body, tq=tq, s_pad=s_pad, n_keys=n_keys, q_pos0=q_pos0, topk=topk)
    return pl.pallas_call(
        body,
        grid=(bsz, nq),
        in_specs=[
            _blk(q, tq, A_Q_W, "q", nq),
            _blk(qi, tq, A_QI_W, "q", nq),
            _blk(wq, tq, SMALL_W, "q", nq),
            _blk(k, s_pad, A_KV_W, "kv", nq),
            _blk(v, s_pad, A_KV_W, "kv", nq),
            _blk(ki, s_pad, ki_w, "kv", nq),
        ],
        out_specs=pl.BlockSpec((tq, A_Q_W), lambda b, i: (b * nq + i, 0)),
        out_shape=jax.ShapeDtypeStruct((bsz * t, A_Q_W), BF16),
        compiler_params=_cparams(("parallel", "parallel")),
        name="dsa",
    )(q[0], qi[0], wq[0], k[0], v[0], ki[0])


def _diff_body(q_ref, k_ref, v_ref, lam_ref, g_ref, o_ref, *, tq, s_pad, n_keys, q_pos0, lam_init):
    vis, _ = _visible(tq, s_pad, q_pos0, n_keys)
    lv = lam_ref[...]
    lam = (jnp.exp(jnp.sum(lv[0:1] * lv[1:2], axis=1, keepdims=True))
           - jnp.exp(jnp.sum(lv[2:3] * lv[3:4], axis=1, keepdims=True)) + lam_init)
    kb = k_ref[...].astype(BF16)
    vb = v_ref[...].astype(BF16)
    gain = g_ref[...] * (1.0 - lam_init)
    for h in range(B_HEADS):
        v_h = vb[:, h * B_V_DIM:(h + 1) * B_V_DIM]
        parts = []
        for c in range(2):
            lo = h * 2 * B_QK_DIM + c * B_QK_DIM
            q_c = q_ref[:, lo:lo + B_QK_DIM].astype(BF16)
            k_c = kb[:, lo:lo + B_QK_DIM]
            logit = jnp.where(vis, _dot_nt(q_c, k_c) * (B_QK_DIM ** -0.5), NEG_BIG)
            p = jnp.exp(logit - jnp.max(logit, axis=1, keepdims=True))
            l = jnp.sum(p, axis=1, keepdims=True)
            parts.append(_dot(p.astype(BF16), v_h) / l)
        o = parts[0] - lam * parts[1]
        o = o * lax.rsqrt(jnp.mean(o * o, axis=1, keepdims=True) + NORM_EPS) * gain
        o_ref[:, h * B_V_DIM:(h + 1) * B_V_DIM] = o.astype(o_ref.dtype)


def _diff(q, k, v, lam_vecs, g_diff, *, bsz, t, s_pad, n_keys, q_pos0, tq, layer_idx):
    nq = t // tq
    lam_init = 0.8 - 0.6 * math.exp(-0.3 * layer_idx)
    body = functools.partial(_diff_body, tq=tq, s_pad=s_pad, n_keys=n_keys, q_pos0=q_pos0, lam_init=lam_init)
    return pl.pallas_call(
        body,
        grid=(bsz, nq),
        in_specs=[
            _blk(q, tq, B_QK_W, "q", nq),
            _blk(k, s_pad, B_QK_W, "kv", nq),
            _blk(v, s_pad, B_V_W, "kv", nq),
            pl.BlockSpec((4, B_QK_DIM), lambda b, i: (0, 0)),
            pl.BlockSpec((1, B_V_DIM), lambda b, i: (0, 0)),
        ],
        out_specs=pl.BlockSpec((tq, B_V_W), lambda b, i: (b * nq + i, 0)),
        out_shape=jax.ShapeDtypeStruct((bsz * t, B_V_W), BF16),
        compiler_params=_cparams(("parallel", "parallel")),
        name="diff_attn",
    )(q[0], k[0], v[0], lam_vecs, g_diff.reshape(1, B_V_DIM))


def _cumsum_rows(x):
    n = x.shape[0]
    row = lax.broadcasted_iota(jnp.int32, (n, 1), 0)
    d = 1
    while d < n:
        x = x + jnp.where(row >= d, pltpu.roll(x, d, axis=0), 0.0)
        d *= 2
    return x


def _anchor_rows(b, c):
    n, w = b.shape
    if 2 * c >= 8:
        pieces = [jnp.broadcast_to(b[j * 2 * c + c - 1:j * 2 * c + c, :], (2 * c, w)) for j in range(n // (2 * c))]
        return pieces[0] if len(pieces) == 1 else jnp.concatenate(pieces, axis=0)
    rmod = lax.broadcasted_iota(jnp.int32, (n, 1), 0) & (2 * c - 1)
    m = b
    for s in range(-c, c):
        if s != 0:
            m = jnp.where(rmod == c - 1 - s, pltpu.roll(b, (-s) % n, axis=0), m)
    return m


def _hgrn_body(*refs, chunk, layer_idx, has_s0):
    if has_s0:
        q_ref, f_ref, i_ref, g_ref, glb_ref, gn_ref, s0_ref, o_ref, s_ref, st_ref = refs
    else:
        q_ref, f_ref, i_ref, g_ref, glb_ref, gn_ref, o_ref, s_ref, st_ref = refs
    ci = pl.program_id(1)

    @pl.when(ci == 0)
    def _():
        for h in range(C_HEADS):
            st_ref[h] = s0_ref[0, h].T if has_s0 else jnp.zeros((C_V_DIM, C_K_DIM), F32)

    glb = glb_ref[...]
    pe = jnp.exp(glb - jnp.max(glb, axis=0, keepdims=True))
    pn = pe / jnp.sum(pe, axis=0, keepdims=True)
    lb = jnp.sum(pn[0:layer_idx + 1], axis=0, keepdims=True) - pn[0:1]

    zf = f_ref[...]
    log_sig = jnp.minimum(zf, 0.0) - jnp.log1p(jnp.exp(-jnp.abs(zf)))
    la = jnp.log(jnp.maximum(lb, LB_FLOOR))
    lc = jnp.log1p(-lb) + log_sig
    hi = jnp.maximum(la, lc)
    log_f = hi + jnp.log1p(jnp.exp(jnp.minimum(la, lc) - hi))
    kc = (1.0 - lb) * jax.nn.sigmoid(-zf)
    b_all = _cumsum_rows(log_f)

    row = lax.broadcasted_iota(jnp.int32, (chunk, 1), 0)
    rr = lax.broadcasted_iota(jnp.int32, (chunk, chunk), 0)
    cc = lax.broadcasted_iota(jnp.int32, (chunk, chunk), 1)
    gn = gn_ref[...]

    for h in range(C_HEADS):
        sl = slice(h * C_K_DIM, (h + 1) * C_K_DIM)
        q = q_ref[:, sl]
        k = kc[:, sl]
        v = i_ref[:, sl]
        b = b_all[:, sl]
        vb = v.astype(BF16)
        st = st_ref[h]
        o = _dot_nt((q * jnp.exp(b)).astype(BF16), st.astype(BF16))
        o = o + jnp.sum(q * k, axis=1, keepdims=True) * v
        a = jnp.zeros((chunk, chunk), F32)
        c = chunk // 2
        while c >= 1:
            upper = ((row >> (c.bit_length() - 1)) & 1) == 1
            d = b - _anchor_rows(b, c)
            e = jnp.exp(jnp.where(upper, d, -d))
            qt = jnp.where(upper, q * e, 0.0).astype(BF16)
            kt = jnp.where(upper, 0.0, k * e).astype(BF16)
            sh = (2 * c).bit_length() - 1
            a = a + jnp.where((rr >> sh) == (cc >> sh), _dot_nt(qt, kt), 0.0)
            c //= 2
        o = o + _dot(a.astype(BF16), vb)
        b_last = b[chunk - 1:chunk, :]
        k_dec = (k * jnp.exp(b_last - b)).astype(BF16)
        st_new = st * jnp.exp(b_last) + _dot_tn(vb, k_dec)
        st_ref[h] = st_new

        o = o * lax.rsqrt(jnp.mean(o * o, axis=1, keepdims=True) + NORM_EPS) * gn
        zg = g_ref[:, sl]
        o_ref[:, sl] = (o * (zg * jax.nn.sigmoid(zg))).astype(o_ref.dtype)

        @pl.when(ci == pl.num_programs(1) - 1)
        def _():
            s_ref[0, h] = st_new.T


def _hgrn(cq, cf, cin, cg, gamma_lb, g_hgrn, s0, *, bsz, t, layer_idx):
    chunk = min(CHUNK, t)
    nc = t // chunk
    depth = gamma_lb.shape[0]
    has_s0 = s0 is not None

    def seq_blk(arr_col):
        cb = arr_col[1] // C_W
        return pl.BlockSpec((chunk, C_W), lambda b, c: (b * nc + c, cb))

    in_specs = [seq_blk(cq), seq_blk(cf), seq_blk(cin), seq_blk(cg),
                pl.BlockSpec((depth, C_W), lambda b, c: (0, 0)),
                pl.BlockSpec((1, C_V_DIM), lambda b, c: (0, 0))]
    args = [cq[0], cf[0], cin[0], cg[0], gamma_lb, g_hgrn.reshape(1, C_V_DIM)]
    state_spec = pl.BlockSpec((1, C_HEADS, C_K_DIM, C_V_DIM), lambda b, c: (b, 0, 0, 0))
    if has_s0:
        in_specs.append(state_spec)
        args.append(s0)
    return pl.pallas_call(
        functools.partial(_hgrn_body, chunk=chunk, layer_idx=layer_idx, has_s0=has_s0),
        grid=(bsz, nc),
        in_specs=in_specs,
        out_specs=[pl.BlockSpec((chunk, C_W), lambda b, c: (b * nc + c, 0)), state_spec],
        out_shape=[jax.ShapeDtypeStruct((bsz * t, C_W), BF16),
                   jax.ShapeDtypeStruct((bsz, C_HEADS, C_K_DIM, C_V_DIM), F32)],
        scratch_shapes=[pltpu.VMEM((C_HEADS, C_V_DIM, C_K_DIM), F32)],
        compiler_params=_cparams(("parallel", "arbitrary")),
        name="hgrn2",
    )(*args)


def _outproj_body(h_ref, oa_ref, ob_ref, oc_ref, wa_ref, wb_ref, wc_ref, o_ref):
    acc = _dot(oa_ref[...], wa_ref[...])
    acc = acc + _dot(ob_ref[...], wb_ref[...])
    acc = acc + _dot(oc_ref[...], wc_ref[...])
    o_ref[...] = h_ref[...] + acc


def _outproj(h, oa, ob, oc, w_out_bf):
    n, d = h.shape
    tm = _row_tile(n, 512)
    row = lambda w: pl.BlockSpec((tm, w), lambda i: (i, 0))
    na, nb = A_Q_W // 512, B_V_W // 512
    return pl.pallas_call(
        _outproj_body,
        grid=(n // tm,),
        in_specs=[row(d), row(A_Q_W), row(B_V_W), row(C_W),
                  pl.BlockSpec((A_Q_W, d), lambda i: (0, 0)),
                  pl.BlockSpec((B_V_W, d), lambda i: (na, 0)),
                  pl.BlockSpec((C_W, d), lambda i: ((na + nb) // 2, 0))],
        out_specs=row(d),
        out_shape=jax.ShapeDtypeStruct((n, d), F32),
        compiler_params=_cparams(("parallel",)),
        name="out_proj",
    )(h, oa, ob, oc, w_out_bf, w_out_bf, w_out_bf)


def _split_w_in(w):
    offs = [0]
    for s in IN_SIZES:
        offs.append(offs[-1] + s)
    seg = lambda i: w[:, offs[i]:offs[i + 1]]
    main = jnp.concatenate([seg(0), seg(1), seg(2), seg(3), seg(6), seg(7), seg(8), seg(9), seg(10), seg(11), seg(12)],
                           axis=1)
    small = jnp.concatenate([seg(4), seg(5), jnp.zeros((w.shape[0], SMALL_W - IDX_DIM - IDX_HEADS), w.dtype)], axis=1)
    return main.astype(BF16), small.astype(BF16)


def _pad_keys(past, new, s_pad):
    bsz, p, w = past.shape
    t = new.shape[1]
    parts = [past, new]
    if s_pad > p + t:
        parts.append(jnp.zeros((bsz, s_pad - p - t, w), past.dtype))
    return jnp.concatenate(parts, axis=1).reshape(bsz * s_pad, w)


def _run_group(x, past, params, g_final, gamma_lb):
    bsz, t, d = x.shape
    n = bsz * t
    x = x.reshape(n, d)
    depth = len(params)
    new = []
    for l, p in enumerate(params):
        h = _ffn(x, p["g_ffn1"], p["w_ffn1_in"], p["w_ffn1_out"])
        main, small = _proj(h, p["g_mix"], p["w_in_main"], p["w_in_small"])
        a_k = main[:, COL_A_K:COL_A_K + A_KV_W]
        a_v = main[:, COL_A_V:COL_A_V + A_KV_W]
        a_ki = small[:, 0:IDX_DIM]
        b_k = main[:, COL_B_K:COL_B_K + B_QK_W]
        b_v = main[:, COL_B_V:COL_B_V + B_V_W]
        if past is None:
            s_pad, n_keys, q_pos0 = t, t, 0
            tq = _row_tile(t, 128)
            ka, va, kia = (main, COL_A_K), (main, COL_A_V), (small, 0)
            kb, vb = (main, COL_B_K), (main, COL_B_V)
            s0 = None
        else:
            pa_k, pa_v, pa_ki, pb_k, pb_v, pc = (c[l] for c in past)
            p_len = pa_k.shape[1]
            n_keys, q_pos0 = p_len + t, p_len
            s_pad = -(-n_keys // V7X_LANES) * V7X_LANES
            tq = t
            cat = lambda pst, nw, w: (_pad_keys(pst.reshape(bsz, p_len, w), nw.reshape(bsz, t, w), s_pad), 0)
            ka, va = cat(pa_k, a_k, A_KV_W), cat(pa_v, a_v, A_KV_W)
            kia = cat(pa_ki, a_ki, IDX_DIM)
            kb, vb = cat(pb_k, b_k, B_QK_W), cat(pb_v, b_v, B_V_W)
            s0 = pc
        geom = dict(bsz=bsz, t=t, s_pad=s_pad, n_keys=n_keys, q_pos0=q_pos0, tq=tq)
        o_a = _dsa((main, COL_A_Q), (main, COL_A_QI), (small, 0), ka, va, kia, **geom)
        o_b = _diff((main, COL_B_Q), kb, vb, p["diff_lambda"], p["g_diff"], layer_idx=l, **geom)
        o_c, s_c = _hgrn((main, COL_C_Q), (main, COL_C_F), (main, COL_C_I), (main, COL_C_G),
                         gamma_lb, p["g_hgrn"], s0, bsz=bsz, t=t, layer_idx=l)
        h = _outproj(h, o_a, o_b, o_c, p["w_out"])
        x = _ffn(h, p["g_ffn2"], p["w_ffn2_in"], p["w_ffn2_out"], g_final if l == depth - 1 else None)
        new.append((a_k.reshape(bsz, t, A_KV_HEADS, A_HEAD_DIM), a_v.reshape(bsz, t, A_KV_HEADS, A_HEAD_DIM),
                    a_ki.reshape(bsz, t, IDX_DIM), b_k.reshape(bsz, t, B_HEADS, 2 * B_QK_DIM),
                    b_v.reshape(bsz, t, B_HEADS, B_V_DIM), s_c))
    stacked = [jnp.stack([s[i] for s in new]) for i in range(6)]
    return x.reshape(bsz, t, d), stacked


def kernel(x_prompt, x_sample, cache_a_k, cache_a_v, cache_a_kidx, cache_b_k, cache_b_v, state_c,
           g_ffn1, w_ffn1_in, w_ffn1_out, g_mix, w_in, diff_lambda, g_diff, gamma_lb, g_hgrn, w_out,
           g_ffn2, w_ffn2_in, w_ffn2_out, g_final):
    depth = w_in.shape[0]
    params = []
    for l in range(depth):
        wm, ws = _split_w_in(w_in[l])
        params.append(dict(
            g_ffn1=g_ffn1[l], w_ffn1_in=w_ffn1_in[l].astype(BF16), w_ffn1_out=w_ffn1_out[l].astype(BF16),
            g_mix=g_mix[l], w_in_main=wm, w_in_small=ws, diff_lambda=diff_lambda[l], g_diff=g_diff[l],
            g_hgrn=g_hgrn[l], w_out=w_out[l].astype(BF16),
            g_ffn2=g_ffn2[l], w_ffn2_in=w_ffn2_in[l].astype(BF16), w_ffn2_out=w_ffn2_out[l].astype(BF16)))
    y_p, st_p = _run_group(x_prompt, None, params, g_final, gamma_lb)
    past = (cache_a_k, cache_a_v, cache_a_kidx, cache_b_k, cache_b_v, state_c)
    y_s, st_s = _run_group(x_sample, past, params, g_final, gamma_lb)
    return (y_p, y_s, *st_p, *st_s)
```

```python
import functools
import math

import jax
import jax.numpy as jnp
from jax import lax
from jax.experimental import pallas as pl
from jax.experimental.pallas import tpu as pltpu

F32 = jnp.float32
BF16 = jnp.bfloat16

CHUNK = 64
A_HEADS, A_KV_HEADS, A_HEAD_DIM = 4, 2, 128
IDX_HEADS, IDX_DIM = 8, 64
TOPK_MAX = 256
B_HEADS, B_QK_DIM, B_V_DIM = 4, 64, 128
C_HEADS, C_K_DIM, C_V_DIM = 8, 128, 128
NORM_EPS = 1e-6
NEG_BIG = -1e30
LB_FLOOR = 1e-30

A_Q_W = A_HEADS * A_HEAD_DIM
A_KV_W = A_KV_HEADS * A_HEAD_DIM
A_QI_W = IDX_HEADS * IDX_DIM
B_QK_W = B_HEADS * 2 * B_QK_DIM
B_V_W = B_HEADS * B_V_DIM
C_W = C_HEADS * C_K_DIM
MIX_W = A_Q_W + B_V_W + C_W
IN_SIZES = (A_Q_W, A_KV_W, A_KV_W, A_QI_W, IDX_DIM, IDX_HEADS, B_QK_W, B_QK_W, B_V_W, C_W, C_W, C_W, C_W)
MAIN_W = A_Q_W + 2 * A_KV_W + A_QI_W + 2 * B_QK_W + B_V_W + 4 * C_W
SMALL_W = 128
PROJ_TN = 1024
COL_A_Q, COL_A_K, COL_A_V, COL_A_QI = 0, 512, 768, 1024
COL_B_Q, COL_B_K, COL_B_V = 1536, 2048, 2560
COL_C_Q, COL_C_F, COL_C_I, COL_C_G = 3072, 4096, 5120, 6144
CACHE_WIDTHS = (A_KV_W, A_KV_W, IDX_DIM, B_QK_W, B_V_W)

V7X_LANES = 128
V7X_VMEM_LIMIT = 56 * 1024 * 1024


def _cparams(semantics):
    return pltpu.CompilerParams(dimension_semantics=semantics, vmem_limit_bytes=V7X_VMEM_LIMIT)


def _rmsnorm(x, g):
    return x * lax.rsqrt(jnp.mean(x * x, axis=-1, keepdims=True) + NORM_EPS) * g


def _dot(a, b):
    return jnp.dot(a, b, preferred_element_type=F32)


def _dot_nt(a, b):
    return lax.dot_general(a, b, (((1,), (1,)), ((), ())), preferred_element_type=F32)


def _dot_tn(a, b):
    return lax.dot_general(a, b, (((0,), (0,)), ((), ())), preferred_element_type=F32)


def _row_tile(n, pref):
    t = min(n, pref)
    while n % t:
        t //= 2
    return t


def _ffn_body(*refs, n_ff_steps, final_norm, emit):
    x_ref, g_ref, wg_ref, wu_ref, wo_ref = refs[:5]
    gf_ref = refs[5] if final_norm else None
    outs = refs[5 + final_norm:]
    o_ref, xn_ref = outs[0], outs[-1]
    j = pl.program_id(1)

    @pl.when(j == 0)
    def _():
        x = x_ref[...]
        xn_ref[...] = _rmsnorm(x, g_ref[...]).astype(BF16)
        o_ref[...] = x

    wg, wu, wo = wg_ref[...].astype(BF16), wu_ref[...].astype(BF16), wo_ref[...].astype(BF16)
    if emit:
        outs[1][...] = wg
        outs[2][...] = wu
        outs[3][...] = wo
    xn = xn_ref[...]
    gate = _dot(xn, wg)
    up = _dot(xn, wu)
    act = (gate * jax.nn.sigmoid(gate)) * up * 0.5
    o_ref[...] += _dot(act.astype(BF16), wo)

    if final_norm:
        @pl.when(j == n_ff_steps - 1)
        def _():
            o_ref[...] = _rmsnorm(o_ref[...], gf_ref[...])


def _ffn(x, g, w, g_final=None):
    n, d = x.shape
    emit = w[0] == "f32"
    if emit:
        _, w_in, w_out, layer = w
        f = w_out.shape[1]
    else:
        _, wg, wu, wo = w
        f = wo.shape[0]
    tm = _row_tile(n, 512)
    tf = 512 if f % 512 == 0 else f
    nj = f // tf
    assert not emit or n == tm, "the weight-emitting variant expects a single row tile"
    final_norm = g_final is not None
    in_specs = [pl.BlockSpec((tm, d), lambda i, j: (i, 0)), pl.BlockSpec((1, d), lambda i, j: (0, 0))]
    if emit:
        in_specs += [pl.BlockSpec((None, d, tf), lambda i, j: (layer, 0, j)),
                     pl.BlockSpec((None, d, tf), lambda i, j: (layer, 0, j + nj)),
                     pl.BlockSpec((None, tf, d), lambda i, j: (layer, j, 0))]
        args = [x, g.reshape(1, d), w_in, w_in, w_out]
    else:
        in_specs += [pl.BlockSpec((d, tf), lambda i, j: (0, j)), pl.BlockSpec((d, tf), lambda i, j: (0, j)),
                     pl.BlockSpec((tf, d), lambda i, j: (j, 0))]
        args = [x, g.reshape(1, d), wg, wu, wo]
    if final_norm:
        in_specs.append(pl.BlockSpec((1, d), lambda i, j: (0, 0)))
        args.append(g_final.reshape(1, d))
    out_specs = [pl.BlockSpec((tm, d), lambda i, j: (i, 0))]
    out_shape = [jax.ShapeDtypeStruct((n, d), F32)]
    if emit:
        out_specs += [pl.BlockSpec((d, tf), lambda i, j: (0, j)), pl.BlockSpec((d, tf), lambda i, j: (0, j)),
                      pl.BlockSpec((tf, d), lambda i, j: (j, 0))]
        out_shape += [jax.ShapeDtypeStruct((d, f), BF16), jax.ShapeDtypeStruct((d, f), BF16),
                      jax.ShapeDtypeStruct((f, d), BF16)]
    res = pl.pallas_call(
        functools.partial(_ffn_body, n_ff_steps=nj, final_norm=final_norm, emit=emit),
        grid=(n // tm, nj),
        in_specs=in_specs,
        out_specs=out_specs,
        out_shape=out_shape,
        scratch_shapes=[pltpu.VMEM((tm, d), BF16)],
        compiler_params=_cparams(("parallel", "arbitrary")),
        name="ffn_emit" if emit else "ffn",
    )(*args)
    return (res[0], ("bf16", res[1], res[2], res[3])) if emit else (res[0], w)


def _proj_body(*refs, emit):
    x_ref, g_ref, wm_ref, ws_ref = refs[:4]
    om_ref, os_ref, ak_ref, av_ref, aki_ref, bk_ref, bv_ref = refs[9:16]
    xn_ref = refs[-1]
    j = pl.program_id(1)

    @pl.when(j == 0)
    def _():
        xn = _rmsnorm(x_ref[...], g_ref[...]).astype(BF16)
        xn_ref[...] = xn
        ws = ws_ref[...].astype(BF16)
        if emit:
            refs[17][...] = ws
        small = _dot(xn, ws)
        os_ref[...] = small
        aki_ref[...] = small[:, 0:IDX_DIM]

    wm = wm_ref[...].astype(BF16)
    if emit:
        refs[16][...] = wm
    acc = _dot(xn_ref[...], wm)
    om_ref[...] = acc

    @pl.when(j == COL_A_K // PROJ_TN)
    def _():
        ak_ref[...] = acc[:, COL_A_K % PROJ_TN:COL_A_K % PROJ_TN + A_KV_W]
        av_ref[...] = acc[:, COL_A_V % PROJ_TN:COL_A_V % PROJ_TN + A_KV_W]

    @pl.when(j == COL_B_K // PROJ_TN)
    def _():
        bk_ref[...] = acc[:, COL_B_K % PROJ_TN:COL_B_K % PROJ_TN + B_QK_W]
        bv_ref[...] = acc[:, COL_B_V % PROJ_TN:COL_B_V % PROJ_TN + B_V_W]


def _proj(x, g, w, caches, layer):
    n, d = x.shape
    emit = w[0] == "f32"
    _, wm, ws = w
    tm = _row_tile(n, 512)
    tn = PROJ_TN
    assert not emit or n == tm, "the weight-emitting variant expects a single row tile"
    assert COL_A_K // tn == COL_A_V // tn and COL_B_K // tn == COL_B_V // tn == (COL_B_V + B_V_W - 1) // tn
    cache_spec = lambda wd: pl.BlockSpec((None, tm, wd), lambda i, j: (layer, i, 0))
    out_specs = [pl.BlockSpec((tm, tn), lambda i, j: (i, j)), pl.BlockSpec((tm, SMALL_W), lambda i, j: (i, 0))]
    out_specs += [cache_spec(wd) for wd in CACHE_WIDTHS]
    out_shape = [jax.ShapeDtypeStruct((n, MAIN_W), F32), jax.ShapeDtypeStruct((n, SMALL_W), F32)]
    out_shape += [jax.ShapeDtypeStruct(c.shape, F32) for c in caches]
    if emit:
        out_specs += [pl.BlockSpec((d, tn), lambda i, j: (0, j)), pl.BlockSpec((d, SMALL_W), lambda i, j: (0, 0))]
        out_shape += [jax.ShapeDtypeStruct((d, MAIN_W), BF16), jax.ShapeDtypeStruct((d, SMALL_W), BF16)]
    res = pl.pallas_call(
        functools.partial(_proj_body, emit=emit),
        grid=(n // tm, MAIN_W // tn),
        in_specs=[
            pl.BlockSpec((tm, d), lambda i, j: (i, 0)),
            pl.BlockSpec((1, d), lambda i, j: (0, 0)),
            pl.BlockSpec((d, tn), lambda i, j: (0, j)),
            pl.BlockSpec((d, SMALL_W), lambda i, j: (0, 0)),
        ] + [pl.BlockSpec(memory_space=pl.ANY)] * len(caches),
        out_specs=out_specs,
        out_shape=out_shape,
        input_output_aliases={4 + k: 2 + k for k in range(len(caches))},
        scratch_shapes=[pltpu.VMEM((tm, d), BF16)],
        compiler_params=_cparams(("parallel", "arbitrary")),
        name="in_proj_emit" if emit else "in_proj",
    )(x, g.reshape(1, d), wm, ws, *caches)
    w_bf = ("bf16", res[7], res[8]) if emit else w
    return res[0], res[1], tuple(res[2:7]), w_bf


def _sortable_key(x):
    bits = lax.bitcast_convert_type(x, jnp.int32)
    return jnp.where(bits < 0, bits ^ jnp.int32(0x7FFFFFFF), bits)


def _kth_largest(key, k):
    rows = key.shape[0]

    def body(it, t):
        cand = t + lax.shift_left(jnp.int32(1), jnp.int32(31) - it)
        cnt = jnp.sum(jnp.where(key >= cand, 1.0, 0.0), axis=1, keepdims=True)
        return jnp.where(cnt >= k, cand, t)

    t0 = jnp.full((rows, 1), jnp.iinfo(jnp.int32).min, jnp.int32)
    return lax.fori_loop(0, 32, body, t0)


def _tie_cut(tie, kpos, need, nbits):
    rows = tie.shape[0]

    def body(it, t):
        cand = t + lax.shift_left(jnp.int32(1), jnp.int32(nbits - 1) - it)
        cnt = jnp.sum(jnp.where(tie & (kpos < cand), 1.0, 0.0), axis=1, keepdims=True)
        return jnp.where(cnt < need, cand, t)

    return lax.fori_loop(0, nbits, body, jnp.zeros((rows, 1), jnp.int32))


def _visible(tq, s_pad, q_pos0, n_keys):
    qpos = q_pos0 + pl.program_id(1) * tq + lax.broadcasted_iota(jnp.int32, (tq, 1), 0)
    limit = jnp.minimum(((qpos >> 6) + 1) << 6, n_keys)
    kpos = lax.broadcasted_iota(jnp.int32, (tq, s_pad), 1)
    return kpos < limit, kpos


def _all_keys(new, past_ref, s_pad):
    if past_ref is None:
        assert new.shape[0] == s_pad
        return new.astype(BF16)
    parts = [past_ref[...], new]
    pad = s_pad - past_ref.shape[0] - new.shape[0]
    if pad:
        parts.append(jnp.zeros((pad, new.shape[1]), F32))
    return jnp.concatenate(parts, axis=0).astype(BF16)


def _dsa_body(*refs, tq, s_pad, n_keys, q_pos0, topk, has_past):
    q_ref, qi_ref, wq_ref, k_ref, v_ref, ki_ref = refs[:6]
    pk_ref, pv_ref, pki_ref = refs[6:9] if has_past else (None, None, None)
    o_ref = refs[-1]
    vis, kpos = _visible(tq, s_pad, q_pos0, n_keys)

    kidx = _all_keys(ki_ref[:, 0:IDX_DIM], pki_ref, s_pad)
    wq = wq_ref[:, IDX_DIM:IDX_DIM + IDX_HEADS] * (IDX_HEADS ** -0.5)
    score = jnp.zeros((tq, s_pad), F32)
    for h in range(IDX_HEADS):
        qi_h = qi_ref[:, h * IDX_DIM:(h + 1) * IDX_DIM].astype(BF16)
        logit = _dot_nt(qi_h, kidx) * (IDX_DIM ** -0.5)
        score = score + jnp.maximum(logit, 0.0) * wq[:, h:h + 1]
    score = jnp.where(vis, score + 0.0, NEG_BIG)

    key = _sortable_key(score)
    thr = _kth_largest(key, float(topk))
    n_gt = jnp.sum(jnp.where(key > thr, 1.0, 0.0), axis=1, keepdims=True)
    tie = key == thr
    cut = _tie_cut(tie, kpos, float(topk) - n_gt, max(1, (s_pad - 1).bit_length()))
    sel = ((key > thr) | (tie & (kpos <= cut))) & vis

    kb = _all_keys(k_ref[...], pk_ref, s_pad)
    vb = _all_keys(v_ref[...], pv_ref, s_pad)
    rep = A_HEADS // A_KV_HEADS
    for h in range(A_HEADS):
        g = h // rep
        q_h = q_ref[:, h * A_HEAD_DIM:(h + 1) * A_HEAD_DIM].astype(BF16)
        k_g = kb[:, g * A_HEAD_DIM:(g + 1) * A_HEAD_DIM]
        v_g = vb[:, g * A_HEAD_DIM:(g + 1) * A_HEAD_DIM]
        att = jnp.where(sel, _dot_nt(q_h, k_g) * (A_HEAD_DIM ** -0.5), NEG_BIG)
        p = jnp.exp(att - jnp.max(att, axis=1, keepdims=True))
        l = jnp.sum(p, axis=1, keepdims=True)
        o = _dot(p.astype(BF16), v_g) / l
        o_ref[:, h * A_HEAD_DIM:(h + 1) * A_HEAD_DIM] = o.astype(o_ref.dtype)


def _blk(arr_col, rows, width, kind, nq):
    _, col = arr_col
    cb = col // width
    assert col % width == 0
    if kind == "q":
        return pl.BlockSpec((rows, width), lambda b, i: (b * nq + i, cb))
    return pl.BlockSpec((rows, width), lambda b, i: (b, cb))


def _past_blk(cache, layer):
    return pl.BlockSpec((None, None) + cache.shape[2:], lambda b, i: (layer, b, 0, 0))


def _dsa(q, qi, wq, k, v, ki, past, *, bsz, t, s_pad, n_keys, q_pos0, tq, layer):
    nq = t // tq
    topk = min(TOPK_MAX, n_keys // 4)
    body = functools.partial(_dsa_body, tq=tq, s_pad=s_pad, n_keys=n_keys, q_pos0=q_pos0, topk=topk,
                             has_past=past is not None)
    in_specs = [
        _blk(q, tq, A_Q_W, "q", nq),
        _blk(qi, tq, A_QI_W, "q", nq),
        _blk(wq, tq, SMALL_W, "q", nq),
        _blk(k, t, A_KV_W, "kv", nq),
        _blk(v, t, A_KV_W, "kv", nq),
        _blk(ki, t, SMALL_W, "kv", nq),
    ]
    args = [q[0], qi[0], wq[0], k[0], v[0], ki[0]]
    if past is not None:
        in_specs += [_past_blk(c, layer) for c in past]
        args += list(past)
    return pl.pallas_call(
        body,
        grid=(bsz, nq),
        in_specs=in_specs,
        out_specs=pl.BlockSpec((tq, A_Q_W), lambda b, i: (b * nq + i, 0)),
        out_shape=jax.ShapeDtypeStruct((bsz * t, A_Q_W), BF16),
        compiler_params=_cparams(("parallel", "parallel")),
        name="dsa",
    )(*args)


def _diff_body(*refs, tq, s_pad, n_keys, q_pos0, lam_init, has_past):
    q_ref, k_ref, v_ref, lam_ref, g_ref = refs[:5]
    pk_ref, pv_ref = refs[5:7] if has_past else (None, None)
    o_ref = refs[-1]
    vis, _ = _visible(tq, s_pad, q_pos0, n_keys)
    lv = lam_ref[...]
    lam = (jnp.exp(jnp.sum(lv[0:1] * lv[1:2], axis=1, keepdims=True))
           - jnp.exp(jnp.sum(lv[2:3] * lv[3:4], axis=1, keepdims=True)) + lam_init)
    kb = _all_keys(k_ref[...], pk_ref, s_pad)
    vb = _all_keys(v_ref[...], pv_ref, s_pad)
    gain = g_ref[...] * (1.0 - lam_init)
    for h in range(B_HEADS):
        v_h = vb[:, h * B_V_DIM:(h + 1) * B_V_DIM]
        parts = []
        for c in range(2):
            lo = h * 2 * B_QK_DIM + c * B_QK_DIM
            q_c = q_ref[:, lo:lo + B_QK_DIM].astype(BF16)
            k_c = kb[:, lo:lo + B_QK_DIM]
            logit = jnp.where(vis, _dot_nt(q_c, k_c) * (B_QK_DIM ** -0.5), NEG_BIG)
            p = jnp.exp(logit - jnp.max(logit, axis=1, keepdims=True))
            l = jnp.sum(p, axis=1, keepdims=True)
            parts.append(_dot(p.astype(BF16), v_h) / l)
        o = parts[0] - lam * parts[1]
        o = o * lax.rsqrt(jnp.mean(o * o, axis=1, keepdims=True) + NORM_EPS) * gain
        o_ref[:, h * B_V_DIM:(h + 1) * B_V_DIM] = o.astype(o_ref.dtype)


def _diff(q, k, v, lam_vecs, g_diff, past, *, bsz, t, s_pad, n_keys, q_pos0, tq, layer):
    nq = t // tq
    lam_init = 0.8 - 0.6 * math.exp(-0.3 * layer)
    body = functools.partial(_diff_body, tq=tq, s_pad=s_pad, n_keys=n_keys, q_pos0=q_pos0, lam_init=lam_init,
                             has_past=past is not None)
    in_specs = [
        _blk(q, tq, B_QK_W, "q", nq),
        _blk(k, t, B_QK_W, "kv", nq),
        _blk(v, t, B_V_W, "kv", nq),
        pl.BlockSpec((None, 4, B_QK_DIM), lambda b, i: (layer, 0, 0)),
        pl.BlockSpec((None, 1, B_V_DIM), lambda b, i: (layer, 0, 0)),
    ]
    args = [q[0], k[0], v[0], lam_vecs, g_diff]
    if past is not None:
        in_specs += [_past_blk(c, layer) for c in past]
        args += list(past)
    return pl.pallas_call(
        body,
        grid=(bsz, nq),
        in_specs=in_specs,
        out_specs=pl.BlockSpec((tq, B_V_W), lambda b, i: (b * nq + i, 0)),
        out_shape=jax.ShapeDtypeStruct((bsz * t, B_V_W), BF16),
        compiler_params=_cparams(("parallel", "parallel")),
        name="diff_attn",
    )(*args)


def _cumsum_rows(x):
    n = x.shape[0]
    row = lax.broadcasted_iota(jnp.int32, (n, 1), 0)
    d = 1
    while d < n:
        x = x + jnp.where(row >= d, pltpu.roll(x, d, axis=0), 0.0)
        d *= 2
    return x


def _anchor_rows(b, c):
    n, w = b.shape
    if 2 * c >= 8:
        pieces = [jnp.broadcast_to(b[j * 2 * c + c - 1:j * 2 * c + c, :], (2 * c, w)) for j in range(n // (2 * c))]
        return pieces[0] if len(pieces) == 1 else jnp.concatenate(pieces, axis=0)
    rmod = lax.broadcasted_iota(jnp.int32, (n, 1), 0) & (2 * c - 1)
    m = b
    for s in range(-c, c):
        if s != 0:
            m = jnp.where(rmod == c - 1 - s, pltpu.roll(b, (-s) % n, axis=0), m)
    return m


def _hgrn_body(*refs, chunk, layer, has_s0):
    if has_s0:
        q_ref, f_ref, i_ref, g_ref, glb_ref, gn_ref, s0_ref, o_ref, s_ref, st_ref = refs
    else:
        q_ref, f_ref, i_ref, g_ref, glb_ref, gn_ref, o_ref, s_ref, st_ref = refs
    ci = pl.program_id(1)

    @pl.when(ci == 0)
    def _():
        for h in range(C_HEADS):
            st_ref[h] = s0_ref[h].T if has_s0 else jnp.zeros((C_V_DIM, C_K_DIM), F32)

    glb = glb_ref[...]
    pe = jnp.exp(glb - jnp.max(glb, axis=0, keepdims=True))
    pn = pe / jnp.sum(pe, axis=0, keepdims=True)
    lb = jnp.sum(pn[0:layer + 1], axis=0, keepdims=True) - pn[0:1]

    zf = f_ref[...]
    log_sig = jnp.minimum(zf, 0.0) - jnp.log1p(jnp.exp(-jnp.abs(zf)))
    la = jnp.log(jnp.maximum(lb, LB_FLOOR))
    lc = jnp.log1p(-lb) + log_sig
    hi = jnp.maximum(la, lc)
    log_f = hi + jnp.log1p(jnp.exp(jnp.minimum(la, lc) - hi))
    kc = (1.0 - lb) * jax.nn.sigmoid(-zf)
    b_all = _cumsum_rows(log_f)

    row = lax.broadcasted_iota(jnp.int32, (chunk, 1), 0)
    rr = lax.broadcasted_iota(jnp.int32, (chunk, chunk), 0)
    cc = lax.broadcasted_iota(jnp.int32, (chunk, chunk), 1)
    gn = gn_ref[...]

    for h in range(C_HEADS):
        sl = slice(h * C_K_DIM, (h + 1) * C_K_DIM)
        q = q_ref[:, sl]
        k = kc[:, sl]
        v = i_ref[:, sl]
        b = b_all[:, sl]
        vb = v.astype(BF16)
        st = st_ref[h]
        o = _dot_nt((q * jnp.exp(b)).astype(BF16), st.astype(BF16))
        o = o + jnp.sum(q * k, axis=1, keepdims=True) * v
        a = jnp.zeros((chunk, chunk), F32)
        c = chunk // 2
        while c >= 1:
            upper = ((row >> (c.bit_length() - 1)) & 1) == 1
            d = b - _anchor_rows(b, c)
            e = jnp.exp(jnp.where(upper, d, -d))
            qt = jnp.where(upper, q * e, 0.0).astype(BF16)
            kt = jnp.where(upper, 0.0, k * e).astype(BF16)
            sh = (2 * c).bit_length() - 1
            a = a + jnp.where((rr >> sh) == (cc >> sh), _dot_nt(qt, kt), 0.0)
            c //= 2
        o = o + _dot(a.astype(BF16), vb)
        b_last = b[chunk - 1:chunk, :]
        k_dec = (k * jnp.exp(b_last - b)).astype(BF16)
        st_new = st * jnp.exp(b_last) + _dot_tn(vb, k_dec)
        st_ref[h] = st_new

        o = o * lax.rsqrt(jnp.mean(o * o, axis=1, keepdims=True) + NORM_EPS) * gn
        zg = g_ref[:, sl]
        o_ref[:, sl] = (o * (zg * jax.nn.sigmoid(zg))).astype(o_ref.dtype)

        @pl.when(ci == pl.num_programs(1) - 1)
        def _():
            s_ref[0, h] = st_new.T


def _hgrn(cq, cf, cin, cg, gamma_lb, g_hgrn, s0, *, bsz, t, layer):
    chunk = min(CHUNK, t)
    nc = t // chunk
    depth = gamma_lb.shape[0]
    has_s0 = s0 is not None

    def seq_blk(arr_col):
        cb = arr_col[1] // C_W
        return pl.BlockSpec((chunk, C_W), lambda b, c: (b * nc + c, cb))

    in_specs = [seq_blk(cq), seq_blk(cf), seq_blk(cin), seq_blk(cg),
                pl.BlockSpec((depth, C_W), lambda b, c: (0, 0)),
                pl.BlockSpec((None, 1, C_V_DIM), lambda b, c: (layer, 0, 0))]
    args = [cq[0], cf[0], cin[0], cg[0], gamma_lb, g_hgrn]
    if has_s0:
        in_specs.append(pl.BlockSpec((None, None, C_HEADS, C_K_DIM, C_V_DIM), lambda b, c: (layer, b, 0, 0, 0)))
        args.append(s0)
    return pl.pallas_call(
        functools.partial(_hgrn_body, chunk=chunk, layer=layer, has_s0=has_s0),
        grid=(bsz, nc),
        in_specs=in_specs,
        out_specs=[pl.BlockSpec((chunk, C_W), lambda b, c: (b * nc + c, 0)),
                   pl.BlockSpec((1, C_HEADS, C_K_DIM, C_V_DIM), lambda b, c: (b, 0, 0, 0))],
        out_shape=[jax.ShapeDtypeStruct((bsz * t, C_W), BF16),
                   jax.ShapeDtypeStruct((bsz, C_HEADS, C_K_DIM, C_V_DIM), F32)],
        scratch_shapes=[pltpu.VMEM((C_HEADS, C_V_DIM, C_K_DIM), F32)],
        compiler_params=_cparams(("parallel", "arbitrary")),
        name="hgrn2",
    )(*args)


def _outproj_body(*refs, emit):
    h_ref, oa_ref, ob_ref, oc_ref, wa_ref, wb_ref, wc_ref, o_ref = refs[:8]
    wa, wb, wc = wa_ref[...].astype(BF16), wb_ref[...].astype(BF16), wc_ref[...].astype(BF16)
    if emit:
        refs[8][...] = wa
        refs[9][...] = wb
        refs[10][...] = wc
    o_ref[...] = h_ref[...] + (_dot(oa_ref[...], wa) + _dot(ob_ref[...], wb) + _dot(oc_ref[...], wc))


def _outproj(h, oa, ob, oc, w):
    n, d = h.shape
    emit = w[0] == "f32"
    tm = _row_tile(n, 512)
    assert not emit or n == tm, "the weight-emitting variant expects a single row tile"
    row = lambda wd: pl.BlockSpec((tm, wd), lambda i: (i, 0))
    whole = lambda r: pl.BlockSpec((r, d), lambda i: (0, 0))
    seg_rows = (A_Q_W, B_V_W, C_W)
    seg_first = (0, A_Q_W, A_Q_W + B_V_W)
    if emit:
        _, w_out, layer = w
        w_specs = [pl.BlockSpec((None, r, d), functools.partial(lambda i, blk: (layer, blk, 0), blk=f // r))
                   for r, f in zip(seg_rows, seg_first)]
        w_args = [w_out] * 3
    else:
        w_specs = [whole(r) for r in seg_rows]
        w_args = list(w[1:])
    out_specs = [row(d)]
    out_shape = [jax.ShapeDtypeStruct((n, d), F32)]
    if emit:
        out_specs += [whole(r) for r in seg_rows]
        out_shape += [jax.ShapeDtypeStruct((r, d), BF16) for r in seg_rows]
    res = pl.pallas_call(
        functools.partial(_outproj_body, emit=emit),
        grid=(n // tm,),
        in_specs=[row(d), row(A_Q_W), row(B_V_W), row(C_W)] + w_specs,
        out_specs=out_specs,
        out_shape=out_shape,
        compiler_params=_cparams(("parallel",)),
        name="out_proj_emit" if emit else "out_proj",
    )(h, oa, ob, oc, *w_args)
    return (res[0], ("bf16", res[1], res[2], res[3])) if emit else (res[0], w)


def _reorder_w_in(w):
    offs = [0]
    for s in IN_SIZES:
        offs.append(offs[-1] + s)
    seg = lambda i: w[:, offs[i]:offs[i + 1]]
    main = jnp.concatenate([seg(0), seg(1), seg(2), seg(3), seg(6), seg(7), seg(8), seg(9), seg(10), seg(11), seg(12)],
                           axis=1)
    small = jnp.concatenate([seg(4), seg(5), jnp.zeros((w.shape[0], SMALL_W - IDX_DIM - IDX_HEADS), w.dtype)], axis=1)
    return main, small


def _layer(x, layer, w, p, caches, past, geom):
    bsz, t = geom["bsz"], geom["t"]
    depth = p["g_ffn1"].shape[0]
    wb = {}
    h, wb["ffn1"] = _ffn(x, p["g_ffn1"][layer], w["ffn1"])
    main, small, caches, wb["proj"] = _proj(h, p["g_mix"][layer], w["proj"], caches, layer)
    if past is None:
        past_a = past_b = s0 = None
    else:
        past_a, past_b, s0 = past[0:3], past[3:5], past[5]
    o_a = _dsa((main, COL_A_Q), (main, COL_A_QI), (small, 0), (main, COL_A_K), (main, COL_A_V), (small, 0), past_a,
               layer=layer, **geom)
    o_b = _diff((main, COL_B_Q), (main, COL_B_K), (main, COL_B_V), p["diff_lambda"], p["g_diff"], past_b,
                layer=layer, **geom)
    o_c, s_c = _hgrn((main, COL_C_Q), (main, COL_C_F), (main, COL_C_I), (main, COL_C_G),
                     p["gamma_lb"], p["g_hgrn"], s0, bsz=bsz, t=t, layer=layer)
    h, wb["out"] = _outproj(h, o_a, o_b, o_c, w["out"])
    x, wb["ffn2"] = _ffn(h, p["g_ffn2"][layer], w["ffn2"], p["g_final"] if layer == depth - 1 else None)
    return x, caches, s_c, wb


def kernel(x_prompt, x_sample, cache_a_k, cache_a_v, cache_a_kidx, cache_b_k, cache_b_v, state_c,
           g_ffn1, w_ffn1_in, w_ffn1_out, g_mix, w_in, diff_lambda, g_diff, gamma_lb, g_hgrn, w_out,
           g_ffn2, w_ffn2_in, w_ffn2_out, g_final):
    depth, d = w_in.shape[0], w_in.shape[1]
    bp, tp, _ = x_prompt.shape
    bs, ts, _ = x_sample.shape
    p_len = cache_a_k.shape[2]
    p = dict(g_ffn1=g_ffn1, g_mix=g_mix, diff_lambda=diff_lambda, g_diff=g_diff.reshape(depth, 1, B_V_DIM),
             gamma_lb=gamma_lb, g_hgrn=g_hgrn.reshape(depth, 1, C_V_DIM), g_ffn2=g_ffn2, g_final=g_final)
    geom_p = dict(bsz=bp, t=tp, s_pad=tp, n_keys=tp, q_pos0=0, tq=_row_tile(tp, 128))
    n_keys_s = p_len + ts
    geom_s = dict(bsz=bs, t=ts, s_pad=-(-n_keys_s // V7X_LANES) * V7X_LANES, n_keys=n_keys_s, q_pos0=p_len, tq=ts)
    past = (cache_a_k.reshape(depth, bs, p_len, A_KV_W), cache_a_v.reshape(depth, bs, p_len, A_KV_W),
            cache_a_kidx, cache_b_k.reshape(depth, bs, p_len, B_QK_W), cache_b_v.reshape(depth, bs, p_len, B_V_W),
            state_c)

    xp = x_prompt.reshape(bp * tp, d)
    xs = x_sample.reshape(bs * ts, d)
    caches_p = tuple(jnp.zeros((depth, bp * tp, wd), F32) for wd in CACHE_WIDTHS)
    caches_s = tuple(jnp.zeros((depth, bs * ts, wd), F32) for wd in CACHE_WIDTHS)
    states_p, states_s = [], []
    for layer in range(depth):
        wm, ws = _reorder_w_in(w_in[layer])
        w_f32 = dict(ffn1=("f32", w_ffn1_in, w_ffn1_out, layer), proj=("f32", wm, ws), out=("f32", w_out, layer),
                     ffn2=("f32", w_ffn2_in, w_ffn2_out, layer))
        xs, caches_s, s_c, w_bf = _layer(xs, layer, w_f32, p, caches_s, past, geom_s)
        states_s.append(s_c)
        xp, caches_p, s_c, _ = _layer(xp, layer, w_bf, p, caches_p, None, geom_p)
        states_p.append(s_c)

    def group_out(x, caches, states, bsz, t):
        a_k, a_v, a_ki, b_k, b_v = caches
        return (x.reshape(bsz, t, d),
                a_k.reshape(depth, bsz, t, A_KV_HEADS, A_HEAD_DIM), a_v.reshape(depth, bsz, t, A_KV_HEADS, A_HEAD_DIM),
                a_ki.reshape(depth, bsz, t, IDX_DIM), b_k.reshape(depth, bsz, t, B_HEADS, 2 * B_QK_DIM),
                b_v.reshape(depth, bsz, t, B_HEADS, B_V_DIM), jnp.stack(states))

    out_p = group_out(xp, caches_p, states_p, bp, tp)
    out_s = group_out(xs, caches_s, states_s, bs, ts)
    return (out_p[0], out_s[0], *out_p[1:], *out_s[1:])
```

```python
import functools
import math

import jax
import jax.numpy as jnp
from jax import lax
from jax.experimental import pallas as pl
from jax.experimental.pallas import tpu as pltpu

F32 = jnp.float32
BF16 = jnp.bfloat16

CHUNK = 64
A_HEADS, A_KV_HEADS, A_HEAD_DIM = 4, 2, 128
IDX_HEADS, IDX_DIM = 8, 64
TOPK_MAX = 256
B_HEADS, B_QK_DIM, B_V_DIM = 4, 64, 128
C_HEADS, C_K_DIM, C_V_DIM = 8, 128, 128
NORM_EPS = 1e-6
NEG_BIG = -1e30
LB_FLOOR = 1e-30

A_Q_W = A_HEADS * A_HEAD_DIM
A_KV_W = A_KV_HEADS * A_HEAD_DIM
A_QI_W = IDX_HEADS * IDX_DIM
B_QK_W = B_HEADS * 2 * B_QK_DIM
B_V_W = B_HEADS * B_V_DIM
C_W = C_HEADS * C_K_DIM
MIX_W = A_Q_W + B_V_W + C_W
MAIN_W = A_Q_W + 2 * A_KV_W + A_QI_W + 2 * B_QK_W + B_V_W + 4 * C_W
NARROW_W = IDX_DIM + IDX_HEADS
SMALL_W = 128
PROJ_TN = 1024
COL_A_Q, COL_A_K, COL_A_V, COL_A_QI = 0, 512, 768, 1024
COL_B_Q, COL_B_K, COL_B_V = 1536, 2048, 2560
COL_C_Q, COL_C_F, COL_C_I, COL_C_G = 3072, 4096, 5120, 6144

V7X_LANES = 128
V7X_VMEM_LIMIT = 56 * 1024 * 1024


def _cparams(semantics):
    return pltpu.CompilerParams(dimension_semantics=semantics, vmem_limit_bytes=V7X_VMEM_LIMIT)


def _rmsnorm(x, g):
    return x * lax.rsqrt(jnp.mean(x * x, axis=-1, keepdims=True) + NORM_EPS) * g


def _dot(a, b):
    return jnp.dot(a, b, preferred_element_type=F32)


def _dot_nt(a, b):
    return lax.dot_general(a, b, (((1,), (1,)), ((), ())), preferred_element_type=F32)


def _dot_tn(a, b):
    return lax.dot_general(a, b, (((0,), (0,)), ((), ())), preferred_element_type=F32)


def _row_tile(n, pref):
    t = min(n, pref)
    while n % t:
        t //= 2
    return t


def _ffn_body(*refs, n_ff_steps, final_norm, emit):
    x_ref, g_ref, wg_ref, wu_ref, wo_ref = refs[:5]
    gf_ref = refs[5] if final_norm else None
    outs = refs[5 + final_norm:]
    o_ref, xn_ref = outs[0], outs[-1]
    j = pl.program_id(1)

    @pl.when(j == 0)
    def _():
        x = x_ref[...]
        xn_ref[...] = _rmsnorm(x, g_ref[...]).astype(BF16)
        o_ref[...] = x

    wg, wu, wo = wg_ref[...].astype(BF16), wu_ref[...].astype(BF16), wo_ref[...].astype(BF16)
    if emit:
        outs[1][...] = wg
        outs[2][...] = wu
        outs[3][...] = wo
    xn = xn_ref[...]
    gate = _dot(xn, wg)
    up = _dot(xn, wu)
    act = (gate * jax.nn.sigmoid(gate)) * up * 0.5
    o_ref[...] += _dot(act.astype(BF16), wo)

    if final_norm:
        @pl.when(j == n_ff_steps - 1)
        def _():
            o_ref[...] = _rmsnorm(o_ref[...], gf_ref[...])


def _ffn(x, g, w, g_final=None):
    n, d = x.shape
    emit = w[0] == "f32"
    if emit:
        _, w_in, w_out, layer = w
        f = w_out.shape[1]
    else:
        _, wg, wu, wo = w
        f = wo.shape[0]
    tm = _row_tile(n, 512)
    tf = 512 if f % 512 == 0 else f
    nj = f // tf
    assert not emit or n == tm, "the weight-emitting variant expects a single row tile"
    final_norm = g_final is not None
    in_specs = [pl.BlockSpec((tm, d), lambda i, j: (i, 0)), pl.BlockSpec((1, d), lambda i, j: (0, 0))]
    if emit:
        in_specs += [pl.BlockSpec((None, d, tf), lambda i, j: (layer, 0, j)),
                     pl.BlockSpec((None, d, tf), lambda i, j: (layer, 0, j + nj)),
                     pl.BlockSpec((None, tf, d), lambda i, j: (layer, j, 0))]
        args = [x, g.reshape(1, d), w_in, w_in, w_out]
    else:
        in_specs += [pl.BlockSpec((d, tf), lambda i, j: (0, j)), pl.BlockSpec((d, tf), lambda i, j: (0, j)),
                     pl.BlockSpec((tf, d), lambda i, j: (j, 0))]
        args = [x, g.reshape(1, d), wg, wu, wo]
    if final_norm:
        in_specs.append(pl.BlockSpec((1, d), lambda i, j: (0, 0)))
        args.append(g_final.reshape(1, d))
    out_specs = [pl.BlockSpec((tm, d), lambda i, j: (i, 0))]
    out_shape = [jax.ShapeDtypeStruct((n, d), F32)]
    if emit:
        out_specs += [pl.BlockSpec((d, tf), lambda i, j: (0, j)), pl.BlockSpec((d, tf), lambda i, j: (0, j)),
                      pl.BlockSpec((tf, d), lambda i, j: (j, 0))]
        out_shape += [jax.ShapeDtypeStruct((d, f), BF16), jax.ShapeDtypeStruct((d, f), BF16),
                      jax.ShapeDtypeStruct((f, d), BF16)]
    res = pl.pallas_call(
        functools.partial(_ffn_body, n_ff_steps=nj, final_norm=final_norm, emit=emit),
        grid=(n // tm, nj),
        in_specs=in_specs,
        out_specs=out_specs,
        out_shape=out_shape,
        scratch_shapes=[pltpu.VMEM((tm, d), BF16)],
        compiler_params=_cparams(("parallel", "arbitrary")),
        name="ffn_emit" if emit else "ffn",
    )(*args)
    return (res[0], ("bf16", res[1], res[2], res[3])) if emit else (res[0], w)


def _proj_body(*refs, emit, tn, tm, n_alias, layer):
    n_w = 3 if emit else 2
    x_ref, g_ref = refs[:2]
    w_refs = refs[2:2 + n_w]
    outs = refs[2 + n_w + n_alias:]
    om_ref, os_ref, ak_ref, av_ref, aki_ref, bk_ref, bv_ref = outs[:7]
    xn_ref = refs[-1]
    j = pl.program_id(1)

    def this_layer(ref):
        if n_alias:
            return ref
        for other in range(ref.shape[0]):
            if other != layer:
                ref[other] = jnp.zeros(ref.shape[1:], F32)
        return ref.at[layer]

    if emit:
        wa_ref, wx_ref, wsm_ref = w_refs
        wmo_ref, wso_ref = outs[7:9]
        n_aligned = COL_B_Q // tn

        @pl.when(j < n_aligned)
        def _():
            wmo_ref[...] = wa_ref[...].astype(BF16)

        @pl.when(j >= n_aligned)
        def _():
            wide = jnp.concatenate([wa_ref[...], wx_ref[...]], axis=1)
            wmo_ref[...] = wide[:, NARROW_W:NARROW_W + tn].astype(BF16)

        wm = wmo_ref[...]
    else:
        wm = w_refs[0][...]

    @pl.when(j == 0)
    def _():
        xn = _rmsnorm(x_ref[...], g_ref[...]).astype(BF16)
        xn_ref[...] = xn
        if emit:
            lane = lax.broadcasted_iota(jnp.int32, wsm_ref.shape, 1)
            ws = jnp.where(lane < NARROW_W, wsm_ref[...], 0.0).astype(BF16)
            wso_ref[...] = ws
        else:
            ws = w_refs[1][...]
        small = _dot(xn, ws)
        os_ref[...] = small
        this_layer(aki_ref)[...] = small[:, 0:IDX_DIM]

    acc = _dot(xn_ref[...], wm)
    om_ref[...] = acc

    def put_heads(ref, col, heads):
        off = col % tn
        assert off + heads * V7X_LANES <= tn

        @pl.when(j == col // tn)
        def _():
            dst = this_layer(ref)
            for hd in range(heads):
                dst[pl.ds(hd, tm, stride=heads), :] = acc[:, off + hd * V7X_LANES:off + (hd + 1) * V7X_LANES]

    put_heads(ak_ref, COL_A_K, A_KV_HEADS)
    put_heads(av_ref, COL_A_V, A_KV_HEADS)
    put_heads(bk_ref, COL_B_K, B_HEADS)
    put_heads(bv_ref, COL_B_V, B_HEADS)


def _proj(x, g, w, caches, layer, depth):
    n, d = x.shape
    emit = w[0] == "f32"
    tm = _row_tile(n, 512)
    tn = 512 if emit else PROJ_TN
    assert not emit or n == tm, "the weight-emitting variant expects a single row tile"
    lanes_per_tn = tn // V7X_LANES
    if emit:
        assert COL_B_Q % tn == 0
        w_in = w[1]
        w_specs = [pl.BlockSpec((None, d, tn), lambda i, j: (layer, 0, j)),
                   pl.BlockSpec((None, d, V7X_LANES), lambda i, j: (layer, 0, lanes_per_tn * (j + 1))),
                   pl.BlockSpec((None, d, SMALL_W), lambda i, j: (layer, 0, COL_B_Q // SMALL_W))]
        w_args = [w_in, w_in, w_in]
    else:
        w_specs = [pl.BlockSpec((d, tn), lambda i, j: (0, j)), pl.BlockSpec((d, SMALL_W), lambda i, j: (0, 0))]
        w_args = list(w[1:])
    cache_rows = (A_KV_HEADS, A_KV_HEADS, 1, B_HEADS, B_HEADS)
    cache_lanes = (V7X_LANES, V7X_LANES, IDX_DIM, V7X_LANES, V7X_LANES)
    alias_args = [] if caches is None else list(caches)
    out_specs = [pl.BlockSpec((tm, tn), lambda i, j: (i, j)), pl.BlockSpec((tm, SMALL_W), lambda i, j: (i, 0))]
    if alias_args:
        cache_spec = lambda r, wd: pl.BlockSpec((None, r * tm, wd), lambda i, j: (layer, i, 0))
    else:
        cache_spec = lambda r, wd: pl.BlockSpec((depth, r * tm, wd), lambda i, j: (0, i, 0))
    out_specs += [cache_spec(r, wd) for r, wd in zip(cache_rows, cache_lanes)]
    out_shape = [jax.ShapeDtypeStruct((n, MAIN_W), F32), jax.ShapeDtypeStruct((n, SMALL_W), F32)]
    out_shape += [jax.ShapeDtypeStruct((depth, r * n, wd), F32) for r, wd in zip(cache_rows, cache_lanes)]
    if emit:
        out_specs += [pl.BlockSpec((d, tn), lambda i, j: (0, j)), pl.BlockSpec((d, SMALL_W), lambda i, j: (0, 0))]
        out_shape += [jax.ShapeDtypeStruct((d, MAIN_W), BF16), jax.ShapeDtypeStruct((d, SMALL_W), BF16)]
    n_in = 2 + len(w_args)
    res = pl.pallas_call(
        functools.partial(_proj_body, emit=emit, tn=tn, tm=tm, n_alias=len(alias_args), layer=layer),
        grid=(n // tm, MAIN_W // tn),
        in_specs=[pl.BlockSpec((tm, d), lambda i, j: (i, 0)), pl.BlockSpec((1, d), lambda i, j: (0, 0))]
        + w_specs + [pl.BlockSpec(memory_space=pl.ANY)] * len(alias_args),
        out_specs=out_specs,
        out_shape=out_shape,
        input_output_aliases={n_in + k: 2 + k for k in range(len(alias_args))},
        scratch_shapes=[pltpu.VMEM((tm, d), BF16)],
        compiler_params=_cparams(("parallel", "arbitrary")),
        name="in_proj_emit" if emit else "in_proj",
    )(x, g.reshape(1, d), *w_args, *alias_args)
    w_bf = ("bf16", res[7], res[8]) if emit else w
    return res[0], res[1], tuple(res[2:7]), w_bf


def _sortable_key(x):
    bits = lax.bitcast_convert_type(x, jnp.int32)
    return jnp.where(bits < 0, bits ^ jnp.int32(0x7FFFFFFF), bits)


def _kth_largest(key, k):
    rows = key.shape[0]

    def body(it, t):
        cand = t + lax.shift_left(jnp.int32(1), jnp.int32(31) - it)
        cnt = jnp.sum(jnp.where(key >= cand, 1.0, 0.0), axis=1, keepdims=True)
        return jnp.where(cnt >= k, cand, t)

    t0 = jnp.full((rows, 1), jnp.iinfo(jnp.int32).min, jnp.int32)
    return lax.fori_loop(0, 32, body, t0)


def _tie_cut(tie, kpos, need, nbits):
    rows = tie.shape[0]

    def body(it, t):
        cand = t + lax.shift_left(jnp.int32(1), jnp.int32(nbits - 1) - it)
        cnt = jnp.sum(jnp.where(tie & (kpos < cand), 1.0, 0.0), axis=1, keepdims=True)
        return jnp.where(cnt < need, cand, t)

    return lax.fori_loop(0, nbits, body, jnp.zeros((rows, 1), jnp.int32))


def _visible(tq, s_pad, q_pos0, n_keys):
    qpos = q_pos0 + pl.program_id(1) * tq + lax.broadcasted_iota(jnp.int32, (tq, 1), 0)
    limit = jnp.minimum(((qpos >> 6) + 1) << 6, n_keys)
    kpos = lax.broadcasted_iota(jnp.int32, (tq, s_pad), 1)
    return kpos < limit, kpos


def _head_keys(new_ref, lo, width, past_ref, head, heads, s_pad):
    new = new_ref[:, lo:lo + width]
    if past_ref is None:
        assert new.shape[0] == s_pad
        return new.astype(BF16)
    p_len = past_ref.shape[0] // heads
    hist = past_ref[...] if heads == 1 else past_ref[pl.ds(head, p_len, stride=heads), :]
    parts = [hist, new]
    pad = s_pad - p_len - new.shape[0]
    if pad:
        parts.append(jnp.zeros((pad, width), F32))
    return jnp.concatenate(parts, axis=0).astype(BF16)


def _dsa_body(*refs, tq, s_pad, n_keys, q_pos0, topk, has_past):
    q_ref, qi_ref, wq_ref, k_ref, v_ref, ki_ref = refs[:6]
    pk_ref, pv_ref, pki_ref = refs[6:9] if has_past else (None, None, None)
    o_ref = refs[-1]
    vis, kpos = _visible(tq, s_pad, q_pos0, n_keys)

    kidx = _head_keys(ki_ref, 0, IDX_DIM, pki_ref, 0, 1, s_pad)
    wq = wq_ref[:, IDX_DIM:IDX_DIM + IDX_HEADS] * (IDX_HEADS ** -0.5)
    score = jnp.zeros((tq, s_pad), F32)
    for h in range(IDX_HEADS):
        qi_h = qi_ref[:, h * IDX_DIM:(h + 1) * IDX_DIM].astype(BF16)
        logit = _dot_nt(qi_h, kidx) * (IDX_DIM ** -0.5)
        score = score + jnp.maximum(logit, 0.0) * wq[:, h:h + 1]
    score = jnp.where(vis, score + 0.0, NEG_BIG)

    key = _sortable_key(score)
    thr = _kth_largest(key, float(topk))
    n_gt = jnp.sum(jnp.where(key > thr, 1.0, 0.0), axis=1, keepdims=True)
    tie = key == thr
    cut = _tie_cut(tie, kpos, float(topk) - n_gt, max(1, (s_pad - 1).bit_length()))
    sel = ((key > thr) | (tie & (kpos <= cut))) & vis

    rep = A_HEADS // A_KV_HEADS
    for g in range(A_KV_HEADS):
        k_g = _head_keys(k_ref, g * A_HEAD_DIM, A_HEAD_DIM, pk_ref, g, A_KV_HEADS, s_pad)
        v_g = _head_keys(v_ref, g * A_HEAD_DIM, A_HEAD_DIM, pv_ref, g, A_KV_HEADS, s_pad)
        for h in range(g * rep, (g + 1) * rep):
            q_h = q_ref[:, h * A_HEAD_DIM:(h + 1) * A_HEAD_DIM].astype(BF16)
            att = jnp.where(sel, _dot_nt(q_h, k_g) * (A_HEAD_DIM ** -0.5), NEG_BIG)
            p = jnp.exp(att - jnp.max(att, axis=1, keepdims=True))
            l = jnp.sum(p, axis=1, keepdims=True)
            o = _dot(p.astype(BF16), v_g) / l
            o_ref[:, h * A_HEAD_DIM:(h + 1) * A_HEAD_DIM] = o.astype(o_ref.dtype)


def _blk(arr_col, rows, width, kind, nq):
    _, col = arr_col
    cb = col // width
    assert col % width == 0
    if kind == "q":
        return pl.BlockSpec((rows, width), lambda b, i: (b * nq + i, cb))
    return pl.BlockSpec((rows, width), lambda b, i: (b, cb))


def _past_blk(cache, layer):
    return pl.BlockSpec((None, None) + cache.shape[2:], lambda b, i: (layer, b, 0, 0))


def _dsa(q, qi, wq, k, v, ki, past, *, bsz, t, s_pad, n_keys, q_pos0, tq, layer):
    nq = t // tq
    topk = min(TOPK_MAX, n_keys // 4)
    body = functools.partial(_dsa_body, tq=tq, s_pad=s_pad, n_keys=n_keys, q_pos0=q_pos0, topk=topk,
                             has_past=past is not None)
    in_specs = [
        _blk(q, tq, A_Q_W, "q", nq),
        _blk(qi, tq, A_QI_W, "q", nq),
        _blk(wq, tq, SMALL_W, "q", nq),
        _blk(k, t, A_KV_W, "kv", nq),
        _blk(v, t, A_KV_W, "kv", nq),
        _blk(ki, t, SMALL_W, "kv", nq),
    ]
    args = [q[0], qi[0], wq[0], k[0], v[0], ki[0]]
    if past is not None:
        in_specs += [_past_blk(c, layer) for c in past]
        args += list(past)
    return pl.pallas_call(
        body,
        grid=(bsz, nq),
        in_specs=in_specs,
        out_specs=pl.BlockSpec((tq, A_Q_W), lambda b, i: (b * nq + i, 0)),
        out_shape=jax.ShapeDtypeStruct((bsz * t, A_Q_W), BF16),
        compiler_params=_cparams(("parallel", "parallel")),
        name="dsa",
    )(*args)


def _diff_body(*refs, tq, s_pad, n_keys, q_pos0, lam_init, has_past):
    q_ref, k_ref, v_ref, lam_ref, g_ref = refs[:5]
    pk_ref, pv_ref = refs[5:7] if has_past else (None, None)
    o_ref = refs[-1]
    vis, _ = _visible(tq, s_pad, q_pos0, n_keys)
    lv = lam_ref[...]
    lam = (jnp.exp(jnp.sum(lv[0:1] * lv[1:2], axis=1, keepdims=True))
           - jnp.exp(jnp.sum(lv[2:3] * lv[3:4], axis=1, keepdims=True)) + lam_init)
    gain = g_ref[...] * (1.0 - lam_init)
    head_w = 2 * B_QK_DIM
    lane = lax.broadcasted_iota(jnp.int32, (tq, head_w), 1)
    for h in range(B_HEADS):
        k_h = _head_keys(k_ref, h * head_w, head_w, pk_ref, h, B_HEADS, s_pad)
        v_h = _head_keys(v_ref, h * B_V_DIM, B_V_DIM, pv_ref, h, B_HEADS, s_pad)
        q_h = q_ref[:, h * head_w:(h + 1) * head_w]
        parts = []
        for c in range(2):
            q_c = jnp.where((lane >= c * B_QK_DIM) & (lane < (c + 1) * B_QK_DIM), q_h, 0.0).astype(BF16)
            logit = jnp.where(vis, _dot_nt(q_c, k_h) * (B_QK_DIM ** -0.5), NEG_BIG)
            p = jnp.exp(logit - jnp.max(logit, axis=1, keepdims=True))
            l = jnp.sum(p, axis=1, keepdims=True)
            parts.append(_dot(p.astype(BF16), v_h) / l)
        o = parts[0] - lam * parts[1]
        o = o * lax.rsqrt(jnp.mean(o * o, axis=1, keepdims=True) + NORM_EPS) * gain
        o_ref[:, h * B_V_DIM:(h + 1) * B_V_DIM] = o.astype(o_ref.dtype)


def _diff(q, k, v, lam_vecs, g_diff, past, *, bsz, t, s_pad, n_keys, q_pos0, tq, layer):
    nq = t // tq
    lam_init = 0.8 - 0.6 * math.exp(-0.3 * layer)
    body = functools.partial(_diff_body, tq=tq, s_pad=s_pad, n_keys=n_keys, q_pos0=q_pos0, lam_init=lam_init,
                             has_past=past is not None)
    in_specs = [
        _blk(q, tq, B_QK_W, "q", nq),
        _blk(k, t, B_QK_W, "kv", nq),
        _blk(v, t, B_V_W, "kv", nq),
        pl.BlockSpec((None, 4, B_QK_DIM), lambda b, i: (layer, 0, 0)),
        pl.BlockSpec((None, 1, B_V_DIM), lambda b, i: (layer, 0, 0)),
    ]
    args = [q[0], k[0], v[0], lam_vecs, g_diff]
    if past is not None:
        in_specs += [_past_blk(c, layer) for c in past]
        args += list(past)
    return pl.pallas_call(
        body,
        grid=(bsz, nq),
        in_specs=in_specs,
        out_specs=pl.BlockSpec((tq, B_V_W), lambda b, i: (b * nq + i, 0)),
        out_shape=jax.ShapeDtypeStruct((bsz * t, B_V_W), BF16),
        compiler_params=_cparams(("parallel", "parallel")),
        name="diff_attn",
    )(*args)


def _cumsum_rows(x):
    n = x.shape[0]
    row = lax.broadcasted_iota(jnp.int32, (n, 1), 0)
    d = 1
    while d < n:
        x = x + jnp.where(row >= d, pltpu.roll(x, d, axis=0), 0.0)
        d *= 2
    return x


def _anchor_rows(b, c):
    n, w = b.shape
    if 2 * c >= 8:
        pieces = [jnp.broadcast_to(b[j * 2 * c + c - 1:j * 2 * c + c, :], (2 * c, w)) for j in range(n // (2 * c))]
        return pieces[0] if len(pieces) == 1 else jnp.concatenate(pieces, axis=0)
    rmod = lax.broadcasted_iota(jnp.int32, (n, 1), 0) & (2 * c - 1)
    m = b
    for s in range(-c, c):
        if s != 0:
            m = jnp.where(rmod == c - 1 - s, pltpu.roll(b, (-s) % n, axis=0), m)
    return m


def _hgrn_body(*refs, chunk, layer, has_s0):
    if has_s0:
        q_ref, f_ref, i_ref, g_ref, glb_ref, gn_ref, s0_ref, o_ref, s_ref, st_ref = refs
    else:
        q_ref, f_ref, i_ref, g_ref, glb_ref, gn_ref, o_ref, s_ref, st_ref = refs
    ci = pl.program_id(1)

    @pl.when(ci == 0)
    def _():
        for h in range(C_HEADS):
            st_ref[h] = s0_ref[h].T if has_s0 else jnp.zeros((C_V_DIM, C_K_DIM), F32)

    glb = glb_ref[...]
    pe = jnp.exp(glb - jnp.max(glb, axis=0, keepdims=True))
    pn = pe / jnp.sum(pe, axis=0, keepdims=True)
    lb = jnp.sum(pn[0:layer + 1], axis=0, keepdims=True) - pn[0:1]

    zf = f_ref[...]
    log_sig = jnp.minimum(zf, 0.0) - jnp.log1p(jnp.exp(-jnp.abs(zf)))
    la = jnp.log(jnp.maximum(lb, LB_FLOOR))
    lc = jnp.log1p(-lb) + log_sig
    hi = jnp.maximum(la, lc)
    log_f = hi + jnp.log1p(jnp.exp(jnp.minimum(la, lc) - hi))
    kc = (1.0 - lb) * jax.nn.sigmoid(-zf)
    b_all = _cumsum_rows(log_f)

    row = lax.broadcasted_iota(jnp.int32, (chunk, 1), 0)
    rr = lax.broadcasted_iota(jnp.int32, (chunk, chunk), 0)
    cc = lax.broadcasted_iota(jnp.int32, (chunk, chunk), 1)
    gn = gn_ref[...]

    for h in range(C_HEADS):
        sl = slice(h * C_K_DIM, (h + 1) * C_K_DIM)
        q = q_ref[:, sl]
        k = kc[:, sl]
        v = i_ref[:, sl]
        b = b_all[:, sl]
        vb = v.astype(BF16)
        st = st_ref[h]
        o = _dot_nt((q * jnp.exp(b)).astype(BF16), st.astype(BF16))
        o = o + jnp.sum(q * k, axis=1, keepdims=True) * v
        a = jnp.zeros((chunk, chunk), F32)
        c = chunk // 2
        while c >= 1:
            upper = ((row >> (c.bit_length() - 1)) & 1) == 1
            d = b - _anchor_rows(b, c)
            e = jnp.exp(jnp.where(upper, d, -d))
            qt = jnp.where(upper, q * e, 0.0).astype(BF16)
            kt = jnp.where(upper, 0.0, k * e).astype(BF16)
            sh = (2 * c).bit_length() - 1
            a = a + jnp.where((rr >> sh) == (cc >> sh), _dot_nt(qt, kt), 0.0)
            c //= 2
        o = o + _dot(a.astype(BF16), vb)
        b_last = b[chunk - 1:chunk, :]
        k_dec = (k * jnp.exp(b_last - b)).astype(BF16)
        st_new = st * jnp.exp(b_last) + _dot_tn(vb, k_dec)
        st_ref[h] = st_new

        o = o * lax.rsqrt(jnp.mean(o * o, axis=1, keepdims=True) + NORM_EPS) * gn
        zg = g_ref[:, sl]
        o_ref[:, sl] = (o * (zg * jax.nn.sigmoid(zg))).astype(o_ref.dtype)

        @pl.when(ci == pl.num_programs(1) - 1)
        def _():
            s_ref[0, h] = st_new.T


def _hgrn(cq, cf, cin, cg, gamma_lb, g_hgrn, s0, *, bsz, t, layer):
    chunk = min(CHUNK, t)
    nc = t // chunk
    depth = gamma_lb.shape[0]
    has_s0 = s0 is not None

    def seq_blk(arr_col):
        cb = arr_col[1] // C_W
        return pl.BlockSpec((chunk, C_W), lambda b, c: (b * nc + c, cb))

    in_specs = [seq_blk(cq), seq_blk(cf), seq_blk(cin), seq_blk(cg),
                pl.BlockSpec((depth, C_W), lambda b, c: (0, 0)),
                pl.BlockSpec((None, 1, C_V_DIM), lambda b, c: (layer, 0, 0))]
    args = [cq[0], cf[0], cin[0], cg[0], gamma_lb, g_hgrn]
    if has_s0:
        in_specs.append(pl.BlockSpec((None, None, C_HEADS, C_K_DIM, C_V_DIM), lambda b, c: (layer, b, 0, 0, 0)))
        args.append(s0)
    return pl.pallas_call(
        functools.partial(_hgrn_body, chunk=chunk, layer=layer, has_s0=has_s0),
        grid=(bsz, nc),
        in_specs=in_specs,
        out_specs=[pl.BlockSpec((chunk, C_W), lambda b, c: (b * nc + c, 0)),
                   pl.BlockSpec((1, C_HEADS, C_K_DIM, C_V_DIM), lambda b, c: (b, 0, 0, 0))],
        out_shape=[jax.ShapeDtypeStruct((bsz * t, C_W), BF16),
                   jax.ShapeDtypeStruct((bsz, C_HEADS, C_K_DIM, C_V_DIM), F32)],
        scratch_shapes=[pltpu.VMEM((C_HEADS, C_V_DIM, C_K_DIM), F32)],
        compiler_params=_cparams(("parallel", "arbitrary")),
        name="hgrn2",
    )(*args)


def _outproj_body(*refs, emit):
    h_ref, oa_ref, ob_ref, oc_ref, wa_ref, wb_ref, wc_ref, o_ref = refs[:8]
    wa, wb, wc = wa_ref[...].astype(BF16), wb_ref[...].astype(BF16), wc_ref[...].astype(BF16)
    if emit:
        refs[8][...] = wa
        refs[9][...] = wb
        refs[10][...] = wc
    o_ref[...] = h_ref[...] + (_dot(oa_ref[...], wa) + _dot(ob_ref[...], wb) + _dot(oc_ref[...], wc))


def _outproj(h, oa, ob, oc, w):
    n, d = h.shape
    emit = w[0] == "f32"
    tm = _row_tile(n, 512)
    assert not emit or n == tm, "the weight-emitting variant expects a single row tile"
    row = lambda wd: pl.BlockSpec((tm, wd), lambda i: (i, 0))
    whole = lambda r: pl.BlockSpec((r, d), lambda i: (0, 0))
    seg_rows = (A_Q_W, B_V_W, C_W)
    seg_first = (0, A_Q_W, A_Q_W + B_V_W)
    if emit:
        _, w_out, layer = w
        w_specs = [pl.BlockSpec((None, r, d), functools.partial(lambda i, blk: (layer, blk, 0), blk=f // r))
                   for r, f in zip(seg_rows, seg_first)]
        w_args = [w_out] * 3
    else:
        w_specs = [whole(r) for r in seg_rows]
        w_args = list(w[1:])
    out_specs = [row(d)]
    out_shape = [jax.ShapeDtypeStruct((n, d), F32)]
    if emit:
        out_specs += [whole(r) for r in seg_rows]
        out_shape += [jax.ShapeDtypeStruct((r, d), BF16) for r in seg_rows]
    res = pl.pallas_call(
        functools.partial(_outproj_body, emit=emit),
        grid=(n // tm,),
        in_specs=[row(d), row(A_Q_W), row(B_V_W), row(C_W)] + w_specs,
        out_specs=out_specs,
        out_shape=out_shape,
        compiler_params=_cparams(("parallel",)),
        name="out_proj_emit" if emit else "out_proj",
    )(h, oa, ob, oc, *w_args)
    return (res[0], ("bf16", res[1], res[2], res[3])) if emit else (res[0], w)


def _layer(x, layer, w, p, caches, past, geom):
    bsz, t = geom["bsz"], geom["t"]
    depth = p["g_ffn1"].shape[0]
    wb = {}
    h, wb["ffn1"] = _ffn(x, p["g_ffn1"][layer], w["ffn1"])
    main, small, caches, wb["proj"] = _proj(h, p["g_mix"][layer], w["proj"], caches, layer, depth)
    if past is None:
        past_a = past_b = s0 = None
    else:
        past_a, past_b, s0 = past[0:3], past[3:5], past[5]
    o_a = _dsa((main, COL_A_Q), (main, COL_A_QI), (small, 0), (main, COL_A_K), (main, COL_A_V), (small, 0), past_a,
               layer=layer, **geom)
    o_b = _diff((main, COL_B_Q), (main, COL_B_K), (main, COL_B_V), p["diff_lambda"], p["g_diff"], past_b,
                layer=layer, **geom)
    o_c, s_c = _hgrn((main, COL_C_Q), (main, COL_C_F), (main, COL_C_I), (main, COL_C_G),
                     p["gamma_lb"], p["g_hgrn"], s0, bsz=bsz, t=t, layer=layer)
    h, wb["out"] = _outproj(h, o_a, o_b, o_c, w["out"])
    x, wb["ffn2"] = _ffn(h, p["g_ffn2"][layer], w["ffn2"], p["g_final"] if layer == depth - 1 else None)
    return x, caches, s_c, wb


def kernel(x_prompt, x_sample, cache_a_k, cache_a_v, cache_a_kidx, cache_b_k, cache_b_v, state_c,
           g_ffn1, w_ffn1_in, w_ffn1_out, g_mix, w_in, diff_lambda, g_diff, gamma_lb, g_hgrn, w_out,
           g_ffn2, w_ffn2_in, w_ffn2_out, g_final):
    depth, d = w_in.shape[0], w_in.shape[1]
    bp, tp, _ = x_prompt.shape
    bs, ts, _ = x_sample.shape
    p_len = cache_a_k.shape[2]
    p = dict(g_ffn1=g_ffn1, g_mix=g_mix, diff_lambda=diff_lambda, g_diff=g_diff.reshape(depth, 1, B_V_DIM),
             gamma_lb=gamma_lb, g_hgrn=g_hgrn.reshape(depth, 1, C_V_DIM), g_ffn2=g_ffn2, g_final=g_final)
    geom_p = dict(bsz=bp, t=tp, s_pad=tp, n_keys=tp, q_pos0=0, tq=_row_tile(tp, 128))
    n_keys_s = p_len + ts
    geom_s = dict(bsz=bs, t=ts, s_pad=-(-n_keys_s // V7X_LANES) * V7X_LANES, n_keys=n_keys_s, q_pos0=p_len, tq=ts)
    past = (cache_a_k.reshape(depth, bs, p_len * A_KV_HEADS, A_HEAD_DIM),
            cache_a_v.reshape(depth, bs, p_len * A_KV_HEADS, A_HEAD_DIM),
            cache_a_kidx,
            cache_b_k.reshape(depth, bs, p_len * B_HEADS, 2 * B_QK_DIM),
            cache_b_v.reshape(depth, bs, p_len * B_HEADS, B_V_DIM),
            state_c)

    xp = x_prompt.reshape(bp * tp, d)
    xs = x_sample.reshape(bs * ts, d)
    caches_p = caches_s = None
    states_p, states_s = [], []
    for layer in range(depth):
        w_f32 = dict(ffn1=("f32", w_ffn1_in, w_ffn1_out, layer), proj=("f32", w_in), out=("f32", w_out, layer),
                     ffn2=("f32", w_ffn2_in, w_ffn2_out, layer))
        xs, caches_s, s_c, w_bf = _layer(xs, layer, w_f32, p, caches_s, past, geom_s)
        states_s.append(s_c)
        xp, caches_p, s_c, _ = _layer(xp, layer, w_bf, p, caches_p, None, geom_p)
        states_p.append(s_c)

    def group_out(x, caches, states, bsz, t):
        a_k, a_v, a_ki, b_k, b_v = caches
        return (x.reshape(bsz, t, d),
                a_k.reshape(depth, bsz, t, A_KV_HEADS, A_HEAD_DIM), a_v.reshape(depth, bsz, t, A_KV_HEADS, A_HEAD_DIM),
                a_ki.reshape(depth, bsz, t, IDX_DIM), b_k.reshape(depth, bsz, t, B_HEADS, 2 * B_QK_DIM),
                b_v.reshape(depth, bsz, t, B_HEADS, B_V_DIM), jnp.stack(states))

    out_p = group_out(xp, caches_p, states_p, bp, tp)
    out_s = group_out(xs, caches_s, states_s, bs, ts)
    return (out_p[0], out_s[0], *out_p[1:], *out_s[1:])
```

```python
import functools
import math

import jax
import jax.numpy as jnp
from jax import lax
from jax.experimental import pallas as pl
from jax.experimental.pallas import tpu as pltpu

F32 = jnp.float32
BF16 = jnp.bfloat16

CHUNK = 64
A_HEADS, A_KV_HEADS, A_HEAD_DIM = 4, 2, 128
IDX_HEADS, IDX_DIM = 8, 64
TOPK_MAX = 256
B_HEADS, B_QK_DIM, B_V_DIM = 4, 64, 128
C_HEADS, C_K_DIM, C_V_DIM = 8, 128, 128
NORM_EPS = 1e-6
NEG_BIG = -1e30
LB_FLOOR = 1e-30

A_Q_W = A_HEADS * A_HEAD_DIM
A_KV_W = A_KV_HEADS * A_HEAD_DIM
A_QI_W = IDX_HEADS * IDX_DIM
B_QK_W = B_HEADS * 2 * B_QK_DIM
B_V_W = B_HEADS * B_V_DIM
C_W = C_HEADS * C_K_DIM
MIX_W = A_Q_W + B_V_W + C_W
MAIN_W = A_Q_W + 2 * A_KV_W + A_QI_W + 2 * B_QK_W + B_V_W + 4 * C_W
NARROW_W = IDX_DIM + IDX_HEADS
SMALL_W = 128
PROJ_TN = 1024
ATTN_QUERY_TILE = 256
ATTN_KEY_CHUNK = 256
COUNT_ROWS = 64
DIFF_HEADS_PER_LOOP = 2
COL_A_Q, COL_A_K, COL_A_V, COL_A_QI = 0, 512, 768, 1024
COL_B_Q, COL_B_K, COL_B_V = 1536, 2048, 2560
COL_C_Q, COL_C_F, COL_C_I, COL_C_G = 3072, 4096, 5120, 6144

V7X_LANES = 128
V7X_VMEM_LIMIT = 56 * 1024 * 1024


def _cparams(semantics):
    return pltpu.CompilerParams(dimension_semantics=semantics, vmem_limit_bytes=V7X_VMEM_LIMIT)


def _rmsnorm(x, g):
    return x * lax.rsqrt(jnp.mean(x * x, axis=-1, keepdims=True) + NORM_EPS) * g


def _dot(a, b):
    return jnp.dot(a, b, preferred_element_type=F32)


def _dot_nt(a, b):
    return lax.dot_general(a, b, (((1,), (1,)), ((), ())), preferred_element_type=F32)


def _dot_tn(a, b):
    return lax.dot_general(a, b, (((0,), (0,)), ((), ())), preferred_element_type=F32)


def _row_tile(n, pref):
    t = min(n, pref)
    while n % t:
        t //= 2
    return t


def _ffn_body(*refs, n_ff_steps, final_norm, emit):
    x_ref, g_ref, wg_ref, wu_ref, wo_ref = refs[:5]
    gf_ref = refs[5] if final_norm else None
    outs = refs[5 + final_norm:]
    o_ref, xn_ref = outs[0], outs[-1]
    j = pl.program_id(1)

    @pl.when(j == 0)
    def _():
        x = x_ref[...]
        xn_ref[...] = _rmsnorm(x, g_ref[...]).astype(BF16)
        o_ref[...] = x

    wg, wu, wo = wg_ref[...].astype(BF16), wu_ref[...].astype(BF16), wo_ref[...].astype(BF16)
    if emit:
        outs[1][...] = wg
        outs[2][...] = wu
        outs[3][...] = wo
    xn = xn_ref[...]
    gate = _dot(xn, wg)
    up = _dot(xn, wu)
    act = (gate * jax.nn.sigmoid(gate)) * up * 0.5
    o_ref[...] += _dot(act.astype(BF16), wo)

    if final_norm:
        @pl.when(j == n_ff_steps - 1)
        def _():
            o_ref[...] = _rmsnorm(o_ref[...], gf_ref[...])


def _ffn(x, g, w, g_final=None):
    n, d = x.shape
    emit = w[0] == "f32"
    if emit:
        _, w_in, w_out, layer = w
        f = w_out.shape[1]
    else:
        _, wg, wu, wo = w
        f = wo.shape[0]
    tm = _row_tile(n, 512)
    tf = 512 if f % 512 == 0 else f
    nj = f // tf
    assert not emit or n == tm, "the weight-emitting variant expects a single row tile"
    final_norm = g_final is not None
    in_specs = [pl.BlockSpec((tm, d), lambda i, j: (i, 0)), pl.BlockSpec((1, d), lambda i, j: (0, 0))]
    if emit:
        in_specs += [pl.BlockSpec((None, d, tf), lambda i, j: (layer, 0, j)),
                     pl.BlockSpec((None, d, tf), lambda i, j: (layer, 0, j + nj)),
                     pl.BlockSpec((None, tf, d), lambda i, j: (layer, j, 0))]
        args = [x, g.reshape(1, d), w_in, w_in, w_out]
    else:
        in_specs += [pl.BlockSpec((d, tf), lambda i, j: (0, j)), pl.BlockSpec((d, tf), lambda i, j: (0, j)),
                     pl.BlockSpec((tf, d), lambda i, j: (j, 0))]
        args = [x, g.reshape(1, d), wg, wu, wo]
    if final_norm:
        in_specs.append(pl.BlockSpec((1, d), lambda i, j: (0, 0)))
        args.append(g_final.reshape(1, d))
    out_specs = [pl.BlockSpec((tm, d), lambda i, j: (i, 0))]
    out_shape = [jax.ShapeDtypeStruct((n, d), F32)]
    if emit:
        out_specs += [pl.BlockSpec((d, tf), lambda i, j: (0, j)), pl.BlockSpec((d, tf), lambda i, j: (0, j)),
                      pl.BlockSpec((tf, d), lambda i, j: (j, 0))]
        out_shape += [jax.ShapeDtypeStruct((d, f), BF16), jax.ShapeDtypeStruct((d, f), BF16),
                      jax.ShapeDtypeStruct((f, d), BF16)]
    res = pl.pallas_call(
        functools.partial(_ffn_body, n_ff_steps=nj, final_norm=final_norm, emit=emit),
        grid=(n // tm, nj),
        in_specs=in_specs,
        out_specs=out_specs,
        out_shape=out_shape,
        scratch_shapes=[pltpu.VMEM((tm, d), BF16)],
        compiler_params=_cparams(("parallel", "arbitrary")),
        name="ffn_emit" if emit else "ffn",
    )(*args)
    return (res[0], ("bf16", res[1], res[2], res[3])) if emit else (res[0], w)


def _proj_body(*refs, emit, tn, tm, n_alias, layer):
    n_w = 3 if emit else 2
    x_ref, g_ref = refs[:2]
    w_refs = refs[2:2 + n_w]
    outs = refs[2 + n_w + n_alias:]
    om_ref, os_ref, ak_ref, av_ref, aki_ref, bk_ref, bv_ref = outs[:7]
    xn_ref = refs[-1]
    j = pl.program_id(1)

    def this_layer(ref):
        if n_alias:
            return ref
        for other in range(ref.shape[0]):
            if other != layer:
                ref[other] = jnp.zeros(ref.shape[1:], F32)
        return ref.at[layer]

    if emit:
        wa_ref, wx_ref, wsm_ref = w_refs
        wmo_ref, wso_ref = outs[7:9]
        n_aligned = COL_B_Q // tn

        @pl.when(j < n_aligned)
        def _():
            wmo_ref[...] = wa_ref[...].astype(BF16)

        @pl.when(j >= n_aligned)
        def _():
            wide = jnp.concatenate([wa_ref[...], wx_ref[...]], axis=1)
            wmo_ref[...] = wide[:, NARROW_W:NARROW_W + tn].astype(BF16)

        wm = wmo_ref[...]
    else:
        wm = w_refs[0][...]

    @pl.when(j == 0)
    def _():
        xn = _rmsnorm(x_ref[...], g_ref[...]).astype(BF16)
        xn_ref[...] = xn
        if emit:
            lane = lax.broadcasted_iota(jnp.int32, wsm_ref.shape, 1)
            ws = jnp.where(lane < NARROW_W, wsm_ref[...], 0.0).astype(BF16)
            wso_ref[...] = ws
        else:
            ws = w_refs[1][...]
        small = _dot(xn, ws)
        os_ref[...] = small
        this_layer(aki_ref)[...] = small[:, 0:IDX_DIM]

    acc = _dot(xn_ref[...], wm)
    om_ref[...] = acc

    def put_heads(ref, col, heads):
        off = col % tn
        assert off + heads * V7X_LANES <= tn

        @pl.when(j == col // tn)
        def _():
            dst = this_layer(ref)
            for hd in range(heads):
                dst[pl.ds(hd, tm, stride=heads), :] = acc[:, off + hd * V7X_LANES:off + (hd + 1) * V7X_LANES]

    put_heads(ak_ref, COL_A_K, A_KV_HEADS)
    put_heads(av_ref, COL_A_V, A_KV_HEADS)
    put_heads(bk_ref, COL_B_K, B_HEADS)
    put_heads(bv_ref, COL_B_V, B_HEADS)


def _proj(x, g, w, caches, layer, depth):
    n, d = x.shape
    emit = w[0] == "f32"
    tm = _row_tile(n, 512)
    tn = 512 if emit else PROJ_TN
    assert not emit or n == tm, "the weight-emitting variant expects a single row tile"
    lanes_per_tn = tn // V7X_LANES
    if emit:
        assert COL_B_Q % tn == 0
        w_in = w[1]
        w_specs = [pl.BlockSpec((None, d, tn), lambda i, j: (layer, 0, j)),
                   pl.BlockSpec((None, d, V7X_LANES), lambda i, j: (layer, 0, lanes_per_tn * (j + 1))),
                   pl.BlockSpec((None, d, SMALL_W), lambda i, j: (layer, 0, COL_B_Q // SMALL_W))]
        w_args = [w_in, w_in, w_in]
    else:
        w_specs = [pl.BlockSpec((d, tn), lambda i, j: (0, j)), pl.BlockSpec((d, SMALL_W), lambda i, j: (0, 0))]
        w_args = list(w[1:])
    cache_rows = (A_KV_HEADS, A_KV_HEADS, 1, B_HEADS, B_HEADS)
    cache_lanes = (V7X_LANES, V7X_LANES, IDX_DIM, V7X_LANES, V7X_LANES)
    alias_args = [] if caches is None else list(caches)
    out_specs = [pl.BlockSpec((tm, tn), lambda i, j: (i, j)), pl.BlockSpec((tm, SMALL_W), lambda i, j: (i, 0))]
    if alias_args:
        cache_spec = lambda r, wd: pl.BlockSpec((None, r * tm, wd), lambda i, j: (layer, i, 0))
    else:
        cache_spec = lambda r, wd: pl.BlockSpec((depth, r * tm, wd), lambda i, j: (0, i, 0))
    out_specs += [cache_spec(r, wd) for r, wd in zip(cache_rows, cache_lanes)]
    out_shape = [jax.ShapeDtypeStruct((n, MAIN_W), F32), jax.ShapeDtypeStruct((n, SMALL_W), F32)]
    out_shape += [jax.ShapeDtypeStruct((depth, r * n, wd), F32) for r, wd in zip(cache_rows, cache_lanes)]
    if emit:
        out_specs += [pl.BlockSpec((d, tn), lambda i, j: (0, j)), pl.BlockSpec((d, SMALL_W), lambda i, j: (0, 0))]
        out_shape += [jax.ShapeDtypeStruct((d, MAIN_W), BF16), jax.ShapeDtypeStruct((d, SMALL_W), BF16)]
    n_in = 2 + len(w_args)
    res = pl.pallas_call(
        functools.partial(_proj_body, emit=emit, tn=tn, tm=tm, n_alias=len(alias_args), layer=layer),
        grid=(n // tm, MAIN_W // tn),
        in_specs=[pl.BlockSpec((tm, d), lambda i, j: (i, 0)), pl.BlockSpec((1, d), lambda i, j: (0, 0))]
        + w_specs + [pl.BlockSpec(memory_space=pl.ANY)] * len(alias_args),
        out_specs=out_specs,
        out_shape=out_shape,
        input_output_aliases={n_in + k: 2 + k for k in range(len(alias_args))},
        scratch_shapes=[pltpu.VMEM((tm, d), BF16)],
        compiler_params=_cparams(("parallel", "arbitrary")),
        name="in_proj_emit" if emit else "in_proj",
    )(x, g.reshape(1, d), *w_args, *alias_args)
    w_bf = ("bf16", res[7], res[8]) if emit else w
    return res[0], res[1], tuple(res[2:7]), w_bf


def _sortable_key(x):
    bits = lax.bitcast_convert_type(x, jnp.int32)
    return jnp.where(bits < 0, bits ^ jnp.int32(0x7FFFFFFF), bits)


def _kth_largest(key, k):
    rows = key.shape[0]

    def body(it, t):
        cand = t + lax.shift_left(jnp.int32(1), jnp.int32(31) - it)
        cnt = jnp.sum(jnp.where(key >= cand, 1.0, 0.0), axis=1, keepdims=True)
        return jnp.where(cnt >= k, cand, t)

    t0 = jnp.full((rows, 1), jnp.iinfo(jnp.int32).min, jnp.int32)
    return lax.fori_loop(0, 32, body, t0)


def _tie_cut(tie, kpos, need, nbits):
    rows = tie.shape[0]

    def body(it, t):
        cand = t + lax.shift_left(jnp.int32(1), jnp.int32(nbits - 1) - it)
        cnt = jnp.sum(jnp.where(tie & (kpos < cand), 1.0, 0.0), axis=1, keepdims=True)
        return jnp.where(cnt < need, cand, t)

    return lax.fori_loop(0, nbits, body, jnp.zeros((rows, 1), jnp.int32))


def _visible(tq, s_pad, q_pos0, n_keys):
    qpos = q_pos0 + pl.program_id(1) * tq + lax.broadcasted_iota(jnp.int32, (tq, 1), 0)
    limit = jnp.minimum(((qpos >> 6) + 1) << 6, n_keys)
    kpos = lax.broadcasted_iota(jnp.int32, (tq, s_pad), 1)
    return kpos < limit, kpos


def _head_keys(new_ref, lo, width, past_ref, head, heads, s_pad):
    new = new_ref[:, lo:lo + width]
    if past_ref is None:
        assert new.shape[0] == s_pad
        return new.astype(BF16)
    p_len = past_ref.shape[0] // heads
    hist = past_ref[...] if heads == 1 else past_ref[pl.ds(head, p_len, stride=heads), :]
    parts = [hist, new]
    pad = s_pad - p_len - new.shape[0]
    if pad:
        parts.append(jnp.zeros((pad, width), F32))
    return jnp.concatenate(parts, axis=0).astype(BF16)


def _dsa_body(*refs, tq, s_pad, n_keys, q_pos0, topk, has_past):
    q_ref, qi_ref, wq_ref, k_ref, v_ref, ki_ref = refs[:6]
    pk_ref, pv_ref, pki_ref = refs[6:9] if has_past else (None, None, None)
    o_ref = refs[-1]
    vis, kpos = _visible(tq, s_pad, q_pos0, n_keys)

    kidx = _head_keys(ki_ref, 0, IDX_DIM, pki_ref, 0, 1, s_pad)
    wq = wq_ref[:, IDX_DIM:IDX_DIM + IDX_HEADS] * (IDX_HEADS ** -0.5)
    score = jnp.zeros((tq, s_pad), F32)
    for h in range(IDX_HEADS):
        qi_h = qi_ref[:, h * IDX_DIM:(h + 1) * IDX_DIM].astype(BF16)
        logit = _dot_nt(qi_h, kidx) * (IDX_DIM ** -0.5)
        score = score + jnp.maximum(logit, 0.0) * wq[:, h:h + 1]
    score = jnp.where(vis, score + 0.0, NEG_BIG)

    key = _sortable_key(score)
    thr = _kth_largest(key, float(topk))
    n_gt = jnp.sum(jnp.where(key > thr, 1.0, 0.0), axis=1, keepdims=True)
    tie = key == thr
    cut = _tie_cut(tie, kpos, float(topk) - n_gt, max(1, (s_pad - 1).bit_length()))
    sel = ((key > thr) | (tie & (kpos <= cut))) & vis

    rep = A_HEADS // A_KV_HEADS
    for g in range(A_KV_HEADS):
        k_g = _head_keys(k_ref, g * A_HEAD_DIM, A_HEAD_DIM, pk_ref, g, A_KV_HEADS, s_pad)
        v_g = _head_keys(v_ref, g * A_HEAD_DIM, A_HEAD_DIM, pv_ref, g, A_KV_HEADS, s_pad)
        for h in range(g * rep, (g + 1) * rep):
            q_h = q_ref[:, h * A_HEAD_DIM:(h + 1) * A_HEAD_DIM].astype(BF16)
            att = jnp.where(sel, _dot_nt(q_h, k_g) * (A_HEAD_DIM ** -0.5), NEG_BIG)
            p = jnp.exp(att - jnp.max(att, axis=1, keepdims=True))
            l = jnp.sum(p, axis=1, keepdims=True)
            o = _dot(p.astype(BF16), v_g) / l
            o_ref[:, h * A_HEAD_DIM:(h + 1) * A_HEAD_DIM] = o.astype(o_ref.dtype)


def _softmax_step_t(carry, s, v_c):
    m, l, acc = carry
    m_new = jnp.maximum(m, _col_reduce(s, jnp.max))
    alpha = jnp.exp(m - m_new)
    p = jnp.exp(s - m_new)
    return m_new, alpha * l + _col_reduce(p, jnp.sum), alpha * acc + _dot_tn(v_c, p.astype(BF16))


def _col_reduce(x, op):
    rows, tq = x.shape
    return op(op(x.reshape(rows // COUNT_ROWS, COUNT_ROWS, tq), axis=0), axis=0, keepdims=True)


def _softmax_init_t(tq, width):
    return (jnp.full((1, tq), -jnp.inf, F32), jnp.zeros((1, tq), F32), jnp.zeros((width, tq), F32))


def _causal_extent(tq, kc):
    i = pl.program_id(1)
    qpos = i * tq + lax.broadcasted_iota(jnp.int32, (1, tq), 1)
    limit = ((qpos >> 6) + 1) << 6
    return (i * tq + tq + kc - 1) // kc, limit, lax.broadcasted_iota(jnp.int32, (kc, 1), 0)


def _dsa_causal_body(q_ref, qi_ref, wq_ref, k_ref, v_ref, ki_ref, o_ref, key_ref, cut_ref, *, tq, kc, topk, nbits):
    n_chunks, limit, krow = _causal_extent(tq, kc)
    wq_t = wq_ref[...].T[IDX_DIM:IDX_DIM + IDX_HEADS, :] * (IDX_HEADS ** -0.5)
    qi_all = jnp.concatenate([qi_ref[:, h * IDX_DIM:(h + 1) * IDX_DIM] for h in range(IDX_HEADS)], axis=0).astype(BF16)

    def score_chunk(c, carry):
        start = pl.multiple_of(c * kc, kc)
        kidx = ki_ref[pl.ds(start, kc), 0:IDX_DIM].astype(BF16)
        logits = _dot_nt(kidx, qi_all)
        acc = jnp.zeros((kc, tq), F32)
        for h in range(IDX_HEADS):
            acc = acc + jnp.maximum(logits[:, h * tq:(h + 1) * tq] * (IDX_DIM ** -0.5), 0.0) * wq_t[h:h + 1, :]
        score = jnp.where(start + krow < limit, acc + 0.0, NEG_BIG)
        key_ref[c] = _sortable_key(score)
        return carry

    lax.fori_loop(0, n_chunks, score_chunk, 0)

    def count(pred):
        def body(c, part):
            ones = jnp.where(pred(key_ref[c], c), 1.0, 0.0)
            return part + jnp.sum(ones.reshape(kc // COUNT_ROWS, COUNT_ROWS, tq), axis=0)

        part = lax.fori_loop(0, n_chunks, body, jnp.zeros((COUNT_ROWS, tq), F32))
        return jnp.sum(part, axis=0, keepdims=True)

    def thr_step(it, t):
        cand = t + lax.shift_left(jnp.int32(1), jnp.int32(31) - it)
        return jnp.where(count(lambda key, c: key >= cand) >= float(topk), cand, t)

    thr = lax.fori_loop(0, 32, thr_step, jnp.full((1, tq), jnp.iinfo(jnp.int32).min, jnp.int32))
    need = float(topk) - count(lambda key, c: key > thr)
    n_tie = count(lambda key, c: key == thr)

    cut_ref[...] = jnp.full(cut_ref.shape, jnp.iinfo(jnp.int32).max, jnp.int32)

    @pl.when(jnp.max(jnp.where(n_tie != need, 1.0, 0.0)) > 0.0)
    def _():
        def cut_step(it, t):
            cand = t + lax.shift_left(jnp.int32(1), jnp.int32(nbits - 1) - it)
            below = count(lambda key, c: (key == thr) & (c * kc + krow < cand))
            return jnp.where(below < need, cand, t)

        cut = lax.fori_loop(0, nbits, cut_step, jnp.zeros((1, tq), jnp.int32))
        cut_ref[...] = jnp.broadcast_to(cut, cut_ref.shape)

    cut = cut_ref[0:1, :]

    def select_chunk(c, carry):
        key = key_ref[c]
        kpos = c * kc + krow
        tie_flag = jnp.where(key == thr, jnp.where(kpos <= cut, 0, 1), 1)
        key_ref[c] = jnp.where(kpos < limit, jnp.where(key > thr, 0, tie_flag), 1)
        return carry

    lax.fori_loop(0, n_chunks, select_chunk, 0)

    rep = A_HEADS // A_KV_HEADS
    q_g = [jnp.concatenate([q_ref[:, h * A_HEAD_DIM:(h + 1) * A_HEAD_DIM] for h in range(g * rep, (g + 1) * rep)],
                           axis=0).astype(BF16) for g in range(A_KV_HEADS)]

    def att_chunk(c, carry):
        start = pl.multiple_of(c * kc, kc)
        attend = key_ref[c] == 0
        attend = jnp.concatenate([attend] * rep, axis=1)
        logits = [_dot_nt(k_ref[pl.ds(start, kc), g * A_HEAD_DIM:(g + 1) * A_HEAD_DIM].astype(BF16), q_g[g])
                  for g in range(A_KV_HEADS)]
        v_c = [v_ref[pl.ds(start, kc), g * A_HEAD_DIM:(g + 1) * A_HEAD_DIM].astype(BF16) for g in range(A_KV_HEADS)]
        return tuple(
            _softmax_step_t(carry[g], jnp.where(attend, logits[g] * (A_HEAD_DIM ** -0.5), NEG_BIG), v_c[g])
            for g in range(A_KV_HEADS))

    res = lax.fori_loop(0, n_chunks, att_chunk, tuple(_softmax_init_t(rep * tq, A_HEAD_DIM) for _ in range(A_KV_HEADS)))
    for g in range(A_KV_HEADS):
        _, l, acc = res[g]
        o_t = acc / l
        for r in range(rep):
            h = g * rep + r
            o_ref[:, h * A_HEAD_DIM:(h + 1) * A_HEAD_DIM] = o_t[:, r * tq:(r + 1) * tq].T.astype(o_ref.dtype)


def _diff_causal_body(q_ref, k_ref, v_ref, lam_ref, g_ref, o_ref, *, tq, kc, lam_init):
    n_chunks, limit, krow = _causal_extent(tq, kc)
    lv = lam_ref[...]
    lam = (jnp.exp(jnp.sum(lv[0:1] * lv[1:2], axis=1, keepdims=True))
           - jnp.exp(jnp.sum(lv[2:3] * lv[3:4], axis=1, keepdims=True)) + lam_init)
    gain = g_ref[...] * (1.0 - lam_init)
    head_w = 2 * B_QK_DIM
    qlane = lax.broadcasted_iota(jnp.int32, (tq, head_w), 1)
    for h0 in range(0, B_HEADS, DIFF_HEADS_PER_LOOP):
        heads = range(h0, h0 + DIFF_HEADS_PER_LOOP)
        q_2 = [jnp.concatenate([jnp.where((qlane >= c * B_QK_DIM) & (qlane < (c + 1) * B_QK_DIM),
                                          q_ref[:, h * head_w:(h + 1) * head_w], 0.0) for c in range(2)],
                               axis=0).astype(BF16) for h in heads]

        def chunk(c, carry, q_2=q_2, heads=heads):
            start = pl.multiple_of(c * kc, kc)
            vis = start + krow < jnp.concatenate([limit, limit], axis=1)
            logits = [_dot_nt(k_ref[pl.ds(start, kc), h * head_w:(h + 1) * head_w].astype(BF16), q_2[j])
                      for j, h in enumerate(heads)]
            v_c = [v_ref[pl.ds(start, kc), h * B_V_DIM:(h + 1) * B_V_DIM].astype(BF16) for h in heads]
            return tuple(
                _softmax_step_t(carry[j], jnp.where(vis, logits[j] * (B_QK_DIM ** -0.5), NEG_BIG), v_c[j])
                for j in range(len(heads)))

        res = lax.fori_loop(0, n_chunks, chunk, tuple(_softmax_init_t(2 * tq, B_V_DIM) for _ in heads))
        for j, h in enumerate(heads):
            _, l, acc = res[j]
            o_t = acc / l
            o = (o_t[:, 0:tq] - lam * o_t[:, tq:2 * tq]).T
            o = o * lax.rsqrt(jnp.mean(o * o, axis=1, keepdims=True) + NORM_EPS) * gain
            o_ref[:, h * B_V_DIM:(h + 1) * B_V_DIM] = o.astype(o_ref.dtype)


def _blk(arr_col, rows, width, kind, nq):
    _, col = arr_col
    cb = col // width
    assert col % width == 0
    if kind == "q":
        return pl.BlockSpec((rows, width), lambda b, i: (b * nq + i, cb))
    return pl.BlockSpec((rows, width), lambda b, i: (b, cb))


def _past_blk(cache, layer):
    return pl.BlockSpec((None, None) + cache.shape[2:], lambda b, i: (layer, b, 0, 0))


def _dsa(q, qi, wq, k, v, ki, past, *, bsz, t, s_pad, n_keys, q_pos0, tq, layer):
    nq = t // tq
    topk = min(TOPK_MAX, n_keys // 4)
    scratch = []
    if past is None:
        kc = _row_tile(t, ATTN_KEY_CHUNK)
        body = functools.partial(_dsa_causal_body, tq=tq, kc=kc, topk=topk, nbits=max(1, (t - 1).bit_length()))
        scratch = [pltpu.VMEM((t // kc, kc, tq), jnp.int32), pltpu.VMEM((8, tq), jnp.int32)]
    else:
        body = functools.partial(_dsa_body, tq=tq, s_pad=s_pad, n_keys=n_keys, q_pos0=q_pos0, topk=topk,
                                 has_past=True)
    in_specs = [
        _blk(q, tq, A_Q_W, "q", nq),
        _blk(qi, tq, A_QI_W, "q", nq),
        _blk(wq, tq, SMALL_W, "q", nq),
        _blk(k, t, A_KV_W, "kv", nq),
        _blk(v, t, A_KV_W, "kv", nq),
        _blk(ki, t, SMALL_W, "kv", nq),
    ]
    args = [q[0], qi[0], wq[0], k[0], v[0], ki[0]]
    if past is not None:
        in_specs += [_past_blk(c, layer) for c in past]
        args += list(past)
    return pl.pallas_call(
        body,
        grid=(bsz, nq),
        in_specs=in_specs,
        out_specs=pl.BlockSpec((tq, A_Q_W), lambda b, i: (b * nq + i, 0)),
        out_shape=jax.ShapeDtypeStruct((bsz * t, A_Q_W), BF16),
        scratch_shapes=scratch,
        compiler_params=_cparams(("parallel", "parallel")),
        name="dsa",
    )(*args)


def _diff_body(*refs, tq, s_pad, n_keys, q_pos0, lam_init, has_past):
    q_ref, k_ref, v_ref, lam_ref, g_ref = refs[:5]
    pk_ref, pv_ref = refs[5:7] if has_past else (None, None)
    o_ref = refs[-1]
    vis, _ = _visible(tq, s_pad, q_pos0, n_keys)
    lv = lam_ref[...]
    lam = (jnp.exp(jnp.sum(lv[0:1] * lv[1:2], axis=1, keepdims=True))
           - jnp.exp(jnp.sum(lv[2:3] * lv[3:4], axis=1, keepdims=True)) + lam_init)
    gain = g_ref[...] * (1.0 - lam_init)
    head_w = 2 * B_QK_DIM
    lane = lax.broadcasted_iota(jnp.int32, (tq, head_w), 1)
    for h in range(B_HEADS):
        k_h = _head_keys(k_ref, h * head_w, head_w, pk_ref, h, B_HEADS, s_pad)
        v_h = _head_keys(v_ref, h * B_V_DIM, B_V_DIM, pv_ref, h, B_HEADS, s_pad)
        q_h = q_ref[:, h * head_w:(h + 1) * head_w]
        parts = []
        for c in range(2):
            q_c = jnp.where((lane >= c * B_QK_DIM) & (lane < (c + 1) * B_QK_DIM), q_h, 0.0).astype(BF16)
            logit = jnp.where(vis, _dot_nt(q_c, k_h) * (B_QK_DIM ** -0.5), NEG_BIG)
            p = jnp.exp(logit - jnp.max(logit, axis=1, keepdims=True))
            l = jnp.sum(p, axis=1, keepdims=True)
            parts.append(_dot(p.astype(BF16), v_h) / l)
        o = parts[0] - lam * parts[1]
        o = o * lax.rsqrt(jnp.mean(o * o, axis=1, keepdims=True) + NORM_EPS) * gain
        o_ref[:, h * B_V_DIM:(h + 1) * B_V_DIM] = o.astype(o_ref.dtype)


def _diff(q, k, v, lam_vecs, g_diff, past, *, bsz, t, s_pad, n_keys, q_pos0, tq, layer):
    nq = t // tq
    lam_init = 0.8 - 0.6 * math.exp(-0.3 * layer)
    if past is None:
        body = functools.partial(_diff_causal_body, tq=tq, kc=_row_tile(t, ATTN_KEY_CHUNK), lam_init=lam_init)
    else:
        body = functools.partial(_diff_body, tq=tq, s_pad=s_pad, n_keys=n_keys, q_pos0=q_pos0, lam_init=lam_init,
                                 has_past=True)
    in_specs = [
        _blk(q, tq, B_QK_W, "q", nq),
        _blk(k, t, B_QK_W, "kv", nq),
        _blk(v, t, B_V_W, "kv", nq),
        pl.BlockSpec((None, 4, B_QK_DIM), lambda b, i: (layer, 0, 0)),
        pl.BlockSpec((None, 1, B_V_DIM), lambda b, i: (layer, 0, 0)),
    ]
    args = [q[0], k[0], v[0], lam_vecs, g_diff]
    if past is not None:
        in_specs += [_past_blk(c, layer) for c in past]
        args += list(past)
    return pl.pallas_call(
        body,
        grid=(bsz, nq),
        in_specs=in_specs,
        out_specs=pl.BlockSpec((tq, B_V_W), lambda b, i: (b * nq + i, 0)),
        out_shape=jax.ShapeDtypeStruct((bsz * t, B_V_W), BF16),
        compiler_params=_cparams(("parallel", "parallel")),
        name="diff_attn",
    )(*args)


def _cumsum_rows(x):
    n = x.shape[0]
    row = lax.broadcasted_iota(jnp.int32, (n, 1), 0)
    d = 1
    while d < n:
        x = x + jnp.where(row >= d, pltpu.roll(x, d, axis=0), 0.0)
        d *= 2
    return x


def _anchor_rows(b, c):
    n, w = b.shape
    if 2 * c >= 8:
        pieces = [jnp.broadcast_to(b[j * 2 * c + c - 1:j * 2 * c + c, :], (2 * c, w)) for j in range(n // (2 * c))]
        return pieces[0] if len(pieces) == 1 else jnp.concatenate(pieces, axis=0)
    rmod = lax.broadcasted_iota(jnp.int32, (n, 1), 0) & (2 * c - 1)
    m = b
    for s in range(-c, c):
        if s != 0:
            m = jnp.where(rmod == c - 1 - s, pltpu.roll(b, (-s) % n, axis=0), m)
    return m


def _hgrn_body(*refs, chunk, layer, has_s0):
    if has_s0:
        q_ref, f_ref, i_ref, g_ref, glb_ref, gn_ref, s0_ref, o_ref, s_ref, st_ref = refs
    else:
        q_ref, f_ref, i_ref, g_ref, glb_ref, gn_ref, o_ref, s_ref, st_ref = refs
    ci = pl.program_id(1)

    @pl.when(ci == 0)
    def _():
        for h in range(C_HEADS):
            st_ref[h] = s0_ref[h].T if has_s0 else jnp.zeros((C_V_DIM, C_K_DIM), F32)

    glb = glb_ref[...]
    pe = jnp.exp(glb - jnp.max(glb, axis=0, keepdims=True))
    pn = pe / jnp.sum(pe, axis=0, keepdims=True)
    lb = jnp.sum(pn[0:layer + 1], axis=0, keepdims=True) - pn[0:1]

    zf = f_ref[...]
    log_sig = jnp.minimum(zf, 0.0) - jnp.log1p(jnp.exp(-jnp.abs(zf)))
    la = jnp.log(jnp.maximum(lb, LB_FLOOR))
    lc = jnp.log1p(-lb) + log_sig
    hi = jnp.maximum(la, lc)
    log_f = hi + jnp.log1p(jnp.exp(jnp.minimum(la, lc) - hi))
    kc = (1.0 - lb) * jax.nn.sigmoid(-zf)
    b_all = _cumsum_rows(log_f)

    row = lax.broadcasted_iota(jnp.int32, (chunk, 1), 0)
    rr = lax.broadcasted_iota(jnp.int32, (chunk, chunk), 0)
    cc = lax.broadcasted_iota(jnp.int32, (chunk, chunk), 1)
    gn = gn_ref[...]

    for h in range(C_HEADS):
        sl = slice(h * C_K_DIM, (h + 1) * C_K_DIM)
        q = q_ref[:, sl]
        k = kc[:, sl]
        v = i_ref[:, sl]
        b = b_all[:, sl]
        vb = v.astype(BF16)
        st = st_ref[h]
        o = _dot_nt((q * jnp.exp(b)).astype(BF16), st.astype(BF16))
        o = o + jnp.sum(q * k, axis=1, keepdims=True) * v
        a = jnp.zeros((chunk, chunk), F32)
        c = chunk // 2
        while c >= 1:
            upper = ((row >> (c.bit_length() - 1)) & 1) == 1
            d = b - _anchor_rows(b, c)
            e = jnp.exp(jnp.where(upper, d, -d))
            qt = jnp.where(upper, q * e, 0.0).astype(BF16)
            kt = jnp.where(upper, 0.0, k * e).astype(BF16)
            sh = (2 * c).bit_length() - 1
            a = a + jnp.where((rr >> sh) == (cc >> sh), _dot_nt(qt, kt), 0.0)
            c //= 2
        o = o + _dot(a.astype(BF16), vb)
        b_last = b[chunk - 1:chunk, :]
        k_dec = (k * jnp.exp(b_last - b)).astype(BF16)
        st_new = st * jnp.exp(b_last) + _dot_tn(vb, k_dec)
        st_ref[h] = st_new

        o = o * lax.rsqrt(jnp.mean(o * o, axis=1, keepdims=True) + NORM_EPS) * gn
        zg = g_ref[:, sl]
        o_ref[:, sl] = (o * (zg * jax.nn.sigmoid(zg))).astype(o_ref.dtype)

        @pl.when(ci == pl.num_programs(1) - 1)
        def _():
            s_ref[0, h] = st_new.T


def _hgrn(cq, cf, cin, cg, gamma_lb, g_hgrn, s0, *, bsz, t, layer):
    chunk = min(CHUNK, t)
    nc = t // chunk
    depth = gamma_lb.shape[0]
    has_s0 = s0 is not None

    def seq_blk(arr_col):
        cb = arr_col[1] // C_W
        return pl.BlockSpec((chunk, C_W), lambda b, c: (b * nc + c, cb))

    in_specs = [seq_blk(cq), seq_blk(cf), seq_blk(cin), seq_blk(cg),
                pl.BlockSpec((depth, C_W), lambda b, c: (0, 0)),
                pl.BlockSpec((None, 1, C_V_DIM), lambda b, c: (layer, 0, 0))]
    args = [cq[0], cf[0], cin[0], cg[0], gamma_lb, g_hgrn]
    if has_s0:
        in_specs.append(pl.BlockSpec((None, None, C_HEADS, C_K_DIM, C_V_DIM), lambda b, c: (layer, b, 0, 0, 0)))
        args.append(s0)
    return pl.pallas_call(
        functools.partial(_hgrn_body, chunk=chunk, layer=layer, has_s0=has_s0),
        grid=(bsz, nc),
        in_specs=in_specs,
        out_specs=[pl.BlockSpec((chunk, C_W), lambda b, c: (b * nc + c, 0)),
                   pl.BlockSpec((1, C_HEADS, C_K_DIM, C_V_DIM), lambda b, c: (b, 0, 0, 0))],
        out_shape=[jax.ShapeDtypeStruct((bsz * t, C_W), BF16),
                   jax.ShapeDtypeStruct((bsz, C_HEADS, C_K_DIM, C_V_DIM), F32)],
        scratch_shapes=[pltpu.VMEM((C_HEADS, C_V_DIM, C_K_DIM), F32)],
        compiler_params=_cparams(("parallel", "arbitrary")),
        name="hgrn2",
    )(*args)


def _outproj_body(*refs, emit):
    h_ref, oa_ref, ob_ref, oc_ref, wa_ref, wb_ref, wc_ref, o_ref = refs[:8]
    wa, wb, wc = wa_ref[...].astype(BF16), wb_ref[...].astype(BF16), wc_ref[...].astype(BF16)
    if emit:
        refs[8][...] = wa
        refs[9][...] = wb
        refs[10][...] = wc
    o_ref[...] = h_ref[...] + (_dot(oa_ref[...], wa) + _dot(ob_ref[...], wb) + _dot(oc_ref[...], wc))


def _outproj(h, oa, ob, oc, w):
    n, d = h.shape
    emit = w[0] == "f32"
    tm = _row_tile(n, 512)
    assert not emit or n == tm, "the weight-emitting variant expects a single row tile"
    row = lambda wd: pl.BlockSpec((tm, wd), lambda i: (i, 0))
    whole = lambda r: pl.BlockSpec((r, d), lambda i: (0, 0))
    seg_rows = (A_Q_W, B_V_W, C_W)
    seg_first = (0, A_Q_W, A_Q_W + B_V_W)
    if emit:
        _, w_out, layer = w
        w_specs = [pl.BlockSpec((None, r, d), functools.partial(lambda i, blk: (layer, blk, 0), blk=f // r))
                   for r, f in zip(seg_rows, seg_first)]
        w_args = [w_out] * 3
    else:
        w_specs = [whole(r) for r in seg_rows]
        w_args = list(w[1:])
    out_specs = [row(d)]
    out_shape = [jax.ShapeDtypeStruct((n, d), F32)]
    if emit:
        out_specs += [whole(r) for r in seg_rows]
        out_shape += [jax.ShapeDtypeStruct((r, d), BF16) for r in seg_rows]
    res = pl.pallas_call(
        functools.partial(_outproj_body, emit=emit),
        grid=(n // tm,),
        in_specs=[row(d), row(A_Q_W), row(B_V_W), row(C_W)] + w_specs,
        out_specs=out_specs,
        out_shape=out_shape,
        compiler_params=_cparams(("parallel",)),
        name="out_proj_emit" if emit else "out_proj",
    )(h, oa, ob, oc, *w_args)
    return (res[0], ("bf16", res[1], res[2], res[3])) if emit else (res[0], w)


def _layer(x, layer, w, p, caches, past, geom):
    bsz, t = geom["bsz"], geom["t"]
    depth = p["g_ffn1"].shape[0]
    wb = {}
    h, wb["ffn1"] = _ffn(x, p["g_ffn1"][layer], w["ffn1"])
    main, small, caches, wb["proj"] = _proj(h, p["g_mix"][layer], w["proj"], caches, layer, depth)
    if past is None:
        past_a = past_b = s0 = None
    else:
        past_a, past_b, s0 = past[0:3], past[3:5], past[5]
    o_a = _dsa((main, COL_A_Q), (main, COL_A_QI), (small, 0), (main, COL_A_K), (main, COL_A_V), (small, 0), past_a,
               layer=layer, **geom)
    o_b = _diff((main, COL_B_Q), (main, COL_B_K), (main, COL_B_V), p["diff_lambda"], p["g_diff"], past_b,
                layer=layer, **geom)
    o_c, s_c = _hgrn((main, COL_C_Q), (main, COL_C_F), (main, COL_C_I), (main, COL_C_G),
                     p["gamma_lb"], p["g_hgrn"], s0, bsz=bsz, t=t, layer=layer)
    h, wb["out"] = _outproj(h, o_a, o_b, o_c, w["out"])
    x, wb["ffn2"] = _ffn(h, p["g_ffn2"][layer], w["ffn2"], p["g_final"] if layer == depth - 1 else None)
    return x, caches, s_c, wb


def kernel(x_prompt, x_sample, cache_a_k, cache_a_v, cache_a_kidx, cache_b_k, cache_b_v, state_c,
           g_ffn1, w_ffn1_in, w_ffn1_out, g_mix, w_in, diff_lambda, g_diff, gamma_lb, g_hgrn, w_out,
           g_ffn2, w_ffn2_in, w_ffn2_out, g_final):
    depth, d = w_in.shape[0], w_in.shape[1]
    bp, tp, _ = x_prompt.shape
    bs, ts, _ = x_sample.shape
    p_len = cache_a_k.shape[2]
    p = dict(g_ffn1=g_ffn1, g_mix=g_mix, diff_lambda=diff_lambda, g_diff=g_diff.reshape(depth, 1, B_V_DIM),
             gamma_lb=gamma_lb, g_hgrn=g_hgrn.reshape(depth, 1, C_V_DIM), g_ffn2=g_ffn2, g_final=g_final)
    geom_p = dict(bsz=bp, t=tp, s_pad=tp, n_keys=tp, q_pos0=0, tq=_row_tile(tp, ATTN_QUERY_TILE))
    n_keys_s = p_len + ts
    geom_s = dict(bsz=bs, t=ts, s_pad=-(-n_keys_s // V7X_LANES) * V7X_LANES, n_keys=n_keys_s, q_pos0=p_len, tq=ts)
    past = (cache_a_k.reshape(depth, bs, p_len * A_KV_HEADS, A_HEAD_DIM),
            cache_a_v.reshape(depth, bs, p_len * A_KV_HEADS, A_HEAD_DIM),
            cache_a_kidx,
            cache_b_k.reshape(depth, bs, p_len * B_HEADS, 2 * B_QK_DIM),
            cache_b_v.reshape(depth, bs, p_len * B_HEADS, B_V_DIM),
            state_c)

    xp = x_prompt.reshape(bp * tp, d)
    xs = x_sample.reshape(bs * ts, d)
    caches_p = caches_s = None
    states_p, states_s = [], []
    for layer in range(depth):
        w_f32 = dict(ffn1=("f32", w_ffn1_in, w_ffn1_out, layer), proj=("f32", w_in), out=("f32", w_out, layer),
                     ffn2=("f32", w_ffn2_in, w_ffn2_out, layer))
        xs, caches_s, s_c, w_bf = _layer(xs, layer, w_f32, p, caches_s, past, geom_s)
        states_s.append(s_c)
        xp, caches_p, s_c, _ = _layer(xp, layer, w_bf, p, caches_p, None, geom_p)
        states_p.append(s_c)

    def group_out(x, caches, states, bsz, t):
        a_k, a_v, a_ki, b_k, b_v = caches
        return (x.reshape(bsz, t, d),
                a_k.reshape(depth, bsz, t, A_KV_HEADS, A_HEAD_DIM), a_v.reshape(depth, bsz, t, A_KV_HEADS, A_HEAD_DIM),
                a_ki.reshape(depth, bsz, t, IDX_DIM), b_k.reshape(depth, bsz, t, B_HEADS, 2 * B_QK_DIM),
                b_v.reshape(depth, bsz, t, B_HEADS, B_V_DIM), jnp.stack(states))

    out_p = group_out(xp, caches_p, states_p, bp, tp)
    out_s = group_out(xs, caches_s, states_s, bs, ts)
    return (out_p[0], out_s[0], *out_p[1:], *out_s[1:])
```

```python
import functools
import math

import jax
import jax.numpy as jnp
from jax import lax
from jax.experimental import pallas as pl
from jax.experimental.pallas import tpu as pltpu

F32 = jnp.float32
BF16 = jnp.bfloat16

CHUNK = 64
A_HEADS, A_KV_HEADS, A_HEAD_DIM = 4, 2, 128
IDX_HEADS, IDX_DIM = 8, 64
TOPK_MAX = 256
B_HEADS, B_QK_DIM, B_V_DIM = 4, 64, 128
C_HEADS, C_K_DIM, C_V_DIM = 8, 128, 128
NORM_EPS = 1e-6
NEG_BIG = -1e30
LB_FLOOR = 1e-30

A_Q_W = A_HEADS * A_HEAD_DIM
A_KV_W = A_KV_HEADS * A_HEAD_DIM
A_QI_W = IDX_HEADS * IDX_DIM
B_QK_W = B_HEADS * 2 * B_QK_DIM
B_V_W = B_HEADS * B_V_DIM
C_W = C_HEADS * C_K_DIM
MIX_W = A_Q_W + B_V_W + C_W
MAIN_W = A_Q_W + 2 * A_KV_W + A_QI_W + 2 * B_QK_W + B_V_W + 4 * C_W
NARROW_W = IDX_DIM + IDX_HEADS
SMALL_W = 128
PROJ_TN = 1024
ATTN_QUERY_TILE = 256
ATTN_KEY_CHUNK = 256
COUNT_ROWS = 64
DIFF_HEADS_PER_LOOP = 2
HGRN_CHUNK = 256
COL_A_Q, COL_A_K, COL_A_V, COL_A_QI = 0, 512, 768, 1024
COL_B_Q, COL_B_K, COL_B_V = 1536, 2048, 2560
COL_C_Q, COL_C_F, COL_C_I, COL_C_G = 3072, 4096, 5120, 6144

V7X_LANES = 128
V7X_VMEM_LIMIT = 56 * 1024 * 1024


def _cparams(semantics):
    return pltpu.CompilerParams(dimension_semantics=semantics, vmem_limit_bytes=V7X_VMEM_LIMIT)


def _rmsnorm(x, g):
    return x * lax.rsqrt(jnp.mean(x * x, axis=-1, keepdims=True) + NORM_EPS) * g


def _dot(a, b):
    return jnp.dot(a, b, preferred_element_type=F32)


def _dot_nt(a, b):
    return lax.dot_general(a, b, (((1,), (1,)), ((), ())), preferred_element_type=F32)


def _dot_tn(a, b):
    return lax.dot_general(a, b, (((0,), (0,)), ((), ())), preferred_element_type=F32)


def _row_tile(n, pref):
    t = min(n, pref)
    while n % t:
        t //= 2
    return t


def _ffn_body(*refs, n_ff_steps, final_norm, emit):
    x_ref, g_ref, wg_ref, wu_ref, wo_ref = refs[:5]
    gf_ref = refs[5] if final_norm else None
    outs = refs[5 + final_norm:]
    o_ref, xn_ref = outs[0], outs[-1]
    j = pl.program_id(1)

    @pl.when(j == 0)
    def _():
        x = x_ref[...]
        xn_ref[...] = _rmsnorm(x, g_ref[...]).astype(BF16)
        o_ref[...] = x

    wg, wu, wo = wg_ref[...].astype(BF16), wu_ref[...].astype(BF16), wo_ref[...].astype(BF16)
    if emit:
        outs[1][...] = wg
        outs[2][...] = wu
        outs[3][...] = wo
    xn = xn_ref[...]
    gate = _dot(xn, wg)
    up = _dot(xn, wu)
    act = (gate * jax.nn.sigmoid(gate)) * up * 0.5
    o_ref[...] += _dot(act.astype(BF16), wo)

    if final_norm:
        @pl.when(j == n_ff_steps - 1)
        def _():
            o_ref[...] = _rmsnorm(o_ref[...], gf_ref[...])


def _ffn(x, g, w, g_final=None):
    n, d = x.shape
    emit = w[0] == "f32"
    if emit:
        _, w_in, w_out, layer = w
        f = w_out.shape[1]
    else:
        _, wg, wu, wo = w
        f = wo.shape[0]
    tm = _row_tile(n, 512)
    tf = 512 if f % 512 == 0 else f
    nj = f // tf
    assert not emit or n == tm, "the weight-emitting variant expects a single row tile"
    final_norm = g_final is not None
    in_specs = [pl.BlockSpec((tm, d), lambda i, j: (i, 0)), pl.BlockSpec((1, d), lambda i, j: (0, 0))]
    if emit:
        in_specs += [pl.BlockSpec((None, d, tf), lambda i, j: (layer, 0, j)),
                     pl.BlockSpec((None, d, tf), lambda i, j: (layer, 0, j + nj)),
                     pl.BlockSpec((None, tf, d), lambda i, j: (layer, j, 0))]
        args = [x, g.reshape(1, d), w_in, w_in, w_out]
    else:
        in_specs += [pl.BlockSpec((d, tf), lambda i, j: (0, j)), pl.BlockSpec((d, tf), lambda i, j: (0, j)),
                     pl.BlockSpec((tf, d), lambda i, j: (j, 0))]
        args = [x, g.reshape(1, d), wg, wu, wo]
    if final_norm:
        in_specs.append(pl.BlockSpec((1, d), lambda i, j: (0, 0)))
        args.append(g_final.reshape(1, d))
    out_specs = [pl.BlockSpec((tm, d), lambda i, j: (i, 0))]
    out_shape = [jax.ShapeDtypeStruct((n, d), F32)]
    if emit:
        out_specs += [pl.BlockSpec((d, tf), lambda i, j: (0, j)), pl.BlockSpec((d, tf), lambda i, j: (0, j)),
                      pl.BlockSpec((tf, d), lambda i, j: (j, 0))]
        out_shape += [jax.ShapeDtypeStruct((d, f), BF16), jax.ShapeDtypeStruct((d, f), BF16),
                      jax.ShapeDtypeStruct((f, d), BF16)]
    res = pl.pallas_call(
        functools.partial(_ffn_body, n_ff_steps=nj, final_norm=final_norm, emit=emit),
        grid=(n // tm, nj),
        in_specs=in_specs,
        out_specs=out_specs,
        out_shape=out_shape,
        scratch_shapes=[pltpu.VMEM((tm, d), BF16)],
        compiler_params=_cparams(("parallel", "arbitrary")),
        name="ffn_emit" if emit else "ffn",
    )(*args)
    return (res[0], ("bf16", res[1], res[2], res[3])) if emit else (res[0], w)


def _proj_body(*refs, emit, tn, tm, n_alias, layer):
    n_w = 3 if emit else 2
    x_ref, g_ref = refs[:2]
    w_refs = refs[2:2 + n_w]
    outs = refs[2 + n_w + n_alias:]
    om_ref, os_ref, ak_ref, av_ref, aki_ref, bk_ref, bv_ref = outs[:7]
    xn_ref = refs[-1]
    j = pl.program_id(1)

    def this_layer(ref):
        if n_alias:
            return ref
        for other in range(ref.shape[0]):
            if other != layer:
                ref[other] = jnp.zeros(ref.shape[1:], F32)
        return ref.at[layer]

    if emit:
        wa_ref, wx_ref, wsm_ref = w_refs
        wmo_ref, wso_ref = outs[7:9]
        n_aligned = COL_B_Q // tn

        @pl.when(j < n_aligned)
        def _():
            wmo_ref[...] = wa_ref[...].astype(BF16)

        @pl.when(j >= n_aligned)
        def _():
            wide = jnp.concatenate([wa_ref[...], wx_ref[...]], axis=1)
            wmo_ref[...] = wide[:, NARROW_W:NARROW_W + tn].astype(BF16)

        wm = wmo_ref[...]
    else:
        wm = w_refs[0][...]

    @pl.when(j == 0)
    def _():
        xn = _rmsnorm(x_ref[...], g_ref[...]).astype(BF16)
        xn_ref[...] = xn
        if emit:
            lane = lax.broadcasted_iota(jnp.int32, wsm_ref.shape, 1)
            ws = jnp.where(lane < NARROW_W, wsm_ref[...], 0.0).astype(BF16)
            wso_ref[...] = ws
        else:
            ws = w_refs[1][...]
        small = _dot(xn, ws)
        os_ref[...] = small
        this_layer(aki_ref)[...] = small[:, 0:IDX_DIM]

    acc = _dot(xn_ref[...], wm)
    om_ref[...] = acc

    def put_heads(ref, col, heads):
        off = col % tn
        assert off + heads * V7X_LANES <= tn

        @pl.when(j == col // tn)
        def _():
            dst = this_layer(ref)
            for hd in range(heads):
                dst[pl.ds(hd, tm, stride=heads), :] = acc[:, off + hd * V7X_LANES:off + (hd + 1) * V7X_LANES]

    put_heads(ak_ref, COL_A_K, A_KV_HEADS)
    put_heads(av_ref, COL_A_V, A_KV_HEADS)
    put_heads(bk_ref, COL_B_K, B_HEADS)
    put_heads(bv_ref, COL_B_V, B_HEADS)


def _proj(x, g, w, caches, layer, depth):
    n, d = x.shape
    emit = w[0] == "f32"
    tm = _row_tile(n, 512)
    tn = 512 if emit else PROJ_TN
    assert not emit or n == tm, "the weight-emitting variant expects a single row tile"
    lanes_per_tn = tn // V7X_LANES
    if emit:
        assert COL_B_Q % tn == 0
        w_in = w[1]
        w_specs = [pl.BlockSpec((None, d, tn), lambda i, j: (layer, 0, j)),
                   pl.BlockSpec((None, d, V7X_LANES), lambda i, j: (layer, 0, lanes_per_tn * (j + 1))),
                   pl.BlockSpec((None, d, SMALL_W), lambda i, j: (layer, 0, COL_B_Q // SMALL_W))]
        w_args = [w_in, w_in, w_in]
    else:
        w_specs = [pl.BlockSpec((d, tn), lambda i, j: (0, j)), pl.BlockSpec((d, SMALL_W), lambda i, j: (0, 0))]
        w_args = list(w[1:])
    cache_rows = (A_KV_HEADS, A_KV_HEADS, 1, B_HEADS, B_HEADS)
    cache_lanes = (V7X_LANES, V7X_LANES, IDX_DIM, V7X_LANES, V7X_LANES)
    alias_args = [] if caches is None else list(caches)
    out_specs = [pl.BlockSpec((tm, tn), lambda i, j: (i, j)), pl.BlockSpec((tm, SMALL_W), lambda i, j: (i, 0))]
    if alias_args:
        cache_spec = lambda r, wd: pl.BlockSpec((None, r * tm, wd), lambda i, j: (layer, i, 0))
    else:
        cache_spec = lambda r, wd: pl.BlockSpec((depth, r * tm, wd), lambda i, j: (0, i, 0))
    out_specs += [cache_spec(r, wd) for r, wd in zip(cache_rows, cache_lanes)]
    out_shape = [jax.ShapeDtypeStruct((n, MAIN_W), F32), jax.ShapeDtypeStruct((n, SMALL_W), F32)]
    out_shape += [jax.ShapeDtypeStruct((depth, r * n, wd), F32) for r, wd in zip(cache_rows, cache_lanes)]
    if emit:
        out_specs += [pl.BlockSpec((d, tn), lambda i, j: (0, j)), pl.BlockSpec((d, SMALL_W), lambda i, j: (0, 0))]
        out_shape += [jax.ShapeDtypeStruct((d, MAIN_W), BF16), jax.ShapeDtypeStruct((d, SMALL_W), BF16)]
    n_in = 2 + len(w_args)
    res = pl.pallas_call(
        functools.partial(_proj_body, emit=emit, tn=tn, tm=tm, n_alias=len(alias_args), layer=layer),
        grid=(n // tm, MAIN_W // tn),
        in_specs=[pl.BlockSpec((tm, d), lambda i, j: (i, 0)), pl.BlockSpec((1, d), lambda i, j: (0, 0))]
        + w_specs + [pl.BlockSpec(memory_space=pl.ANY)] * len(alias_args),
        out_specs=out_specs,
        out_shape=out_shape,
        input_output_aliases={n_in + k: 2 + k for k in range(len(alias_args))},
        scratch_shapes=[pltpu.VMEM((tm, d), BF16)],
        compiler_params=_cparams(("parallel", "arbitrary")),
        name="in_proj_emit" if emit else "in_proj",
    )(x, g.reshape(1, d), *w_args, *alias_args)
    w_bf = ("bf16", res[7], res[8]) if emit else w
    return res[0], res[1], tuple(res[2:7]), w_bf


def _sortable_key(x):
    bits = lax.bitcast_convert_type(x, jnp.int32)
    return jnp.where(bits < 0, bits ^ jnp.int32(0x7FFFFFFF), bits)


def _kth_largest(key, k):
    rows = key.shape[0]

    def body(it, t):
        cand = t + lax.shift_left(jnp.int32(1), jnp.int32(31) - it)
        cnt = jnp.sum(jnp.where(key >= cand, 1.0, 0.0), axis=1, keepdims=True)
        return jnp.where(cnt >= k, cand, t)

    t0 = jnp.full((rows, 1), jnp.iinfo(jnp.int32).min, jnp.int32)
    return lax.fori_loop(0, 32, body, t0)


def _tie_cut(tie, kpos, need, nbits):
    rows = tie.shape[0]

    def body(it, t):
        cand = t + lax.shift_left(jnp.int32(1), jnp.int32(nbits - 1) - it)
        cnt = jnp.sum(jnp.where(tie & (kpos < cand), 1.0, 0.0), axis=1, keepdims=True)
        return jnp.where(cnt < need, cand, t)

    return lax.fori_loop(0, nbits, body, jnp.zeros((rows, 1), jnp.int32))


def _visible(tq, s_pad, q_pos0, n_keys):
    qpos = q_pos0 + pl.program_id(1) * tq + lax.broadcasted_iota(jnp.int32, (tq, 1), 0)
    limit = jnp.minimum(((qpos >> 6) + 1) << 6, n_keys)
    kpos = lax.broadcasted_iota(jnp.int32, (tq, s_pad), 1)
    return kpos < limit, kpos


def _head_keys(new_ref, lo, width, past_ref, head, heads, s_pad):
    new = new_ref[:, lo:lo + width]
    if past_ref is None:
        assert new.shape[0] == s_pad
        return new.astype(BF16)
    p_len = past_ref.shape[0] // heads
    hist = past_ref[...] if heads == 1 else past_ref[pl.ds(head, p_len, stride=heads), :]
    parts = [hist, new]
    pad = s_pad - p_len - new.shape[0]
    if pad:
        parts.append(jnp.zeros((pad, width), F32))
    return jnp.concatenate(parts, axis=0).astype(BF16)


def _dsa_body(*refs, tq, s_pad, n_keys, q_pos0, topk, has_past):
    q_ref, qi_ref, wq_ref, k_ref, v_ref, ki_ref = refs[:6]
    pk_ref, pv_ref, pki_ref = refs[6:9] if has_past else (None, None, None)
    o_ref = refs[-1]
    vis, kpos = _visible(tq, s_pad, q_pos0, n_keys)

    kidx = _head_keys(ki_ref, 0, IDX_DIM, pki_ref, 0, 1, s_pad)
    wq = wq_ref[:, IDX_DIM:IDX_DIM + IDX_HEADS] * (IDX_HEADS ** -0.5)
    score = jnp.zeros((tq, s_pad), F32)
    for h in range(IDX_HEADS):
        qi_h = qi_ref[:, h * IDX_DIM:(h + 1) * IDX_DIM].astype(BF16)
        logit = _dot_nt(qi_h, kidx) * (IDX_DIM ** -0.5)
        score = score + jnp.maximum(logit, 0.0) * wq[:, h:h + 1]
    score = jnp.where(vis, score + 0.0, NEG_BIG)

    key = _sortable_key(score)
    thr = _kth_largest(key, float(topk))
    n_gt = jnp.sum(jnp.where(key > thr, 1.0, 0.0), axis=1, keepdims=True)
    tie = key == thr
    cut = _tie_cut(tie, kpos, float(topk) - n_gt, max(1, (s_pad - 1).bit_length()))
    sel = ((key > thr) | (tie & (kpos <= cut))) & vis

    rep = A_HEADS // A_KV_HEADS
    for g in range(A_KV_HEADS):
        k_g = _head_keys(k_ref, g * A_HEAD_DIM, A_HEAD_DIM, pk_ref, g, A_KV_HEADS, s_pad)
        v_g = _head_keys(v_ref, g * A_HEAD_DIM, A_HEAD_DIM, pv_ref, g, A_KV_HEADS, s_pad)
        for h in range(g * rep, (g + 1) * rep):
            q_h = q_ref[:, h * A_HEAD_DIM:(h + 1) * A_HEAD_DIM].astype(BF16)
            att = jnp.where(sel, _dot_nt(q_h, k_g) * (A_HEAD_DIM ** -0.5), NEG_BIG)
            p = jnp.exp(att - jnp.max(att, axis=1, keepdims=True))
            l = jnp.sum(p, axis=1, keepdims=True)
            o = _dot(p.astype(BF16), v_g) / l
            o_ref[:, h * A_HEAD_DIM:(h + 1) * A_HEAD_DIM] = o.astype(o_ref.dtype)


def _softmax_step_t(carry, s, v_c):
    m, l, acc = carry
    m_new = jnp.maximum(m, _col_reduce(s, jnp.max))
    alpha = jnp.exp(m - m_new)
    p = jnp.exp(s - m_new)
    return m_new, alpha * l + _col_reduce(p, jnp.sum), alpha * acc + _dot_tn(v_c, p.astype(BF16))


def _col_reduce(x, op):
    rows, tq = x.shape
    return op(op(x.reshape(rows // COUNT_ROWS, COUNT_ROWS, tq), axis=0), axis=0, keepdims=True)


def _softmax_init_t(tq, width):
    return (jnp.full((1, tq), -jnp.inf, F32), jnp.zeros((1, tq), F32), jnp.zeros((width, tq), F32))


def _causal_extent(tq, kc):
    i = pl.program_id(1)
    qpos = i * tq + lax.broadcasted_iota(jnp.int32, (1, tq), 1)
    limit = ((qpos >> 6) + 1) << 6
    return (i * tq + tq + kc - 1) // kc, limit, lax.broadcasted_iota(jnp.int32, (kc, 1), 0)


def _dsa_causal_body(q_ref, qi_ref, wq_ref, k_ref, v_ref, ki_ref, o_ref, key_ref, cut_ref, *, tq, kc, topk, nbits):
    n_chunks, limit, krow = _causal_extent(tq, kc)
    wq_t = wq_ref[...].T[IDX_DIM:IDX_DIM + IDX_HEADS, :] * (IDX_HEADS ** -0.5)
    qi_all = jnp.concatenate([qi_ref[:, h * IDX_DIM:(h + 1) * IDX_DIM] for h in range(IDX_HEADS)], axis=0).astype(BF16)

    def score_chunk(c, carry):
        start = pl.multiple_of(c * kc, kc)
        kidx = ki_ref[pl.ds(start, kc), 0:IDX_DIM].astype(BF16)
        logits = _dot_nt(kidx, qi_all)
        acc = jnp.zeros((kc, tq), F32)
        for h in range(IDX_HEADS):
            acc = acc + jnp.maximum(logits[:, h * tq:(h + 1) * tq] * (IDX_DIM ** -0.5), 0.0) * wq_t[h:h + 1, :]
        score = jnp.where(start + krow < limit, acc + 0.0, NEG_BIG)
        key_ref[c] = _sortable_key(score)
        return carry

    lax.fori_loop(0, n_chunks, score_chunk, 0)

    def count(pred):
        def body(c, part):
            ones = jnp.where(pred(key_ref[c], c), 1.0, 0.0)
            return part + jnp.sum(ones.reshape(kc // COUNT_ROWS, COUNT_ROWS, tq), axis=0)

        part = lax.fori_loop(0, n_chunks, body, jnp.zeros((COUNT_ROWS, tq), F32))
        return jnp.sum(part, axis=0, keepdims=True)

    def thr_step(it, t):
        cand = t + lax.shift_left(jnp.int32(1), jnp.int32(31) - it)
        return jnp.where(count(lambda key, c: key >= cand) >= float(topk), cand, t)

    thr = lax.fori_loop(0, 32, thr_step, jnp.full((1, tq), jnp.iinfo(jnp.int32).min, jnp.int32))
    need = float(topk) - count(lambda key, c: key > thr)
    n_tie = count(lambda key, c: key == thr)

    cut_ref[...] = jnp.full(cut_ref.shape, jnp.iinfo(jnp.int32).max, jnp.int32)

    @pl.when(jnp.max(jnp.where(n_tie != need, 1.0, 0.0)) > 0.0)
    def _():
        def cut_step(it, t):
            cand = t + lax.shift_left(jnp.int32(1), jnp.int32(nbits - 1) - it)
            below = count(lambda key, c: (key == thr) & (c * kc + krow < cand))
            return jnp.where(below < need, cand, t)

        cut = lax.fori_loop(0, nbits, cut_step, jnp.zeros((1, tq), jnp.int32))
        cut_ref[...] = jnp.broadcast_to(cut, cut_ref.shape)

    cut = cut_ref[0:1, :]

    def select_chunk(c, carry):
        key = key_ref[c]
        kpos = c * kc + krow
        tie_flag = jnp.where(key == thr, jnp.where(kpos <= cut, 0, 1), 1)
        key_ref[c] = jnp.where(kpos < limit, jnp.where(key > thr, 0, tie_flag), 1)
        return carry

    lax.fori_loop(0, n_chunks, select_chunk, 0)

    rep = A_HEADS // A_KV_HEADS
    q_g = [jnp.concatenate([q_ref[:, h * A_HEAD_DIM:(h + 1) * A_HEAD_DIM] for h in range(g * rep, (g + 1) * rep)],
                           axis=0).astype(BF16) for g in range(A_KV_HEADS)]

    def att_chunk(c, carry):
        start = pl.multiple_of(c * kc, kc)
        attend = key_ref[c] == 0
        attend = jnp.concatenate([attend] * rep, axis=1)
        logits = [_dot_nt(k_ref[pl.ds(start, kc), g * A_HEAD_DIM:(g + 1) * A_HEAD_DIM].astype(BF16), q_g[g])
                  for g in range(A_KV_HEADS)]
        v_c = [v_ref[pl.ds(start, kc), g * A_HEAD_DIM:(g + 1) * A_HEAD_DIM].astype(BF16) for g in range(A_KV_HEADS)]
        return tuple(
            _softmax_step_t(carry[g], jnp.where(attend, logits[g] * (A_HEAD_DIM ** -0.5), NEG_BIG), v_c[g])
            for g in range(A_KV_HEADS))

    res = lax.fori_loop(0, n_chunks, att_chunk, tuple(_softmax_init_t(rep * tq, A_HEAD_DIM) for _ in range(A_KV_HEADS)))
    for g in range(A_KV_HEADS):
        _, l, acc = res[g]
        o_t = acc / l
        for r in range(rep):
            h = g * rep + r
            o_ref[:, h * A_HEAD_DIM:(h + 1) * A_HEAD_DIM] = o_t[:, r * tq:(r + 1) * tq].T.astype(o_ref.dtype)


def _diff_causal_body(q_ref, k_ref, v_ref, lam_ref, g_ref, o_ref, *, tq, kc, lam_init):
    n_chunks, limit, krow = _causal_extent(tq, kc)
    lv = lam_ref[...]
    lam = (jnp.exp(jnp.sum(lv[0:1] * lv[1:2], axis=1, keepdims=True))
           - jnp.exp(jnp.sum(lv[2:3] * lv[3:4], axis=1, keepdims=True)) + lam_init)
    gain = g_ref[...] * (1.0 - lam_init)
    head_w = 2 * B_QK_DIM
    qlane = lax.broadcasted_iota(jnp.int32, (tq, head_w), 1)
    for h0 in range(0, B_HEADS, DIFF_HEADS_PER_LOOP):
        heads = range(h0, h0 + DIFF_HEADS_PER_LOOP)
        q_2 = [jnp.concatenate([jnp.where((qlane >= c * B_QK_DIM) & (qlane < (c + 1) * B_QK_DIM),
                                          q_ref[:, h * head_w:(h + 1) * head_w], 0.0) for c in range(2)],
                               axis=0).astype(BF16) for h in heads]

        def chunk(c, carry, q_2=q_2, heads=heads):
            start = pl.multiple_of(c * kc, kc)
            vis = start + krow < jnp.concatenate([limit, limit], axis=1)
            logits = [_dot_nt(k_ref[pl.ds(start, kc), h * head_w:(h + 1) * head_w].astype(BF16), q_2[j])
                      for j, h in enumerate(heads)]
            v_c = [v_ref[pl.ds(start, kc), h * B_V_DIM:(h + 1) * B_V_DIM].astype(BF16) for h in heads]
            return tuple(
                _softmax_step_t(carry[j], jnp.where(vis, logits[j] * (B_QK_DIM ** -0.5), NEG_BIG), v_c[j])
                for j in range(len(heads)))

        res = lax.fori_loop(0, n_chunks, chunk, tuple(_softmax_init_t(2 * tq, B_V_DIM) for _ in heads))
        for j, h in enumerate(heads):
            _, l, acc = res[j]
            o_t = acc / l
            o = (o_t[:, 0:tq] - lam * o_t[:, tq:2 * tq]).T
            o = o * lax.rsqrt(jnp.mean(o * o, axis=1, keepdims=True) + NORM_EPS) * gain
            o_ref[:, h * B_V_DIM:(h + 1) * B_V_DIM] = o.astype(o_ref.dtype)


def _blk(arr_col, rows, width, kind, nq):
    _, col = arr_col
    cb = col // width
    assert col % width == 0
    if kind == "q":
        return pl.BlockSpec((rows, width), lambda b, i: (b * nq + i, cb))
    return pl.BlockSpec((rows, width), lambda b, i: (b, cb))


def _past_blk(cache, layer):
    return pl.BlockSpec((None, None) + cache.shape[2:], lambda b, i: (layer, b, 0, 0))


def _dsa(q, qi, wq, k, v, ki, past, *, bsz, t, s_pad, n_keys, q_pos0, tq, layer):
    nq = t // tq
    topk = min(TOPK_MAX, n_keys // 4)
    scratch = []
    if past is None:
        kc = _row_tile(t, ATTN_KEY_CHUNK)
        body = functools.partial(_dsa_causal_body, tq=tq, kc=kc, topk=topk, nbits=max(1, (t - 1).bit_length()))
        scratch = [pltpu.VMEM((t // kc, kc, tq), jnp.int32), pltpu.VMEM((8, tq), jnp.int32)]
    else:
        body = functools.partial(_dsa_body, tq=tq, s_pad=s_pad, n_keys=n_keys, q_pos0=q_pos0, topk=topk,
                                 has_past=True)
    in_specs = [
        _blk(q, tq, A_Q_W, "q", nq),
        _blk(qi, tq, A_QI_W, "q", nq),
        _blk(wq, tq, SMALL_W, "q", nq),
        _blk(k, t, A_KV_W, "kv", nq),
        _blk(v, t, A_KV_W, "kv", nq),
        _blk(ki, t, SMALL_W, "kv", nq),
    ]
    args = [q[0], qi[0], wq[0], k[0], v[0], ki[0]]
    if past is not None:
        in_specs += [_past_blk(c, layer) for c in past]
        args += list(past)
    return pl.pallas_call(
        body,
        grid=(bsz, nq),
        in_specs=in_specs,
        out_specs=pl.BlockSpec((tq, A_Q_W), lambda b, i: (b * nq + i, 0)),
        out_shape=jax.ShapeDtypeStruct((bsz * t, A_Q_W), BF16),
        scratch_shapes=scratch,
        compiler_params=_cparams(("parallel", "parallel")),
        name="dsa",
    )(*args)


def _diff_body(*refs, tq, s_pad, n_keys, q_pos0, lam_init, has_past):
    q_ref, k_ref, v_ref, lam_ref, g_ref = refs[:5]
    pk_ref, pv_ref = refs[5:7] if has_past else (None, None)
    o_ref = refs[-1]
    vis, _ = _visible(tq, s_pad, q_pos0, n_keys)
    lv = lam_ref[...]
    lam = (jnp.exp(jnp.sum(lv[0:1] * lv[1:2], axis=1, keepdims=True))
           - jnp.exp(jnp.sum(lv[2:3] * lv[3:4], axis=1, keepdims=True)) + lam_init)
    gain = g_ref[...] * (1.0 - lam_init)
    head_w = 2 * B_QK_DIM
    lane = lax.broadcasted_iota(jnp.int32, (tq, head_w), 1)
    for h in range(B_HEADS):
        k_h = _head_keys(k_ref, h * head_w, head_w, pk_ref, h, B_HEADS, s_pad)
        v_h = _head_keys(v_ref, h * B_V_DIM, B_V_DIM, pv_ref, h, B_HEADS, s_pad)
        q_h = q_ref[:, h * head_w:(h + 1) * head_w]
        parts = []
        for c in range(2):
            q_c = jnp.where((lane >= c * B_QK_DIM) & (lane < (c + 1) * B_QK_DIM), q_h, 0.0).astype(BF16)
            logit = jnp.where(vis, _dot_nt(q_c, k_h) * (B_QK_DIM ** -0.5), NEG_BIG)
            p = jnp.exp(logit - jnp.max(logit, axis=1, keepdims=True))
            l = jnp.sum(p, axis=1, keepdims=True)
            parts.append(_dot(p.astype(BF16), v_h) / l)
        o = parts[0] - lam * parts[1]
        o = o * lax.rsqrt(jnp.mean(o * o, axis=1, keepdims=True) + NORM_EPS) * gain
        o_ref[:, h * B_V_DIM:(h + 1) * B_V_DIM] = o.astype(o_ref.dtype)


def _diff(q, k, v, lam_vecs, g_diff, past, *, bsz, t, s_pad, n_keys, q_pos0, tq, layer):
    nq = t // tq
    lam_init = 0.8 - 0.6 * math.exp(-0.3 * layer)
    if past is None:
        body = functools.partial(_diff_causal_body, tq=tq, kc=_row_tile(t, ATTN_KEY_CHUNK), lam_init=lam_init)
    else:
        body = functools.partial(_diff_body, tq=tq, s_pad=s_pad, n_keys=n_keys, q_pos0=q_pos0, lam_init=lam_init,
                                 has_past=True)
    in_specs = [
        _blk(q, tq, B_QK_W, "q", nq),
        _blk(k, t, B_QK_W, "kv", nq),
        _blk(v, t, B_V_W, "kv", nq),
        pl.BlockSpec((None, 4, B_QK_DIM), lambda b, i: (layer, 0, 0)),
        pl.BlockSpec((None, 1, B_V_DIM), lambda b, i: (layer, 0, 0)),
    ]
    args = [q[0], k[0], v[0], lam_vecs, g_diff]
    if past is not None:
        in_specs += [_past_blk(c, layer) for c in past]
        args += list(past)
    return pl.pallas_call(
        body,
        grid=(bsz, nq),
        in_specs=in_specs,
        out_specs=pl.BlockSpec((tq, B_V_W), lambda b, i: (b * nq + i, 0)),
        out_shape=jax.ShapeDtypeStruct((bsz * t, B_V_W), BF16),
        compiler_params=_cparams(("parallel", "parallel")),
        name="diff_attn",
    )(*args)


def _cumsum_rows(x):
    n = x.shape[0]
    row = lax.broadcasted_iota(jnp.int32, (n, 1), 0)
    d = 1
    while d < n:
        x = x + jnp.where(row >= d, pltpu.roll(x, d, axis=0), 0.0)
        d *= 2
    return x


def _anchor_rows(b, c):
    n, w = b.shape
    if 2 * c >= 8:
        pieces = [jnp.broadcast_to(b[j * 2 * c + c - 1:j * 2 * c + c, :], (2 * c, w)) for j in range(n // (2 * c))]
        return pieces[0] if len(pieces) == 1 else jnp.concatenate(pieces, axis=0)
    rmod = lax.broadcasted_iota(jnp.int32, (n, 1), 0) & (2 * c - 1)
    m = b
    for s in range(-c, c):
        if s != 0:
            m = jnp.where(rmod == c - 1 - s, pltpu.roll(b, (-s) % n, axis=0), m)
    return m


def _hgrn_body(*refs, chunk, layer, has_s0):
    if has_s0:
        q_ref, f_ref, i_ref, g_ref, glb_ref, gn_ref, s0_ref, o_ref, s_ref, st_ref = refs
    else:
        q_ref, f_ref, i_ref, g_ref, glb_ref, gn_ref, o_ref, s_ref, st_ref = refs
    ci = pl.program_id(1)

    @pl.when(ci == 0)
    def _():
        for h in range(C_HEADS):
            st_ref[h] = s0_ref[h].T if has_s0 else jnp.zeros((C_V_DIM, C_K_DIM), F32)

    glb = glb_ref[...]
    pe = jnp.exp(glb - jnp.max(glb, axis=0, keepdims=True))
    pn = pe / jnp.sum(pe, axis=0, keepdims=True)
    lb = jnp.sum(pn[0:layer + 1], axis=0, keepdims=True) - pn[0:1]

    zf = f_ref[...]
    log_sig = jnp.minimum(zf, 0.0) - jnp.log1p(jnp.exp(-jnp.abs(zf)))
    la = jnp.log(jnp.maximum(lb, LB_FLOOR))
    lc = jnp.log1p(-lb) + log_sig
    hi = jnp.maximum(la, lc)
    log_f = hi + jnp.log1p(jnp.exp(jnp.minimum(la, lc) - hi))
    kc = (1.0 - lb) * jax.nn.sigmoid(-zf)
    b_all = _cumsum_rows(log_f)

    row = lax.broadcasted_iota(jnp.int32, (chunk, 1), 0)
    rr = lax.broadcasted_iota(jnp.int32, (chunk, chunk), 0)
    cc = lax.broadcasted_iota(jnp.int32, (chunk, chunk), 1)
    heads = range(C_HEADS)
    head = lambda x, h: x[:, h * C_K_DIM:(h + 1) * C_K_DIM]

    q_all = q_ref[...]
    v_all = i_ref[...]
    vb_all = v_all.astype(BF16)
    b_last = b_all[chunk - 1:chunk, :]
    qe_all = (q_all * jnp.exp(b_all)).astype(BF16)
    kdec_all = (kc * jnp.exp(b_last - b_all)).astype(BF16)
    diag_all = q_all * kc
    levels = []
    c = chunk // 2
    while c >= 1:
        lg = c.bit_length() - 1
        upper = ((row >> lg) & 1) == 1
        d = b_all - _anchor_rows(b_all, c)
        e = jnp.exp(jnp.where(upper, d, -d))
        x = (jnp.where(upper, q_all, kc) * e).astype(BF16)
        pairs = ((rr >> (lg + 1)) == (cc >> (lg + 1))) & (((rr >> lg) & 1) == 1) & (((cc >> lg) & 1) == 0)
        levels.append((x, pairs))
        c //= 2

    st = [st_ref[h] for h in heads]
    o_inter = [_dot_nt(head(qe_all, h), st[h].astype(BF16)) for h in heads]
    a = []
    for h in heads:
        a_h = jnp.zeros((chunk, chunk), F32)
        for x, pairs in levels:
            a_h = jnp.where(pairs, _dot_nt(head(x, h), head(x, h)), a_h)
        a.append(a_h)
    for h in heads:
        st_new = st[h] * jnp.exp(head(b_last, h)) + _dot_tn(head(vb_all, h), head(kdec_all, h))
        st_ref[h] = st_new

        @pl.when(ci == pl.num_programs(1) - 1)
        def _():
            s_ref[0, h] = st_new.T

    gn = gn_ref[...]
    for h in heads:
        o = o_inter[h] + jnp.sum(head(diag_all, h), axis=1, keepdims=True) * head(v_all, h)
        o = o + _dot(a[h].astype(BF16), head(vb_all, h))
        o = o * lax.rsqrt(jnp.mean(o * o, axis=1, keepdims=True) + NORM_EPS) * gn
        zg = head(g_ref, h)
        o_ref[:, h * C_V_DIM:(h + 1) * C_V_DIM] = (o * (zg * jax.nn.sigmoid(zg))).astype(o_ref.dtype)


def _hgrn(cq, cf, cin, cg, gamma_lb, g_hgrn, s0, *, bsz, t, layer):
    chunk = _row_tile(t, HGRN_CHUNK)
    nc = t // chunk
    depth = gamma_lb.shape[0]
    has_s0 = s0 is not None

    def seq_blk(arr_col):
        cb = arr_col[1] // C_W
        return pl.BlockSpec((chunk, C_W), lambda b, c: (b * nc + c, cb))

    in_specs = [seq_blk(cq), seq_blk(cf), seq_blk(cin), seq_blk(cg),
                pl.BlockSpec((depth, C_W), lambda b, c: (0, 0)),
                pl.BlockSpec((None, 1, C_V_DIM), lambda b, c: (layer, 0, 0))]
    args = [cq[0], cf[0], cin[0], cg[0], gamma_lb, g_hgrn]
    if has_s0:
        in_specs.append(pl.BlockSpec((None, None, C_HEADS, C_K_DIM, C_V_DIM), lambda b, c: (layer, b, 0, 0, 0)))
        args.append(s0)
    return pl.pallas_call(
        functools.partial(_hgrn_body, chunk=chunk, layer=layer, has_s0=has_s0),
        grid=(bsz, nc),
        in_specs=in_specs,
        out_specs=[pl.BlockSpec((chunk, C_W), lambda b, c: (b * nc + c, 0)),
                   pl.BlockSpec((1, C_HEADS, C_K_DIM, C_V_DIM), lambda b, c: (b, 0, 0, 0))],
        out_shape=[jax.ShapeDtypeStruct((bsz * t, C_W), BF16),
                   jax.ShapeDtypeStruct((bsz, C_HEADS, C_K_DIM, C_V_DIM), F32)],
        scratch_shapes=[pltpu.VMEM((C_HEADS, C_V_DIM, C_K_DIM), F32)],
        compiler_params=_cparams(("parallel", "arbitrary")),
        name="hgrn2",
    )(*args)


def _outproj_body(*refs, emit):
    h_ref, oa_ref, ob_ref, oc_ref, wa_ref, wb_ref, wc_ref, o_ref = refs[:8]
    wa, wb, wc = wa_ref[...].astype(BF16), wb_ref[...].astype(BF16), wc_ref[...].astype(BF16)
    if emit:
        refs[8][...] = wa
        refs[9][...] = wb
        refs[10][...] = wc
    o_ref[...] = h_ref[...] + (_dot(oa_ref[...], wa) + _dot(ob_ref[...], wb) + _dot(oc_ref[...], wc))


def _outproj(h, oa, ob, oc, w):
    n, d = h.shape
    emit = w[0] == "f32"
    tm = _row_tile(n, 512)
    assert not emit or n == tm, "the weight-emitting variant expects a single row tile"
    row = lambda wd: pl.BlockSpec((tm, wd), lambda i: (i, 0))
    whole = lambda r: pl.BlockSpec((r, d), lambda i: (0, 0))
    seg_rows = (A_Q_W, B_V_W, C_W)
    seg_first = (0, A_Q_W, A_Q_W + B_V_W)
    if emit:
        _, w_out, layer = w
        w_specs = [pl.BlockSpec((None, r, d), functools.partial(lambda i, blk: (layer, blk, 0), blk=f // r))
                   for r, f in zip(seg_rows, seg_first)]
        w_args = [w_out] * 3
    else:
        w_specs = [whole(r) for r in seg_rows]
        w_args = list(w[1:])
    out_specs = [row(d)]
    out_shape = [jax.ShapeDtypeStruct((n, d), F32)]
    if emit:
        out_specs += [whole(r) for r in seg_rows]
        out_shape += [jax.ShapeDtypeStruct((r, d), BF16) for r in seg_rows]
    res = pl.pallas_call(
        functools.partial(_outproj_body, emit=emit),
        grid=(n // tm,),
        in_specs=[row(d), row(A_Q_W), row(B_V_W), row(C_W)] + w_specs,
        out_specs=out_specs,
        out_shape=out_shape,
        compiler_params=_cparams(("parallel",)),
        name="out_proj_emit" if emit else "out_proj",
    )(h, oa, ob, oc, *w_args)
    return (res[0], ("bf16", res[1], res[2], res[3])) if emit else (res[0], w)


def _layer(x, layer, w, p, caches, past, geom):
    bsz, t = geom["bsz"], geom["t"]
    depth = p["g_ffn1"].shape[0]
    wb = {}
    h, wb["ffn1"] = _ffn(x, p["g_ffn1"][layer], w["ffn1"])
    main, small, caches, wb["proj"] = _proj(h, p["g_mix"][layer], w["proj"], caches, layer, depth)
    if past is None:
        past_a = past_b = s0 = None
    else:
        past_a, past_b, s0 = past[0:3], past[3:5], past[5]
    o_a = _dsa((main, COL_A_Q), (main, COL_A_QI), (small, 0), (main, COL_A_K), (main, COL_A_V), (small, 0), past_a,
               layer=layer, **geom)
    o_b = _diff((main, COL_B_Q), (main, COL_B_K), (main, COL_B_V), p["diff_lambda"], p["g_diff"], past_b,
                layer=layer, **geom)
    o_c, s_c = _hgrn((main, COL_C_Q), (main, COL_C_F), (main, COL_C_I), (main, COL_C_G),
                     p["gamma_lb"], p["g_hgrn"], s0, bsz=bsz, t=t, layer=layer)
    h, wb["out"] = _outproj(h, o_a, o_b, o_c, w["out"])
    x, wb["ffn2"] = _ffn(h, p["g_ffn2"][layer], w["ffn2"], p["g_final"] if layer == depth - 1 else None)
    return x, caches, s_c, wb


def kernel(x_prompt, x_sample, cache_a_k, cache_a_v, cache_a_kidx, cache_b_k, cache_b_v, state_c,
           g_ffn1, w_ffn1_in, w_ffn1_out, g_mix, w_in, diff_lambda, g_diff, gamma_lb, g_hgrn, w_out,
           g_ffn2, w_ffn2_in, w_ffn2_out, g_final):
    depth, d = w_in.shape[0], w_in.shape[1]
    bp, tp, _ = x_prompt.shape
    bs, ts, _ = x_sample.shape
    p_len = cache_a_k.shape[2]
    p = dict(g_ffn1=g_ffn1, g_mix=g_mix, diff_lambda=diff_lambda, g_diff=g_diff.reshape(depth, 1, B_V_DIM),
             gamma_lb=gamma_lb, g_hgrn=g_hgrn.reshape(depth, 1, C_V_DIM), g_ffn2=g_ffn2, g_final=g_final)
    geom_p = dict(bsz=bp, t=tp, s_pad=tp, n_keys=tp, q_pos0=0, tq=_row_tile(tp, ATTN_QUERY_TILE))
    n_keys_s = p_len + ts
    geom_s = dict(bsz=bs, t=ts, s_pad=-(-n_keys_s // V7X_LANES) * V7X_LANES, n_keys=n_keys_s, q_pos0=p_len, tq=ts)
    past = (cache_a_k.reshape(depth, bs, p_len * A_KV_HEADS, A_HEAD_DIM),
            cache_a_v.reshape(depth, bs, p_len * A_KV_HEADS, A_HEAD_DIM),
            cache_a_kidx,
            cache_b_k.reshape(depth, bs, p_len * B_HEADS, 2 * B_QK_DIM),
            cache_b_v.reshape(depth, bs, p_len * B_HEADS, B_V_DIM),
            state_c)

    xp = x_prompt.reshape(bp * tp, d)
    xs = x_sample.reshape(bs * ts, d)
    caches_p = caches_s = None
    states_p, states_s = [], []
    for layer in range(depth):
        w_f32 = dict(ffn1=("f32", w_ffn1_in, w_ffn1_out, layer), proj=("f32", w_in), out=("f32", w_out, layer),
                     ffn2=("f32", w_ffn2_in, w_ffn2_out, layer))
        xs, caches_s, s_c, w_bf = _layer(xs, layer, w_f32, p, caches_s, past, geom_s)
        states_s.append(s_c)
        xp, caches_p, s_c, _ = _layer(xp, layer, w_bf, p, caches_p, None, geom_p)
        states_p.append(s_c)

    def group_out(x, caches, states, bsz, t):
        a_k, a_v, a_ki, b_k, b_v = caches
        return (x.reshape(bsz, t, d),
                a_k.reshape(depth, bsz, t, A_KV_HEADS, A_HEAD_DIM), a_v.reshape(depth, bsz, t, A_KV_HEADS, A_HEAD_DIM),
                a_ki.reshape(depth, bsz, t, IDX_DIM), b_k.reshape(depth, bsz, t, B_HEADS, 2 * B_QK_DIM),
                b_v.reshape(depth, bsz, t, B_HEADS, B_V_DIM), jnp.stack(states))

    out_p = group_out(xp, caches_p, states_p, bp, tp)
    out_s = group_out(xs, caches_s, states_s, bs, ts)
    return (out_p[0], out_s[0], *out_p[1:], *out_s[1:])
```

```python
import functools
import math

import jax
import jax.numpy as jnp
from jax import lax
from jax.experimental import pallas as pl
from jax.experimental.pallas import tpu as pltpu

F32 = jnp.float32
BF16 = jnp.bfloat16

CHUNK = 64
A_HEADS, A_KV_HEADS, A_HEAD_DIM = 4, 2, 128
IDX_HEADS, IDX_DIM = 8, 64
TOPK_MAX = 256
B_HEADS, B_QK_DIM, B_V_DIM = 4, 64, 128
C_HEADS, C_K_DIM, C_V_DIM = 8, 128, 128
NORM_EPS = 1e-6
NEG_BIG = -1e30
LB_FLOOR = 1e-30

A_Q_W = A_HEADS * A_HEAD_DIM
A_KV_W = A_KV_HEADS * A_HEAD_DIM
A_QI_W = IDX_HEADS * IDX_DIM
B_QK_W = B_HEADS * 2 * B_QK_DIM
B_V_W = B_HEADS * B_V_DIM
C_W = C_HEADS * C_K_DIM
MIX_W = A_Q_W + B_V_W + C_W
MAIN_W = A_Q_W + 2 * A_KV_W + A_QI_W + 2 * B_QK_W + B_V_W + 4 * C_W
NARROW_W = IDX_DIM + IDX_HEADS
SMALL_W = 128
PROJ_TN = 1024
ATTN_QUERY_TILE = 256
ATTN_KEY_CHUNK = 256
COUNT_ROWS = 64
DIFF_HEADS_PER_LOOP = 2
DSA_SEQS_PER_STEP = 4
DIFF_SEQS_PER_STEP = 2
HGRN_CHUNK = 256
COL_A_Q, COL_A_K, COL_A_V, COL_A_QI = 0, 512, 768, 1024
COL_B_Q, COL_B_K, COL_B_V = 1536, 2048, 2560
COL_C_Q, COL_C_F, COL_C_I, COL_C_G = 3072, 4096, 5120, 6144

V7X_LANES = 128
V7X_VMEM_LIMIT = 56 * 1024 * 1024


def _cparams(semantics):
    return pltpu.CompilerParams(dimension_semantics=semantics, vmem_limit_bytes=V7X_VMEM_LIMIT)


def _rmsnorm(x, g):
    return x * lax.rsqrt(jnp.mean(x * x, axis=-1, keepdims=True) + NORM_EPS) * g


def _dot(a, b):
    return jnp.dot(a, b, preferred_element_type=F32)


def _dot_nt(a, b):
    return lax.dot_general(a, b, (((1,), (1,)), ((), ())), preferred_element_type=F32)


def _dot_tn(a, b):
    return lax.dot_general(a, b, (((0,), (0,)), ((), ())), preferred_element_type=F32)


def _row_tile(n, pref):
    t = min(n, pref)
    while n % t:
        t //= 2
    return t


def _ffn_body(*refs, n_ff_steps, final_norm, emit):
    x_ref, g_ref, wg_ref, wu_ref, wo_ref = refs[:5]
    gf_ref = refs[5] if final_norm else None
    outs = refs[5 + final_norm:]
    o_ref, xn_ref = outs[0], outs[-1]
    j = pl.program_id(1)

    @pl.when(j == 0)
    def _():
        x = x_ref[...]
        xn_ref[...] = _rmsnorm(x, g_ref[...]).astype(BF16)
        o_ref[...] = x

    wg, wu, wo = wg_ref[...].astype(BF16), wu_ref[...].astype(BF16), wo_ref[...].astype(BF16)
    if emit:
        outs[1][...] = wg
        outs[2][...] = wu
        outs[3][...] = wo
    xn = xn_ref[...]
    gate = _dot(xn, wg)
    up = _dot(xn, wu)
    act = (gate * jax.nn.sigmoid(gate)) * up * 0.5
    o_ref[...] += _dot(act.astype(BF16), wo)

    if final_norm:
        @pl.when(j == n_ff_steps - 1)
        def _():
            o_ref[...] = _rmsnorm(o_ref[...], gf_ref[...])


def _ffn(x, g, w, g_final=None):
    n, d = x.shape
    emit = w[0] == "f32"
    if emit:
        _, w_in, w_out, layer = w
        f = w_out.shape[1]
    else:
        _, wg, wu, wo = w
        f = wo.shape[0]
    tm = _row_tile(n, 512)
    tf = 512 if f % 512 == 0 else f
    nj = f // tf
    assert not emit or n == tm, "the weight-emitting variant expects a single row tile"
    final_norm = g_final is not None
    in_specs = [pl.BlockSpec((tm, d), lambda i, j: (i, 0)), pl.BlockSpec((1, d), lambda i, j: (0, 0))]
    if emit:
        in_specs += [pl.BlockSpec((None, d, tf), lambda i, j: (layer, 0, j)),
                     pl.BlockSpec((None, d, tf), lambda i, j: (layer, 0, j + nj)),
                     pl.BlockSpec((None, tf, d), lambda i, j: (layer, j, 0))]
        args = [x, g.reshape(1, d), w_in, w_in, w_out]
    else:
        in_specs += [pl.BlockSpec((d, tf), lambda i, j: (0, j)), pl.BlockSpec((d, tf), lambda i, j: (0, j)),
                     pl.BlockSpec((tf, d), lambda i, j: (j, 0))]
        args = [x, g.reshape(1, d), wg, wu, wo]
    if final_norm:
        in_specs.append(pl.BlockSpec((1, d), lambda i, j: (0, 0)))
        args.append(g_final.reshape(1, d))
    out_specs = [pl.BlockSpec((tm, d), lambda i, j: (i, 0))]
    out_shape = [jax.ShapeDtypeStruct((n, d), F32)]
    if emit:
        out_specs += [pl.BlockSpec((d, tf), lambda i, j: (0, j)), pl.BlockSpec((d, tf), lambda i, j: (0, j)),
                      pl.BlockSpec((tf, d), lambda i, j: (j, 0))]
        out_shape += [jax.ShapeDtypeStruct((d, f), BF16), jax.ShapeDtypeStruct((d, f), BF16),
                      jax.ShapeDtypeStruct((f, d), BF16)]
    res = pl.pallas_call(
        functools.partial(_ffn_body, n_ff_steps=nj, final_norm=final_norm, emit=emit),
        grid=(n // tm, nj),
        in_specs=in_specs,
        out_specs=out_specs,
        out_shape=out_shape,
        scratch_shapes=[pltpu.VMEM((tm, d), BF16)],
        compiler_params=_cparams(("parallel", "arbitrary")),
        name="ffn_emit" if emit else "ffn",
    )(*args)
    return (res[0], ("bf16", res[1], res[2], res[3])) if emit else (res[0], w)


def _proj_body(*refs, emit, tn, tm, n_alias, layer):
    n_w = 3 if emit else 2
    x_ref, g_ref = refs[:2]
    w_refs = refs[2:2 + n_w]
    outs = refs[2 + n_w + n_alias:]
    om_ref, os_ref, ak_ref, av_ref, aki_ref, bk_ref, bv_ref = outs[:7]
    xn_ref = refs[-1]
    j = pl.program_id(1)

    def this_layer(ref):
        if n_alias:
            return ref
        for other in range(ref.shape[0]):
            if other != layer:
                ref[other] = jnp.zeros(ref.shape[1:], F32)
        return ref.at[layer]

    if emit:
        wa_ref, wx_ref, wsm_ref = w_refs
        wmo_ref, wso_ref = outs[7:9]
        n_aligned = COL_B_Q // tn

        @pl.when(j < n_aligned)
        def _():
            wmo_ref[...] = wa_ref[...].astype(BF16)

        @pl.when(j >= n_aligned)
        def _():
            tall = jnp.concatenate([wa_ref[...], wx_ref[...]], axis=0)
            wmo_ref[...] = tall[NARROW_W:NARROW_W + tn, :].astype(BF16)

        wm = wmo_ref[...]
    else:
        wm = w_refs[0][...]

    @pl.when(j == 0)
    def _():
        xn = _rmsnorm(x_ref[...], g_ref[...]).astype(BF16)
        xn_ref[...] = xn
        if emit:
            out_col = lax.broadcasted_iota(jnp.int32, wsm_ref.shape, 0)
            ws = jnp.where(out_col < NARROW_W, wsm_ref[...], 0.0).astype(BF16)
            wso_ref[...] = ws
        else:
            ws = w_refs[1][...]
        small = _dot_nt(xn, ws)
        os_ref[...] = small
        this_layer(aki_ref)[...] = small[:, 0:IDX_DIM]

    acc = _dot_nt(xn_ref[...], wm)
    om_ref[...] = acc

    def put_heads(ref, col, heads):
        off = col % tn
        assert off + heads * V7X_LANES <= tn

        @pl.when(j == col // tn)
        def _():
            dst = this_layer(ref)
            for hd in range(heads):
                dst[pl.ds(hd, tm, stride=heads), :] = acc[:, off + hd * V7X_LANES:off + (hd + 1) * V7X_LANES]

    put_heads(ak_ref, COL_A_K, A_KV_HEADS)
    put_heads(av_ref, COL_A_V, A_KV_HEADS)
    put_heads(bk_ref, COL_B_K, B_HEADS)
    put_heads(bv_ref, COL_B_V, B_HEADS)


def _proj(x, g, w, caches, layer, depth):
    n, d = x.shape
    emit = w[0] == "f32"
    tm = _row_tile(n, 512)
    tn = 512 if emit else PROJ_TN
    assert not emit or n == tm, "the weight-emitting variant expects a single row tile"
    lanes_per_tn = tn // V7X_LANES
    if emit:
        assert COL_B_Q % tn == 0 and NARROW_W % 8 == 0
        w_in_t = w[1]
        w_specs = [pl.BlockSpec((None, tn, d), lambda i, j: (layer, j, 0)),
                   pl.BlockSpec((None, V7X_LANES, d), lambda i, j: (layer, lanes_per_tn * (j + 1), 0)),
                   pl.BlockSpec((None, SMALL_W, d), lambda i, j: (layer, COL_B_Q // SMALL_W, 0))]
        w_args = [w_in_t, w_in_t, w_in_t]
    else:
        w_specs = [pl.BlockSpec((tn, d), lambda i, j: (j, 0)), pl.BlockSpec((SMALL_W, d), lambda i, j: (0, 0))]
        w_args = list(w[1:])
    cache_rows = (A_KV_HEADS, A_KV_HEADS, 1, B_HEADS, B_HEADS)
    cache_lanes = (V7X_LANES, V7X_LANES, IDX_DIM, V7X_LANES, V7X_LANES)
    alias_args = [] if caches is None else list(caches)
    out_specs = [pl.BlockSpec((tm, tn), lambda i, j: (i, j)), pl.BlockSpec((tm, SMALL_W), lambda i, j: (i, 0))]
    if alias_args:
        cache_spec = lambda r, wd: pl.BlockSpec((None, r * tm, wd), lambda i, j: (layer, i, 0))
    else:
        cache_spec = lambda r, wd: pl.BlockSpec((depth, r * tm, wd), lambda i, j: (0, i, 0))
    out_specs += [cache_spec(r, wd) for r, wd in zip(cache_rows, cache_lanes)]
    out_shape = [jax.ShapeDtypeStruct((n, MAIN_W), F32), jax.ShapeDtypeStruct((n, SMALL_W), F32)]
    out_shape += [jax.ShapeDtypeStruct((depth, r * n, wd), F32) for r, wd in zip(cache_rows, cache_lanes)]
    if emit:
        out_specs += [pl.BlockSpec((tn, d), lambda i, j: (j, 0)), pl.BlockSpec((SMALL_W, d), lambda i, j: (0, 0))]
        out_shape += [jax.ShapeDtypeStruct((MAIN_W, d), BF16), jax.ShapeDtypeStruct((SMALL_W, d), BF16)]
    n_in = 2 + len(w_args)
    res = pl.pallas_call(
        functools.partial(_proj_body, emit=emit, tn=tn, tm=tm, n_alias=len(alias_args), layer=layer),
        grid=(n // tm, MAIN_W // tn),
        in_specs=[pl.BlockSpec((tm, d), lambda i, j: (i, 0)), pl.BlockSpec((1, d), lambda i, j: (0, 0))]
        + w_specs + [pl.BlockSpec(memory_space=pl.ANY)] * len(alias_args),
        out_specs=out_specs,
        out_shape=out_shape,
        input_output_aliases={n_in + k: 2 + k for k in range(len(alias_args))},
        scratch_shapes=[pltpu.VMEM((tm, d), BF16)],
        compiler_params=_cparams(("parallel", "arbitrary")),
        name="in_proj_emit" if emit else "in_proj",
    )(x, g.reshape(1, d), *w_args, *alias_args)
    w_bf = ("bf16", res[7], res[8]) if emit else w
    return res[0], res[1], tuple(res[2:7]), w_bf


def _sortable_key(x):
    bits = lax.bitcast_convert_type(x, jnp.int32)
    return jnp.where(bits < 0, bits ^ jnp.int32(0x7FFFFFFF), bits)


def _kth_largest(key, k):
    rows = key.shape[0]

    def body(it, t):
        cand = t + lax.shift_left(jnp.int32(1), jnp.int32(31) - it)
        cnt = jnp.sum(jnp.where(key >= cand, 1.0, 0.0), axis=1, keepdims=True)
        return jnp.where(cnt >= k, cand, t)

    t0 = jnp.full((rows, 1), jnp.iinfo(jnp.int32).min, jnp.int32)
    return lax.fori_loop(0, 32, body, t0)


def _tie_cut(tie, kpos, need, nbits):
    rows = tie.shape[0]

    def body(it, t):
        cand = t + lax.shift_left(jnp.int32(1), jnp.int32(nbits - 1) - it)
        cnt = jnp.sum(jnp.where(tie & (kpos < cand), 1.0, 0.0), axis=1, keepdims=True)
        return jnp.where(cnt < need, cand, t)

    return lax.fori_loop(0, nbits, body, jnp.zeros((rows, 1), jnp.int32))


def _visible(tq, s_pad, q_pos0, n_keys):
    qpos = q_pos0 + pl.program_id(1) * tq + lax.broadcasted_iota(jnp.int32, (tq, 1), 0)
    limit = jnp.minimum(((qpos >> 6) + 1) << 6, n_keys)
    kpos = lax.broadcasted_iota(jnp.int32, (tq, s_pad), 1)
    return kpos < limit, kpos


def _head_keys(new_ref, rows, lo, width, past_ref, head, heads, s_pad):
    new = new_ref[rows, lo:lo + width]
    p_len = past_ref.shape[0] // heads
    hist = past_ref[...] if heads == 1 else past_ref[pl.ds(head, p_len, stride=heads), :]
    parts = [hist, new]
    pad = s_pad - p_len - new.shape[0]
    if pad:
        parts.append(jnp.zeros((pad, width), F32))
    return jnp.concatenate(parts, axis=0).astype(BF16)


def _stack_rows(pieces):
    return pieces[0] if len(pieces) == 1 else jnp.concatenate(pieces, axis=0)


def _dsa_hist_body(q_ref, qi_ref, wq_ref, k_ref, v_ref, ki_ref, pk_ref, pv_ref, pki_ref, o_ref,
                   *, nb, tq, s_pad, n_keys, q_pos0, topk):
    vis1, kpos1 = _visible(tq, s_pad, q_pos0, n_keys)
    rep = A_HEADS // A_KV_HEADS
    scores = []
    for bb in range(nb):
        rows = slice(bb * tq, (bb + 1) * tq)
        kidx = _head_keys(ki_ref, rows, 0, IDX_DIM, pki_ref.at[bb], 0, 1, s_pad)
        qi_all = _stack_rows([qi_ref[rows, h * IDX_DIM:(h + 1) * IDX_DIM] for h in range(IDX_HEADS)]).astype(BF16)
        wq = wq_ref[rows, IDX_DIM:IDX_DIM + IDX_HEADS] * (IDX_HEADS ** -0.5)
        w_col = _stack_rows([wq[:, h:h + 1] for h in range(IDX_HEADS)])
        x = jnp.maximum(_dot_nt(qi_all, kidx) * (IDX_DIM ** -0.5), 0.0) * w_col
        score = jnp.sum(x.reshape(IDX_HEADS, tq, s_pad), axis=0)
        scores.append(jnp.where(vis1, score + 0.0, NEG_BIG))

    key = _sortable_key(_stack_rows(scores))
    kpos = _stack_rows([kpos1] * nb)
    thr = _kth_largest(key, float(topk))
    n_gt = jnp.sum(jnp.where(key > thr, 1.0, 0.0), axis=1, keepdims=True)
    tie = key == thr
    cut = _tie_cut(tie, kpos, float(topk) - n_gt, max(1, (s_pad - 1).bit_length()))
    attend = jnp.where((key > thr) | (tie & (kpos <= cut)), 1.0, 0.0)

    for bb in range(nb):
        rows = slice(bb * tq, (bb + 1) * tq)
        sel = _stack_rows([jnp.where(vis1, attend[rows], 0.0)] * rep) > 0.5
        for g in range(A_KV_HEADS):
            k_g = _head_keys(k_ref, rows, g * A_HEAD_DIM, A_HEAD_DIM, pk_ref.at[bb], g, A_KV_HEADS, s_pad)
            v_g = _head_keys(v_ref, rows, g * A_HEAD_DIM, A_HEAD_DIM, pv_ref.at[bb], g, A_KV_HEADS, s_pad)
            heads = range(g * rep, (g + 1) * rep)
            q_g = _stack_rows([q_ref[rows, h * A_HEAD_DIM:(h + 1) * A_HEAD_DIM] for h in heads]).astype(BF16)
            att = jnp.where(sel, _dot_nt(q_g, k_g) * (A_HEAD_DIM ** -0.5), NEG_BIG)
            p = jnp.exp(att - jnp.max(att, axis=1, keepdims=True))
            l = jnp.sum(p, axis=1, keepdims=True)
            o = _dot(p.astype(BF16), v_g) / l
            for r, h in enumerate(heads):
                o_ref[rows, h * A_HEAD_DIM:(h + 1) * A_HEAD_DIM] = o[r * tq:(r + 1) * tq].astype(o_ref.dtype)


def _softmax_step_t(carry, s, v_c):
    m, l, acc = carry
    m_new = jnp.maximum(m, _col_reduce(s, jnp.max))
    alpha = jnp.exp(m - m_new)
    p = jnp.exp(s - m_new)
    return m_new, alpha * l + _col_reduce(p, jnp.sum), alpha * acc + _dot_tn(v_c, p.astype(BF16))


def _col_reduce(x, op):
    rows, tq = x.shape
    return op(op(x.reshape(rows // COUNT_ROWS, COUNT_ROWS, tq), axis=0), axis=0, keepdims=True)


def _softmax_init_t(tq, width):
    return (jnp.full((1, tq), -jnp.inf, F32), jnp.zeros((1, tq), F32), jnp.zeros((width, tq), F32))


def _causal_extent(tq, kc):
    i = pl.program_id(1)
    qpos = i * tq + lax.broadcasted_iota(jnp.int32, (1, tq), 1)
    limit = ((qpos >> 6) + 1) << 6
    return (i * tq + tq + kc - 1) // kc, limit, lax.broadcasted_iota(jnp.int32, (kc, 1), 0)


def _dsa_causal_body(q_ref, qi_ref, wq_ref, k_ref, v_ref, ki_ref, o_ref, key_ref, cut_ref, *, tq, kc, topk, nbits):
    n_chunks, limit, krow = _causal_extent(tq, kc)
    wq_t = wq_ref[...].T[IDX_DIM:IDX_DIM + IDX_HEADS, :] * (IDX_HEADS ** -0.5)
    qi_all = jnp.concatenate([qi_ref[:, h * IDX_DIM:(h + 1) * IDX_DIM] for h in range(IDX_HEADS)], axis=0).astype(BF16)

    def score_chunk(c, carry):
        start = pl.multiple_of(c * kc, kc)
        kidx = ki_ref[pl.ds(start, kc), 0:IDX_DIM].astype(BF16)
        logits = _dot_nt(kidx, qi_all)
        acc = jnp.zeros((kc, tq), F32)
        for h in range(IDX_HEADS):
            acc = acc + jnp.maximum(logits[:, h * tq:(h + 1) * tq] * (IDX_DIM ** -0.5), 0.0) * wq_t[h:h + 1, :]
        score = jnp.where(start + krow < limit, acc + 0.0, NEG_BIG)
        key_ref[c] = _sortable_key(score)
        return carry

    lax.fori_loop(0, n_chunks, score_chunk, 0)

    def count(pred):
        def body(c, part):
            ones = jnp.where(pred(key_ref[c], c), 1.0, 0.0)
            return part + jnp.sum(ones.reshape(kc // COUNT_ROWS, COUNT_ROWS, tq), axis=0)

        part = lax.fori_loop(0, n_chunks, body, jnp.zeros((COUNT_ROWS, tq), F32))
        return jnp.sum(part, axis=0, keepdims=True)

    def thr_step(it, t):
        cand = t + lax.shift_left(jnp.int32(1), jnp.int32(31) - it)
        return jnp.where(count(lambda key, c: key >= cand) >= float(topk), cand, t)

    thr = lax.fori_loop(0, 32, thr_step, jnp.full((1, tq), jnp.iinfo(jnp.int32).min, jnp.int32))
    need = float(topk) - count(lambda key, c: key > thr)
    n_tie = count(lambda key, c: key == thr)

    cut_ref[...] = jnp.full(cut_ref.shape, jnp.iinfo(jnp.int32).max, jnp.int32)

    @pl.when(jnp.max(jnp.where(n_tie != need, 1.0, 0.0)) > 0.0)
    def _():
        def cut_step(it, t):
            cand = t + lax.shift_left(jnp.int32(1), jnp.int32(nbits - 1) - it)
            below = count(lambda key, c: (key == thr) & (c * kc + krow < cand))
            return jnp.where(below < need, cand, t)

        cut = lax.fori_loop(0, nbits, cut_step, jnp.zeros((1, tq), jnp.int32))
        cut_ref[...] = jnp.broadcast_to(cut, cut_ref.shape)

    cut = cut_ref[0:1, :]

    def select_chunk(c, carry):
        key = key_ref[c]
        kpos = c * kc + krow
        tie_flag = jnp.where(key == thr, jnp.where(kpos <= cut, 0, 1), 1)
        key_ref[c] = jnp.where(kpos < limit, jnp.where(key > thr, 0, tie_flag), 1)
        return carry

    lax.fori_loop(0, n_chunks, select_chunk, 0)

    rep = A_HEADS // A_KV_HEADS
    q_g = [jnp.concatenate([q_ref[:, h * A_HEAD_DIM:(h + 1) * A_HEAD_DIM] for h in range(g * rep, (g + 1) * rep)],
                           axis=0).astype(BF16) for g in range(A_KV_HEADS)]

    def att_chunk(c, carry):
        start = pl.multiple_of(c * kc, kc)
        attend = key_ref[c] == 0
        attend = jnp.concatenate([attend] * rep, axis=1)
        logits = [_dot_nt(k_ref[pl.ds(start, kc), g * A_HEAD_DIM:(g + 1) * A_HEAD_DIM].astype(BF16), q_g[g])
                  for g in range(A_KV_HEADS)]
        v_c = [v_ref[pl.ds(start, kc), g * A_HEAD_DIM:(g + 1) * A_HEAD_DIM].astype(BF16) for g in range(A_KV_HEADS)]
        return tuple(
            _softmax_step_t(carry[g], jnp.where(attend, logits[g] * (A_HEAD_DIM ** -0.5), NEG_BIG), v_c[g])
            for g in range(A_KV_HEADS))

    res = lax.fori_loop(0, n_chunks, att_chunk, tuple(_softmax_init_t(rep * tq, A_HEAD_DIM) for _ in range(A_KV_HEADS)))
    for g in range(A_KV_HEADS):
        _, l, acc = res[g]
        o_t = acc / l
        for r in range(rep):
            h = g * rep + r
            o_ref[:, h * A_HEAD_DIM:(h + 1) * A_HEAD_DIM] = o_t[:, r * tq:(r + 1) * tq].T.astype(o_ref.dtype)


def _diff_causal_body(q_ref, k_ref, v_ref, lam_ref, g_ref, o_ref, *, tq, kc, lam_init):
    n_chunks, limit, krow = _causal_extent(tq, kc)
    lv = lam_ref[...]
    lam = (jnp.exp(jnp.sum(lv[0:1] * lv[1:2], axis=1, keepdims=True))
           - jnp.exp(jnp.sum(lv[2:3] * lv[3:4], axis=1, keepdims=True)) + lam_init)
    gain = g_ref[...] * (1.0 - lam_init)
    head_w = 2 * B_QK_DIM
    qlane = lax.broadcasted_iota(jnp.int32, (tq, head_w), 1)
    for h0 in range(0, B_HEADS, DIFF_HEADS_PER_LOOP):
        heads = range(h0, h0 + DIFF_HEADS_PER_LOOP)
        q_2 = [jnp.concatenate([jnp.where((qlane >= c * B_QK_DIM) & (qlane < (c + 1) * B_QK_DIM),
                                          q_ref[:, h * head_w:(h + 1) * head_w], 0.0) for c in range(2)],
                               axis=0).astype(BF16) for h in heads]

        def chunk(c, carry, q_2=q_2, heads=heads):
            start = pl.multiple_of(c * kc, kc)
            vis = start + krow < jnp.concatenate([limit, limit], axis=1)
            logits = [_dot_nt(k_ref[pl.ds(start, kc), h * head_w:(h + 1) * head_w].astype(BF16), q_2[j])
                      for j, h in enumerate(heads)]
            v_c = [v_ref[pl.ds(start, kc), h * B_V_DIM:(h + 1) * B_V_DIM].astype(BF16) for h in heads]
            return tuple(
                _softmax_step_t(carry[j], jnp.where(vis, logits[j] * (B_QK_DIM ** -0.5), NEG_BIG), v_c[j])
                for j in range(len(heads)))

        res = lax.fori_loop(0, n_chunks, chunk, tuple(_softmax_init_t(2 * tq, B_V_DIM) for _ in heads))
        for j, h in enumerate(heads):
            _, l, acc = res[j]
            o_t = acc / l
            o = (o_t[:, 0:tq] - lam * o_t[:, tq:2 * tq]).T
            o = o * lax.rsqrt(jnp.mean(o * o, axis=1, keepdims=True) + NORM_EPS) * gain
            o_ref[:, h * B_V_DIM:(h + 1) * B_V_DIM] = o.astype(o_ref.dtype)


def _blk(arr_col, rows, width, kind, nq):
    _, col = arr_col
    cb = col // width
    assert col % width == 0
    if kind == "q":
        return pl.BlockSpec((rows, width), lambda b, i: (b * nq + i, cb))
    return pl.BlockSpec((rows, width), lambda b, i: (b, cb))


def _past_blk(cache, layer, nb):
    return pl.BlockSpec((None, nb) + cache.shape[2:], lambda b, i: (layer, b, 0, 0))


def _seqs_per_step(bsz, limit):
    nb = min(bsz, limit)
    while bsz % nb:
        nb -= 1
    return nb


def _dsa(q, qi, wq, k, v, ki, past, *, bsz, t, s_pad, n_keys, q_pos0, tq, layer):
    nq = t // tq
    topk = min(TOPK_MAX, n_keys // 4)
    scratch = []
    if past is None:
        nb = 1
        kc = _row_tile(t, ATTN_KEY_CHUNK)
        body = functools.partial(_dsa_causal_body, tq=tq, kc=kc, topk=topk, nbits=max(1, (t - 1).bit_length()))
        scratch = [pltpu.VMEM((t // kc, kc, tq), jnp.int32), pltpu.VMEM((8, tq), jnp.int32)]
    else:
        assert nq == 1
        nb = _seqs_per_step(bsz, DSA_SEQS_PER_STEP)
        body = functools.partial(_dsa_hist_body, nb=nb, tq=tq, s_pad=s_pad, n_keys=n_keys, q_pos0=q_pos0, topk=topk)
    in_specs = [
        _blk(q, nb * tq, A_Q_W, "q", nq),
        _blk(qi, nb * tq, A_QI_W, "q", nq),
        _blk(wq, nb * tq, SMALL_W, "q", nq),
        _blk(k, nb * t, A_KV_W, "kv", nq),
        _blk(v, nb * t, A_KV_W, "kv", nq),
        _blk(ki, nb * t, SMALL_W, "kv", nq),
    ]
    args = [q[0], qi[0], wq[0], k[0], v[0], ki[0]]
    if past is not None:
        in_specs += [_past_blk(c, layer, nb) for c in past]
        args += list(past)
    return pl.pallas_call(
        body,
        grid=(bsz // nb, nq),
        in_specs=in_specs,
        out_specs=pl.BlockSpec((nb * tq, A_Q_W), lambda b, i: (b * nq + i, 0)),
        out_shape=jax.ShapeDtypeStruct((bsz * t, A_Q_W), BF16),
        scratch_shapes=scratch,
        compiler_params=_cparams(("parallel", "parallel")),
        name="dsa",
    )(*args)


def _diff_hist_body(q_ref, k_ref, v_ref, lam_ref, g_ref, pk_ref, pv_ref, o_ref,
                    *, nb, tq, s_pad, n_keys, q_pos0, lam_init):
    vis1, _ = _visible(tq, s_pad, q_pos0, n_keys)
    vis = _stack_rows([jnp.where(vis1, 1.0, 0.0)] * 2) > 0.5
    lv = lam_ref[...]
    lam = (jnp.exp(jnp.sum(lv[0:1] * lv[1:2], axis=1, keepdims=True))
           - jnp.exp(jnp.sum(lv[2:3] * lv[3:4], axis=1, keepdims=True)) + lam_init)
    gain = g_ref[...] * (1.0 - lam_init)
    head_w = 2 * B_QK_DIM
    lane = lax.broadcasted_iota(jnp.int32, (tq, head_w), 1)
    for bb in range(nb):
        rows = slice(bb * tq, (bb + 1) * tq)
        for h in range(B_HEADS):
            k_h = _head_keys(k_ref, rows, h * head_w, head_w, pk_ref.at[bb], h, B_HEADS, s_pad)
            v_h = _head_keys(v_ref, rows, h * B_V_DIM, B_V_DIM, pv_ref.at[bb], h, B_HEADS, s_pad)
            q_h = q_ref[rows, h * head_w:(h + 1) * head_w]
            q_2 = _stack_rows([jnp.where((lane >= c * B_QK_DIM) & (lane < (c + 1) * B_QK_DIM), q_h, 0.0)
                               for c in range(2)]).astype(BF16)
            logit = jnp.where(vis, _dot_nt(q_2, k_h) * (B_QK_DIM ** -0.5), NEG_BIG)
            p = jnp.exp(logit - jnp.max(logit, axis=1, keepdims=True))
            l = jnp.sum(p, axis=1, keepdims=True)
            o_2 = _dot(p.astype(BF16), v_h) / l
            o = o_2[0:tq] - lam * o_2[tq:2 * tq]
            o = o * lax.rsqrt(jnp.mean(o * o, axis=1, keepdims=True) + NORM_EPS) * gain
            o_ref[rows, h * B_V_DIM:(h + 1) * B_V_DIM] = o.astype(o_ref.dtype)


def _diff(q, k, v, lam_vecs, g_diff, past, *, bsz, t, s_pad, n_keys, q_pos0, tq, layer):
    nq = t // tq
    lam_init = 0.8 - 0.6 * math.exp(-0.3 * layer)
    if past is None:
        nb = 1
        body = functools.partial(_diff_causal_body, tq=tq, kc=_row_tile(t, ATTN_KEY_CHUNK), lam_init=lam_init)
    else:
        assert nq == 1
        nb = _seqs_per_step(bsz, DIFF_SEQS_PER_STEP)
        body = functools.partial(_diff_hist_body, nb=nb, tq=tq, s_pad=s_pad, n_keys=n_keys, q_pos0=q_pos0,
                                 lam_init=lam_init)
    in_specs = [
        _blk(q, nb * tq, B_QK_W, "q", nq),
        _blk(k, nb * t, B_QK_W, "kv", nq),
        _blk(v, nb * t, B_V_W, "kv", nq),
        pl.BlockSpec((None, 4, B_QK_DIM), lambda b, i: (layer, 0, 0)),
        pl.BlockSpec((None, 1, B_V_DIM), lambda b, i: (layer, 0, 0)),
    ]
    args = [q[0], k[0], v[0], lam_vecs, g_diff]
    if past is not None:
        in_specs += [_past_blk(c, layer, nb) for c in past]
        args += list(past)
    return pl.pallas_call(
        body,
        grid=(bsz // nb, nq),
        in_specs=in_specs,
        out_specs=pl.BlockSpec((nb * tq, B_V_W), lambda b, i: (b * nq + i, 0)),
        out_shape=jax.ShapeDtypeStruct((bsz * t, B_V_W), BF16),
        compiler_params=_cparams(("parallel", "parallel")),
        name="diff_attn",
    )(*args)


def _cumsum_rows(x):
    n = x.shape[0]
    row = lax.broadcasted_iota(jnp.int32, (n, 1), 0)
    d = 1
    while d < n:
        x = x + jnp.where(row >= d, pltpu.roll(x, d, axis=0), 0.0)
        d *= 2
    return x


def _anchor_rows(b, c):
    n, w = b.shape
    if 2 * c >= 8:
        pieces = [jnp.broadcast_to(b[j * 2 * c + c - 1:j * 2 * c + c, :], (2 * c, w)) for j in range(n // (2 * c))]
        return pieces[0] if len(pieces) == 1 else jnp.concatenate(pieces, axis=0)
    rmod = lax.broadcasted_iota(jnp.int32, (n, 1), 0) & (2 * c - 1)
    m = b
    for s in range(-c, c):
        if s != 0:
            m = jnp.where(rmod == c - 1 - s, pltpu.roll(b, (-s) % n, axis=0), m)
    return m


def _hgrn_body(*refs, chunk, layer, has_s0):
    if has_s0:
        q_ref, f_ref, i_ref, g_ref, glb_ref, gn_ref, s0_ref, o_ref, s_ref, st_ref = refs
    else:
        q_ref, f_ref, i_ref, g_ref, glb_ref, gn_ref, o_ref, s_ref, st_ref = refs
    ci = pl.program_id(1)

    @pl.when(ci == 0)
    def _():
        for h in range(C_HEADS):
            st_ref[h] = s0_ref[h].T if has_s0 else jnp.zeros((C_V_DIM, C_K_DIM), F32)

    glb = glb_ref[...]
    pe = jnp.exp(glb - jnp.max(glb, axis=0, keepdims=True))
    pn = pe / jnp.sum(pe, axis=0, keepdims=True)
    lb = jnp.sum(pn[0:layer + 1], axis=0, keepdims=True) - pn[0:1]

    zf = f_ref[...]
    log_sig = jnp.minimum(zf, 0.0) - jnp.log1p(jnp.exp(-jnp.abs(zf)))
    la = jnp.log(jnp.maximum(lb, LB_FLOOR))
    lc = jnp.log1p(-lb) + log_sig
    hi = jnp.maximum(la, lc)
    log_f = hi + jnp.log1p(jnp.exp(jnp.minimum(la, lc) - hi))
    kc = (1.0 - lb) * jax.nn.sigmoid(-zf)
    b_all = _cumsum_rows(log_f)

    row = lax.broadcasted_iota(jnp.int32, (chunk, 1), 0)
    rr = lax.broadcasted_iota(jnp.int32, (chunk, chunk), 0)
    cc = lax.broadcasted_iota(jnp.int32, (chunk, chunk), 1)
    heads = range(C_HEADS)
    head = lambda x, h: x[:, h * C_K_DIM:(h + 1) * C_K_DIM]

    q_all = q_ref[...]
    v_all = i_ref[...]
    vb_all = v_all.astype(BF16)
    b_last = b_all[chunk - 1:chunk, :]
    qe_all = (q_all * jnp.exp(b_all)).astype(BF16)
    kdec_all = (kc * jnp.exp(b_last - b_all)).astype(BF16)
    diag_all = q_all * kc
    levels = []
    c = chunk // 2
    while c >= 1:
        lg = c.bit_length() - 1
        upper = ((row >> lg) & 1) == 1
        d = b_all - _anchor_rows(b_all, c)
        e = jnp.exp(jnp.where(upper, d, -d))
        x = (jnp.where(upper, q_all, kc) * e).astype(BF16)
        pairs = ((rr >> (lg + 1)) == (cc >> (lg + 1))) & (((rr >> lg) & 1) == 1) & (((cc >> lg) & 1) == 0)
        levels.append((x, pairs))
        c //= 2

    st = [st_ref[h] for h in heads]
    o_inter = [_dot_nt(head(qe_all, h), st[h].astype(BF16)) for h in heads]
    a = []
    for h in heads:
        a_h = jnp.zeros((chunk, chunk), F32)
        for x, pairs in levels:
            a_h = jnp.where(pairs, _dot_nt(head(x, h), head(x, h)), a_h)
        a.append(a_h)
    for h in heads:
        st_new = st[h] * jnp.exp(head(b_last, h)) + _dot_tn(head(vb_all, h), head(kdec_all, h))
        st_ref[h] = st_new

        @pl.when(ci == pl.num_programs(1) - 1)
        def _():
            s_ref[0, h] = st_new.T

    gn = gn_ref[...]
    for h in heads:
        o = o_inter[h] + jnp.sum(head(diag_all, h), axis=1, keepdims=True) * head(v_all, h)
        o = o + _dot(a[h].astype(BF16), head(vb_all, h))
        o = o * lax.rsqrt(jnp.mean(o * o, axis=1, keepdims=True) + NORM_EPS) * gn
        zg = head(g_ref, h)
        o_ref[:, h * C_V_DIM:(h + 1) * C_V_DIM] = (o * (zg * jax.nn.sigmoid(zg))).astype(o_ref.dtype)


def _hgrn(cq, cf, cin, cg, gamma_lb, g_hgrn, s0, *, bsz, t, layer):
    chunk = _row_tile(t, HGRN_CHUNK)
    nc = t // chunk
    depth = gamma_lb.shape[0]
    has_s0 = s0 is not None

    def seq_blk(arr_col):
        cb = arr_col[1] // C_W
        return pl.BlockSpec((chunk, C_W), lambda b, c: (b * nc + c, cb))

    in_specs = [seq_blk(cq), seq_blk(cf), seq_blk(cin), seq_blk(cg),
                pl.BlockSpec((depth, C_W), lambda b, c: (0, 0)),
                pl.BlockSpec((None, 1, C_V_DIM), lambda b, c: (layer, 0, 0))]
    args = [cq[0], cf[0], cin[0], cg[0], gamma_lb, g_hgrn]
    if has_s0:
        in_specs.append(pl.BlockSpec((None, None, C_HEADS, C_K_DIM, C_V_DIM), lambda b, c: (layer, b, 0, 0, 0)))
        args.append(s0)
    return pl.pallas_call(
        functools.partial(_hgrn_body, chunk=chunk, layer=layer, has_s0=has_s0),
        grid=(bsz, nc),
        in_specs=in_specs,
        out_specs=[pl.BlockSpec((chunk, C_W), lambda b, c: (b * nc + c, 0)),
                   pl.BlockSpec((1, C_HEADS, C_K_DIM, C_V_DIM), lambda b, c: (b, 0, 0, 0))],
        out_shape=[jax.ShapeDtypeStruct((bsz * t, C_W), BF16),
                   jax.ShapeDtypeStruct((bsz, C_HEADS, C_K_DIM, C_V_DIM), F32)],
        scratch_shapes=[pltpu.VMEM((C_HEADS, C_V_DIM, C_K_DIM), F32)],
        compiler_params=_cparams(("parallel", "arbitrary")),
        name="hgrn2",
    )(*args)


def _outproj_body(*refs, emit):
    h_ref, oa_ref, ob_ref, oc_ref, wa_ref, wb_ref, wc_ref, o_ref = refs[:8]
    wa, wb, wc = wa_ref[...].astype(BF16), wb_ref[...].astype(BF16), wc_ref[...].astype(BF16)
    if emit:
        refs[8][...] = wa
        refs[9][...] = wb
        refs[10][...] = wc
    o_ref[...] = h_ref[...] + (_dot(oa_ref[...], wa) + _dot(ob_ref[...], wb) + _dot(oc_ref[...], wc))


def _outproj(h, oa, ob, oc, w):
    n, d = h.shape
    emit = w[0] == "f32"
    tm = _row_tile(n, 512)
    assert not emit or n == tm, "the weight-emitting variant expects a single row tile"
    row = lambda wd: pl.BlockSpec((tm, wd), lambda i: (i, 0))
    whole = lambda r: pl.BlockSpec((r, d), lambda i: (0, 0))
    seg_rows = (A_Q_W, B_V_W, C_W)
    seg_first = (0, A_Q_W, A_Q_W + B_V_W)
    if emit:
        _, w_out, layer = w
        w_specs = [pl.BlockSpec((None, r, d), functools.partial(lambda i, blk: (layer, blk, 0), blk=f // r))
                   for r, f in zip(seg_rows, seg_first)]
        w_args = [w_out] * 3
    else:
        w_specs = [whole(r) for r in seg_rows]
        w_args = list(w[1:])
    out_specs = [row(d)]
    out_shape = [jax.ShapeDtypeStruct((n, d), F32)]
    if emit:
        out_specs += [whole(r) for r in seg_rows]
        out_shape += [jax.ShapeDtypeStruct((r, d), BF16) for r in seg_rows]
    res = pl.pallas_call(
        functools.partial(_outproj_body, emit=emit),
        grid=(n // tm,),
        in_specs=[row(d), row(A_Q_W), row(B_V_W), row(C_W)] + w_specs,
        out_specs=out_specs,
        out_shape=out_shape,
        compiler_params=_cparams(("parallel",)),
        name="out_proj_emit" if emit else "out_proj",
    )(h, oa, ob, oc, *w_args)
    return (res[0], ("bf16", res[1], res[2], res[3])) if emit else (res[0], w)


def _layer(x, layer, w, p, caches, past, geom):
    bsz, t = geom["bsz"], geom["t"]
    depth = p["g_ffn1"].shape[0]
    wb = {}
    h, wb["ffn1"] = _ffn(x, p["g_ffn1"][layer], w["ffn1"])
    main, small, caches, wb["proj"] = _proj(h, p["g_mix"][layer], w["proj"], caches, layer, depth)
    if past is None:
        past_a = past_b = s0 = None
    else:
        past_a, past_b, s0 = past[0:3], past[3:5], past[5]
    o_a = _dsa((main, COL_A_Q), (main, COL_A_QI), (small, 0), (main, COL_A_K), (main, COL_A_V), (small, 0), past_a,
               layer=layer, **geom)
    o_b = _diff((main, COL_B_Q), (main, COL_B_K), (main, COL_B_V), p["diff_lambda"], p["g_diff"], past_b,
                layer=layer, **geom)
    o_c, s_c = _hgrn((main, COL_C_Q), (main, COL_C_F), (main, COL_C_I), (main, COL_C_G),
                     p["gamma_lb"], p["g_hgrn"], s0, bsz=bsz, t=t, layer=layer)
    h, wb["out"] = _outproj(h, o_a, o_b, o_c, w["out"])
    x, wb["ffn2"] = _ffn(h, p["g_ffn2"][layer], w["ffn2"], p["g_final"] if layer == depth - 1 else None)
    return x, caches, s_c, wb


def kernel(x_prompt, x_sample, cache_a_k, cache_a_v, cache_a_kidx, cache_b_k, cache_b_v, state_c,
           g_ffn1, w_ffn1_in, w_ffn1_out, g_mix, w_in, diff_lambda, g_diff, gamma_lb, g_hgrn, w_out,
           g_ffn2, w_ffn2_in, w_ffn2_out, g_final):
    depth, d = w_in.shape[0], w_in.shape[1]
    bp, tp, _ = x_prompt.shape
    bs, ts, _ = x_sample.shape
    p_len = cache_a_k.shape[2]
    p = dict(g_ffn1=g_ffn1, g_mix=g_mix, diff_lambda=diff_lambda, g_diff=g_diff.reshape(depth, 1, B_V_DIM),
             gamma_lb=gamma_lb, g_hgrn=g_hgrn.reshape(depth, 1, C_V_DIM), g_ffn2=g_ffn2, g_final=g_final)
    geom_p = dict(bsz=bp, t=tp, s_pad=tp, n_keys=tp, q_pos0=0, tq=_row_tile(tp, ATTN_QUERY_TILE))
    n_keys_s = p_len + ts
    geom_s = dict(bsz=bs, t=ts, s_pad=-(-n_keys_s // V7X_LANES) * V7X_LANES, n_keys=n_keys_s, q_pos0=p_len, tq=ts)
    past = (cache_a_k.reshape(depth, bs, p_len * A_KV_HEADS, A_HEAD_DIM),
            cache_a_v.reshape(depth, bs, p_len * A_KV_HEADS, A_HEAD_DIM),
            cache_a_kidx,
            cache_b_k.reshape(depth, bs, p_len * B_HEADS, 2 * B_QK_DIM),
            cache_b_v.reshape(depth, bs, p_len * B_HEADS, B_V_DIM),
            state_c)

    w_in_t = jnp.swapaxes(w_in, 1, 2)
    xp = x_prompt.reshape(bp * tp, d)
    xs = x_sample.reshape(bs * ts, d)
    caches_p = caches_s = None
    states_p, states_s = [], []
    for layer in range(depth):
        w_f32 = dict(ffn1=("f32", w_ffn1_in, w_ffn1_out, layer), proj=("f32", w_in_t), out=("f32", w_out, layer),
                     ffn2=("f32", w_ffn2_in, w_ffn2_out, layer))
        xs, caches_s, s_c, w_bf = _layer(xs, layer, w_f32, p, caches_s, past, geom_s)
        states_s.append(s_c)
        xp, caches_p, s_c, _ = _layer(xp, layer, w_bf, p, caches_p, None, geom_p)
        states_p.append(s_c)

    def group_out(x, caches, states, bsz, t):
        a_k, a_v, a_ki, b_k, b_v = caches
        return (x.reshape(bsz, t, d),
                a_k.reshape(depth, bsz, t, A_KV_HEADS, A_HEAD_DIM), a_v.reshape(depth, bsz, t, A_KV_HEADS, A_HEAD_DIM),
                a_ki.reshape(depth, bsz, t, IDX_DIM), b_k.reshape(depth, bsz, t, B_HEADS, 2 * B_QK_DIM),
                b_v.reshape(depth, bsz, t, B_HEADS, B_V_DIM), jnp.stack(states))

    out_p = group_out(xp, caches_p, states_p, bp, tp)
    out_s = group_out(xs, caches_s, states_s, bs, ts)
    return (out_p[0], out_s[0], *out_p[1:], *out_s[1:])
```

```python
import functools
import math

import jax
import jax.numpy as jnp
from jax import lax
from jax.experimental import pallas as pl
from jax.experimental.pallas import tpu as pltpu

F32 = jnp.float32
BF16 = jnp.bfloat16

CHUNK = 64
A_HEADS, A_KV_HEADS, A_HEAD_DIM = 4, 2, 128
IDX_HEADS, IDX_DIM = 8, 64
TOPK_MAX = 256
B_HEADS, B_QK_DIM, B_V_DIM = 4, 64, 128
C_HEADS, C_K_DIM, C_V_DIM = 8, 128, 128
NORM_EPS = 1e-6
NEG_BIG = -1e30
LB_FLOOR = 1e-30

A_Q_W = A_HEADS * A_HEAD_DIM
A_KV_W = A_KV_HEADS * A_HEAD_DIM
A_QI_W = IDX_HEADS * IDX_DIM
B_QK_W = B_HEADS * 2 * B_QK_DIM
B_V_W = B_HEADS * B_V_DIM
C_W = C_HEADS * C_K_DIM
MIX_W = A_Q_W + B_V_W + C_W
MAIN_W = A_Q_W + 2 * A_KV_W + A_QI_W + 2 * B_QK_W + B_V_W + 4 * C_W
NARROW_W = IDX_DIM + IDX_HEADS
SMALL_W = 128
PROJ_TN = 1024
ATTN_QUERY_TILE = 256
ATTN_KEY_CHUNK = 256
COUNT_ROWS = 64
DIFF_HEADS_PER_LOOP = 4
DSA_SEQS_PER_STEP = 4
DIFF_SEQS_PER_STEP = 2
HGRN_CHUNK = 256
COL_A_Q, COL_A_K, COL_A_V, COL_A_QI = 0, 512, 768, 1024
COL_B_Q, COL_B_K, COL_B_V = 1536, 2048, 2560
COL_C_Q, COL_C_F, COL_C_I, COL_C_G = 3072, 4096, 5120, 6144

V7X_LANES = 128
V7X_VMEM_LIMIT = 56 * 1024 * 1024


def _cparams(semantics):
    return pltpu.CompilerParams(dimension_semantics=semantics, vmem_limit_bytes=V7X_VMEM_LIMIT)


def _rmsnorm(x, g):
    return x * lax.rsqrt(jnp.mean(x * x, axis=-1, keepdims=True) + NORM_EPS) * g


def _dot(a, b):
    return jnp.dot(a, b, preferred_element_type=F32)


def _dot_nt(a, b):
    return lax.dot_general(a, b, (((1,), (1,)), ((), ())), preferred_element_type=F32)


def _dot_tn(a, b):
    return lax.dot_general(a, b, (((0,), (0,)), ((), ())), preferred_element_type=F32)


def _row_tile(n, pref):
    t = min(n, pref)
    while n % t:
        t //= 2
    return t


def _ffn_body(*refs, n_ff_steps, final_norm, emit):
    x_ref, g_ref, wg_ref, wu_ref, wo_ref = refs[:5]
    gf_ref = refs[5] if final_norm else None
    outs = refs[5 + final_norm:]
    o_ref, xn_ref = outs[0], outs[-1]
    j = pl.program_id(1)

    @pl.when(j == 0)
    def _():
        x = x_ref[...]
        xn_ref[...] = _rmsnorm(x, g_ref[...]).astype(BF16)
        o_ref[...] = x

    wg, wu, wo = wg_ref[...].astype(BF16), wu_ref[...].astype(BF16), wo_ref[...].astype(BF16)
    if emit:
        outs[1][...] = wg
        outs[2][...] = wu
        outs[3][...] = wo
    xn = xn_ref[...]
    gate = _dot(xn, wg)
    up = _dot(xn, wu)
    act = (gate * jax.nn.sigmoid(gate)) * up * 0.5
    o_ref[...] += _dot(act.astype(BF16), wo)

    if final_norm:
        @pl.when(j == n_ff_steps - 1)
        def _():
            o_ref[...] = _rmsnorm(o_ref[...], gf_ref[...])


def _ffn(x, g, w, g_final=None):
    n, d = x.shape
    emit = w[0] == "f32"
    if emit:
        _, w_in, w_out, layer = w
        f = w_out.shape[1]
    else:
        _, wg, wu, wo = w
        f = wo.shape[0]
    tm = _row_tile(n, 512)
    tf = 512 if f % 512 == 0 else f
    nj = f // tf
    assert not emit or n == tm, "the weight-emitting variant expects a single row tile"
    final_norm = g_final is not None
    in_specs = [pl.BlockSpec((tm, d), lambda i, j: (i, 0)), pl.BlockSpec((1, d), lambda i, j: (0, 0))]
    if emit:
        in_specs += [pl.BlockSpec((None, d, tf), lambda i, j: (layer, 0, j)),
                     pl.BlockSpec((None, d, tf), lambda i, j: (layer, 0, j + nj)),
                     pl.BlockSpec((None, tf, d), lambda i, j: (layer, j, 0))]
        args = [x, g.reshape(1, d), w_in, w_in, w_out]
    else:
        in_specs += [pl.BlockSpec((d, tf), lambda i, j: (0, j)), pl.BlockSpec((d, tf), lambda i, j: (0, j)),
                     pl.BlockSpec((tf, d), lambda i, j: (j, 0))]
        args = [x, g.reshape(1, d), wg, wu, wo]
    if final_norm:
        in_specs.append(pl.BlockSpec((1, d), lambda i, j: (0, 0)))
        args.append(g_final.reshape(1, d))
    out_specs = [pl.BlockSpec((tm, d), lambda i, j: (i, 0))]
    out_shape = [jax.ShapeDtypeStruct((n, d), F32)]
    if emit:
        out_specs += [pl.BlockSpec((d, tf), lambda i, j: (0, j)), pl.BlockSpec((d, tf), lambda i, j: (0, j)),
                      pl.BlockSpec((tf, d), lambda i, j: (j, 0))]
        out_shape += [jax.ShapeDtypeStruct((d, f), BF16), jax.ShapeDtypeStruct((d, f), BF16),
                      jax.ShapeDtypeStruct((f, d), BF16)]
    res = pl.pallas_call(
        functools.partial(_ffn_body, n_ff_steps=nj, final_norm=final_norm, emit=emit),
        grid=(n // tm, nj),
        in_specs=in_specs,
        out_specs=out_specs,
        out_shape=out_shape,
        scratch_shapes=[pltpu.VMEM((tm, d), BF16)],
        compiler_params=_cparams(("parallel", "arbitrary")),
        name="ffn_emit" if emit else "ffn",
    )(*args)
    return (res[0], ("bf16", res[1], res[2], res[3])) if emit else (res[0], w)


def _proj_body(*refs, emit, tn, tm, n_alias, layer):
    n_w = 3 if emit else 2
    x_ref, g_ref = refs[:2]
    w_refs = refs[2:2 + n_w]
    outs = refs[2 + n_w + n_alias:-1]
    om_ref, os_ref, ak_ref, av_ref, aki_ref, bk_ref, bv_ref = outs[:7]
    cf_ref = outs[-1]
    xn_ref = refs[-1]
    j = pl.program_id(1)

    def this_layer(ref):
        if n_alias:
            return ref
        for other in range(ref.shape[0]):
            if other != layer:
                ref[other] = jnp.zeros(ref.shape[1:], F32)
        return ref.at[layer]

    if emit:
        wa_ref, wx_ref, wsm_ref = w_refs
        wmo_ref, wso_ref = outs[7:9]
        n_aligned = COL_B_Q // tn

        @pl.when(j < n_aligned)
        def _():
            wmo_ref[...] = wa_ref[...].astype(BF16)

        @pl.when(j >= n_aligned)
        def _():
            tall = jnp.concatenate([wa_ref[...], wx_ref[...]], axis=0)
            wmo_ref[...] = tall[NARROW_W:NARROW_W + tn, :].astype(BF16)

        wm = wmo_ref[...]
    else:
        wm = w_refs[0][...]

    @pl.when(j == 0)
    def _():
        xn = _rmsnorm(x_ref[...], g_ref[...]).astype(BF16)
        xn_ref[...] = xn
        if emit:
            out_col = lax.broadcasted_iota(jnp.int32, wsm_ref.shape, 0)
            ws = jnp.where(out_col < NARROW_W, wsm_ref[...], 0.0).astype(BF16)
            wso_ref[...] = ws
        else:
            ws = w_refs[1][...]
        small = _dot_nt(xn, ws)
        os_ref[...] = small
        this_layer(aki_ref)[...] = small[:, 0:IDX_DIM]

    acc = _dot_nt(xn_ref[...], wm)
    om_ref[...] = acc.astype(BF16)

    @pl.when((j >= COL_C_F // tn) & (j < COL_C_I // tn))
    def _():
        cf_ref[...] = acc

    def put_heads(ref, col, heads):
        off = col % tn
        assert off + heads * V7X_LANES <= tn

        @pl.when(j == col // tn)
        def _():
            dst = this_layer(ref)
            for hd in range(heads):
                dst[pl.ds(hd, tm, stride=heads), :] = acc[:, off + hd * V7X_LANES:off + (hd + 1) * V7X_LANES]

    put_heads(ak_ref, COL_A_K, A_KV_HEADS)
    put_heads(av_ref, COL_A_V, A_KV_HEADS)
    put_heads(bk_ref, COL_B_K, B_HEADS)
    put_heads(bv_ref, COL_B_V, B_HEADS)


def _proj(x, g, w, caches, layer, depth):
    n, d = x.shape
    emit = w[0] == "f32"
    tm = _row_tile(n, 512)
    tn = 512 if emit else PROJ_TN
    assert not emit or n == tm, "the weight-emitting variant expects a single row tile"
    lanes_per_tn = tn // V7X_LANES
    if emit:
        assert COL_B_Q % tn == 0 and NARROW_W % 8 == 0
        w_in_t = w[1]
        w_specs = [pl.BlockSpec((None, tn, d), lambda i, j: (layer, j, 0)),
                   pl.BlockSpec((None, V7X_LANES, d), lambda i, j: (layer, lanes_per_tn * (j + 1), 0)),
                   pl.BlockSpec((None, SMALL_W, d), lambda i, j: (layer, COL_B_Q // SMALL_W, 0))]
        w_args = [w_in_t, w_in_t, w_in_t]
    else:
        w_specs = [pl.BlockSpec((tn, d), lambda i, j: (j, 0)), pl.BlockSpec((SMALL_W, d), lambda i, j: (0, 0))]
        w_args = list(w[1:])
    cache_rows = (A_KV_HEADS, A_KV_HEADS, 1, B_HEADS, B_HEADS)
    cache_lanes = (V7X_LANES, V7X_LANES, IDX_DIM, V7X_LANES, V7X_LANES)
    alias_args = [] if caches is None else list(caches)
    out_specs = [pl.BlockSpec((tm, tn), lambda i, j: (i, j)), pl.BlockSpec((tm, SMALL_W), lambda i, j: (i, 0))]
    if alias_args:
        cache_spec = lambda r, wd: pl.BlockSpec((None, r * tm, wd), lambda i, j: (layer, i, 0))
    else:
        cache_spec = lambda r, wd: pl.BlockSpec((depth, r * tm, wd), lambda i, j: (0, i, 0))
    out_specs += [cache_spec(r, wd) for r, wd in zip(cache_rows, cache_lanes)]
    out_shape = [jax.ShapeDtypeStruct((n, MAIN_W), BF16), jax.ShapeDtypeStruct((n, SMALL_W), F32)]
    out_shape += [jax.ShapeDtypeStruct((depth, r * n, wd), F32) for r, wd in zip(cache_rows, cache_lanes)]
    if emit:
        out_specs += [pl.BlockSpec((tn, d), lambda i, j: (j, 0)), pl.BlockSpec((SMALL_W, d), lambda i, j: (0, 0))]
        out_shape += [jax.ShapeDtypeStruct((MAIN_W, d), BF16), jax.ShapeDtypeStruct((SMALL_W, d), BF16)]
    assert COL_C_F % tn == 0 and C_W % tn == 0
    cf_first, cf_tiles = COL_C_F // tn, C_W // tn
    out_specs.append(pl.BlockSpec((tm, tn), lambda i, j: (i, jnp.clip(j - cf_first, 0, cf_tiles - 1))))
    out_shape.append(jax.ShapeDtypeStruct((n, C_W), F32))
    n_in = 2 + len(w_args)
    res = pl.pallas_call(
        functools.partial(_proj_body, emit=emit, tn=tn, tm=tm, n_alias=len(alias_args), layer=layer),
        grid=(n // tm, MAIN_W // tn),
        in_specs=[pl.BlockSpec((tm, d), lambda i, j: (i, 0)), pl.BlockSpec((1, d), lambda i, j: (0, 0))]
        + w_specs + [pl.BlockSpec(memory_space=pl.ANY)] * len(alias_args),
        out_specs=out_specs,
        out_shape=out_shape,
        input_output_aliases={n_in + k: 2 + k for k in range(len(alias_args))},
        scratch_shapes=[pltpu.VMEM((tm, d), BF16)],
        compiler_params=_cparams(("parallel", "arbitrary")),
        name="in_proj_emit" if emit else "in_proj",
    )(x, g.reshape(1, d), *w_args, *alias_args)
    w_bf = ("bf16", res[7], res[8]) if emit else w
    return res[0], res[1], res[-1], tuple(res[2:7]), w_bf


def _sortable_key(x):
    bits = lax.bitcast_convert_type(x, jnp.int32)
    return jnp.where(bits < 0, bits ^ jnp.int32(0x7FFFFFFF), bits)


def _kth_largest(key, k):
    rows = key.shape[0]

    def body(it, t):
        cand = t + lax.shift_left(jnp.int32(1), jnp.int32(31) - it)
        cnt = jnp.sum(jnp.where(key >= cand, 1.0, 0.0), axis=1, keepdims=True)
        return jnp.where(cnt >= k, cand, t)

    t0 = jnp.full((rows, 1), jnp.iinfo(jnp.int32).min, jnp.int32)
    return lax.fori_loop(0, 32, body, t0)


def _tie_cut(tie, kpos, need, nbits):
    rows = tie.shape[0]

    def body(it, t):
        cand = t + lax.shift_left(jnp.int32(1), jnp.int32(nbits - 1) - it)
        cnt = jnp.sum(jnp.where(tie & (kpos < cand), 1.0, 0.0), axis=1, keepdims=True)
        return jnp.where(cnt < need, cand, t)

    return lax.fori_loop(0, nbits, body, jnp.zeros((rows, 1), jnp.int32))


def _visible(tq, s_pad, q_pos0, n_keys):
    qpos = q_pos0 + pl.program_id(1) * tq + lax.broadcasted_iota(jnp.int32, (tq, 1), 0)
    limit = jnp.minimum(((qpos >> 6) + 1) << 6, n_keys)
    kpos = lax.broadcasted_iota(jnp.int32, (tq, s_pad), 1)
    return kpos < limit, kpos


def _head_keys(new_ref, rows, lo, width, past_ref, head, heads, s_pad):
    new = new_ref[rows, lo:lo + width]
    p_len = past_ref.shape[0] // heads
    hist = past_ref[...] if heads == 1 else past_ref[pl.ds(head, p_len, stride=heads), :]
    parts = [hist.astype(BF16), new.astype(BF16)]
    pad = s_pad - p_len - new.shape[0]
    if pad:
        parts.append(jnp.zeros((pad, width), BF16))
    return jnp.concatenate(parts, axis=0)


def _stack_rows(pieces):
    return pieces[0] if len(pieces) == 1 else jnp.concatenate(pieces, axis=0)


def _dsa_hist_body(q_ref, qi_ref, wq_ref, k_ref, v_ref, ki_ref, pk_ref, pv_ref, pki_ref, o_ref,
                   *, nb, tq, s_pad, n_keys, q_pos0, topk):
    vis1, kpos1 = _visible(tq, s_pad, q_pos0, n_keys)
    rep = A_HEADS // A_KV_HEADS
    scores = []
    for bb in range(nb):
        rows = slice(bb * tq, (bb + 1) * tq)
        kidx = _head_keys(ki_ref, rows, 0, IDX_DIM, pki_ref.at[bb], 0, 1, s_pad)
        qi_all = _stack_rows([qi_ref[rows, h * IDX_DIM:(h + 1) * IDX_DIM] for h in range(IDX_HEADS)]).astype(BF16)
        wq = wq_ref[rows, IDX_DIM:IDX_DIM + IDX_HEADS] * (IDX_HEADS ** -0.5)
        w_col = _stack_rows([wq[:, h:h + 1] for h in range(IDX_HEADS)])
        x = jnp.maximum(_dot_nt(qi_all, kidx) * (IDX_DIM ** -0.5), 0.0) * w_col
        score = jnp.sum(x.reshape(IDX_HEADS, tq, s_pad), axis=0)
        scores.append(jnp.where(vis1, score + 0.0, NEG_BIG))

    key = _sortable_key(_stack_rows(scores))
    kpos = _stack_rows([kpos1] * nb)
    thr = _kth_largest(key, float(topk))
    n_gt = jnp.sum(jnp.where(key > thr, 1.0, 0.0), axis=1, keepdims=True)
    tie = key == thr
    cut = _tie_cut(tie, kpos, float(topk) - n_gt, max(1, (s_pad - 1).bit_length()))
    attend = jnp.where((key > thr) | (tie & (kpos <= cut)), 1.0, 0.0)

    for bb in range(nb):
        rows = slice(bb * tq, (bb + 1) * tq)
        sel = _stack_rows([jnp.where(vis1, attend[rows], 0.0)] * rep) > 0.5
        for g in range(A_KV_HEADS):
            k_g = _head_keys(k_ref, rows, g * A_HEAD_DIM, A_HEAD_DIM, pk_ref.at[bb], g, A_KV_HEADS, s_pad)
            v_g = _head_keys(v_ref, rows, g * A_HEAD_DIM, A_HEAD_DIM, pv_ref.at[bb], g, A_KV_HEADS, s_pad)
            heads = range(g * rep, (g + 1) * rep)
            q_g = _stack_rows([q_ref[rows, h * A_HEAD_DIM:(h + 1) * A_HEAD_DIM] for h in heads]).astype(BF16)
            att = jnp.where(sel, _dot_nt(q_g, k_g) * (A_HEAD_DIM ** -0.5), NEG_BIG)
            p = jnp.exp(att - jnp.max(att, axis=1, keepdims=True))
            l = jnp.sum(p, axis=1, keepdims=True)
            o = _dot(p.astype(BF16), v_g) / l
            for r, h in enumerate(heads):
                o_ref[rows, h * A_HEAD_DIM:(h + 1) * A_HEAD_DIM] = o[r * tq:(r + 1) * tq].astype(o_ref.dtype)


def _softmax_step_t(carry, s, v_c):
    m, l, acc = carry
    m_new = jnp.maximum(m, _col_reduce(s, jnp.max))
    alpha = jnp.exp(m - m_new)
    p = jnp.exp(s - m_new)
    return m_new, alpha * l + _col_reduce(p, jnp.sum), alpha * acc + _dot_tn(v_c, p.astype(BF16))


def _col_reduce(x, op):
    rows, tq = x.shape
    return op(op(x.reshape(rows // COUNT_ROWS, COUNT_ROWS, tq), axis=0), axis=0, keepdims=True)


def _softmax_init_t(tq, width):
    return (jnp.full((1, tq), -jnp.inf, F32), jnp.zeros((1, tq), F32), jnp.zeros((width, tq), F32))


def _causal_extent(tq, kc):
    i = pl.program_id(1)
    qpos = i * tq + lax.broadcasted_iota(jnp.int32, (1, tq), 1)
    limit = ((qpos >> 6) + 1) << 6
    return (i * tq + tq + kc - 1) // kc, limit, lax.broadcasted_iota(jnp.int32, (kc, 1), 0)


def _dsa_causal_body(q_ref, qi_ref, wq_ref, k_ref, v_ref, ki_ref, o_ref, key_ref, cut_ref, *, tq, kc, topk, nbits):
    n_chunks, limit, krow = _causal_extent(tq, kc)
    wq_t = wq_ref[...].T[IDX_DIM:IDX_DIM + IDX_HEADS, :] * (IDX_HEADS ** -0.5)
    qi_all = jnp.concatenate([qi_ref[:, h * IDX_DIM:(h + 1) * IDX_DIM] for h in range(IDX_HEADS)], axis=0).astype(BF16)

    def score_chunk(c, carry):
        start = pl.multiple_of(c * kc, kc)
        kidx = ki_ref[pl.ds(start, kc), 0:IDX_DIM].astype(BF16)
        logits = _dot_nt(kidx, qi_all)
        acc = jnp.zeros((kc, tq), F32)
        for h in range(IDX_HEADS):
            acc = acc + jnp.maximum(logits[:, h * tq:(h + 1) * tq] * (IDX_DIM ** -0.5), 0.0) * wq_t[h:h + 1, :]
        score = jnp.where(start + krow < limit, acc + 0.0, NEG_BIG)
        key_ref[c] = _sortable_key(score)
        return carry

    lax.fori_loop(0, n_chunks, score_chunk, 0)

    def count(pred):
        def body(c, part):
            ones = jnp.where(pred(key_ref[c], c), 1.0, 0.0)
            return part + jnp.sum(ones.reshape(kc // COUNT_ROWS, COUNT_ROWS, tq), axis=0)

        part = lax.fori_loop(0, n_chunks, body, jnp.zeros((COUNT_ROWS, tq), F32))
        return jnp.sum(part, axis=0, keepdims=True)

    def thr_step(it, t):
        cand = t + lax.shift_left(jnp.int32(1), jnp.int32(31) - it)
        return jnp.where(count(lambda key, c: key >= cand) >= float(topk), cand, t)

    thr = lax.fori_loop(0, 32, thr_step, jnp.full((1, tq), jnp.iinfo(jnp.int32).min, jnp.int32))
    need = float(topk) - count(lambda key, c: key > thr)
    n_tie = count(lambda key, c: key == thr)

    cut_ref[...] = jnp.full(cut_ref.shape, jnp.iinfo(jnp.int32).max, jnp.int32)

    @pl.when(jnp.max(jnp.where(n_tie != need, 1.0, 0.0)) > 0.0)
    def _():
        def cut_step(it, t):
            cand = t + lax.shift_left(jnp.int32(1), jnp.int32(nbits - 1) - it)
            below = count(lambda key, c: (key == thr) & (c * kc + krow < cand))
            return jnp.where(below < need, cand, t)

        cut = lax.fori_loop(0, nbits, cut_step, jnp.zeros((1, tq), jnp.int32))
        cut_ref[...] = jnp.broadcast_to(cut, cut_ref.shape)

    cut = cut_ref[0:1, :]

    def select_chunk(c, carry):
        key = key_ref[c]
        kpos = c * kc + krow
        tie_flag = jnp.where(key == thr, jnp.where(kpos <= cut, 0, 1), 1)
        key_ref[c] = jnp.where(kpos < limit, jnp.where(key > thr, 0, tie_flag), 1)
        return carry

    lax.fori_loop(0, n_chunks, select_chunk, 0)

    rep = A_HEADS // A_KV_HEADS
    q_g = [jnp.concatenate([q_ref[:, h * A_HEAD_DIM:(h + 1) * A_HEAD_DIM] for h in range(g * rep, (g + 1) * rep)],
                           axis=0).astype(BF16) for g in range(A_KV_HEADS)]

    def att_chunk(c, carry):
        start = pl.multiple_of(c * kc, kc)
        attend = key_ref[c] == 0
        attend = jnp.concatenate([attend] * rep, axis=1)
        logits = [_dot_nt(k_ref[pl.ds(start, kc), g * A_HEAD_DIM:(g + 1) * A_HEAD_DIM].astype(BF16), q_g[g])
                  for g in range(A_KV_HEADS)]
        v_c = [v_ref[pl.ds(start, kc), g * A_HEAD_DIM:(g + 1) * A_HEAD_DIM].astype(BF16) for g in range(A_KV_HEADS)]
        return tuple(
            _softmax_step_t(carry[g], jnp.where(attend, logits[g] * (A_HEAD_DIM ** -0.5), NEG_BIG), v_c[g])
            for g in range(A_KV_HEADS))

    res = lax.fori_loop(0, n_chunks, att_chunk, tuple(_softmax_init_t(rep * tq, A_HEAD_DIM) for _ in range(A_KV_HEADS)))
    for g in range(A_KV_HEADS):
        _, l, acc = res[g]
        o_t = acc / l
        for r in range(rep):
            h = g * rep + r
            o_ref[:, h * A_HEAD_DIM:(h + 1) * A_HEAD_DIM] = o_t[:, r * tq:(r + 1) * tq].T.astype(o_ref.dtype)


def _diff_causal_body(q_ref, k_ref, v_ref, lam_ref, g_ref, o_ref, *, tq, kc, lam_init):
    n_chunks, limit, krow = _causal_extent(tq, kc)
    lv = lam_ref[...]
    lam = (jnp.exp(jnp.sum(lv[0:1] * lv[1:2], axis=1, keepdims=True))
           - jnp.exp(jnp.sum(lv[2:3] * lv[3:4], axis=1, keepdims=True)) + lam_init)
    gain = g_ref[...] * (1.0 - lam_init)
    head_w = 2 * B_QK_DIM
    qlane = lax.broadcasted_iota(jnp.int32, (tq, head_w), 1)
    for h0 in range(0, B_HEADS, DIFF_HEADS_PER_LOOP):
        heads = range(h0, h0 + DIFF_HEADS_PER_LOOP)
        q_2 = [jnp.concatenate([jnp.where((qlane >= c * B_QK_DIM) & (qlane < (c + 1) * B_QK_DIM),
                                          q_ref[:, h * head_w:(h + 1) * head_w], 0.0) for c in range(2)],
                               axis=0).astype(BF16) for h in heads]

        def chunk(c, carry, q_2=q_2, heads=heads):
            start = pl.multiple_of(c * kc, kc)
            vis = start + krow < jnp.concatenate([limit, limit], axis=1)
            logits = [_dot_nt(k_ref[pl.ds(start, kc), h * head_w:(h + 1) * head_w].astype(BF16), q_2[j])
                      for j, h in enumerate(heads)]
            v_c = [v_ref[pl.ds(start, kc), h * B_V_DIM:(h + 1) * B_V_DIM].astype(BF16) for h in heads]
            return tuple(
                _softmax_step_t(carry[j], jnp.where(vis, logits[j] * (B_QK_DIM ** -0.5), NEG_BIG), v_c[j])
                for j in range(len(heads)))

        res = lax.fori_loop(0, n_chunks, chunk, tuple(_softmax_init_t(2 * tq, B_V_DIM) for _ in heads))
        for j, h in enumerate(heads):
            _, l, acc = res[j]
            o_t = acc / l
            o = (o_t[:, 0:tq] - lam * o_t[:, tq:2 * tq]).T
            o = o * lax.rsqrt(jnp.mean(o * o, axis=1, keepdims=True) + NORM_EPS) * gain
            o_ref[:, h * B_V_DIM:(h + 1) * B_V_DIM] = o.astype(o_ref.dtype)


def _blk(arr_col, rows, width, kind, nq):
    _, col = arr_col
    cb = col // width
    assert col % width == 0
    if kind == "q":
        return pl.BlockSpec((rows, width), lambda b, i: (b * nq + i, cb))
    return pl.BlockSpec((rows, width), lambda b, i: (b, cb))


def _past_blk(cache, layer, nb):
    return pl.BlockSpec((None, nb) + cache.shape[2:], lambda b, i: (layer, b, 0, 0))


def _seqs_per_step(bsz, limit):
    nb = min(bsz, limit)
    while bsz % nb:
        nb -= 1
    return nb


def _dsa(q, qi, wq, k, v, ki, past, *, bsz, t, s_pad, n_keys, q_pos0, tq, layer):
    nq = t // tq
    topk = min(TOPK_MAX, n_keys // 4)
    scratch = []
    if past is None:
        nb = 1
        kc = _row_tile(t, ATTN_KEY_CHUNK)
        body = functools.partial(_dsa_causal_body, tq=tq, kc=kc, topk=topk, nbits=max(1, (t - 1).bit_length()))
        scratch = [pltpu.VMEM((t // kc, kc, tq), jnp.int32), pltpu.VMEM((8, tq), jnp.int32)]
    else:
        assert nq == 1
        nb = _seqs_per_step(bsz, DSA_SEQS_PER_STEP)
        body = functools.partial(_dsa_hist_body, nb=nb, tq=tq, s_pad=s_pad, n_keys=n_keys, q_pos0=q_pos0, topk=topk)
    in_specs = [
        _blk(q, nb * tq, A_Q_W, "q", nq),
        _blk(qi, nb * tq, A_QI_W, "q", nq),
        _blk(wq, nb * tq, SMALL_W, "q", nq),
        _blk(k, nb * t, A_KV_W, "kv", nq),
        _blk(v, nb * t, A_KV_W, "kv", nq),
        _blk(ki, nb * t, SMALL_W, "kv", nq),
    ]
    args = [q[0], qi[0], wq[0], k[0], v[0], ki[0]]
    if past is not None:
        in_specs += [_past_blk(c, layer, nb) for c in past]
        args += list(past)
    return pl.pallas_call(
        body,
        grid=(bsz // nb, nq),
        in_specs=in_specs,
        out_specs=pl.BlockSpec((nb * tq, A_Q_W), lambda b, i: (b * nq + i, 0)),
        out_shape=jax.ShapeDtypeStruct((bsz * t, A_Q_W), BF16),
        scratch_shapes=scratch,
        compiler_params=_cparams(("parallel", "parallel")),
        name="dsa",
    )(*args)


def _diff_hist_body(q_ref, k_ref, v_ref, lam_ref, g_ref, pk_ref, pv_ref, o_ref,
                    *, nb, tq, s_pad, n_keys, q_pos0, lam_init):
    vis1, _ = _visible(tq, s_pad, q_pos0, n_keys)
    vis = _stack_rows([jnp.where(vis1, 1.0, 0.0)] * 2) > 0.5
    lv = lam_ref[...]
    lam = (jnp.exp(jnp.sum(lv[0:1] * lv[1:2], axis=1, keepdims=True))
           - jnp.exp(jnp.sum(lv[2:3] * lv[3:4], axis=1, keepdims=True)) + lam_init)
    gain = g_ref[...] * (1.0 - lam_init)
    head_w = 2 * B_QK_DIM
    lane = lax.broadcasted_iota(jnp.int32, (tq, head_w), 1)
    for bb in range(nb):
        rows = slice(bb * tq, (bb + 1) * tq)
        for h in range(B_HEADS):
            k_h = _head_keys(k_ref, rows, h * head_w, head_w, pk_ref.at[bb], h, B_HEADS, s_pad)
            v_h = _head_keys(v_ref, rows, h * B_V_DIM, B_V_DIM, pv_ref.at[bb], h, B_HEADS, s_pad)
            q_h = q_ref[rows, h * head_w:(h + 1) * head_w]
            q_2 = _stack_rows([jnp.where((lane >= c * B_QK_DIM) & (lane < (c + 1) * B_QK_DIM), q_h, 0.0)
                               for c in range(2)]).astype(BF16)
            logit = jnp.where(vis, _dot_nt(q_2, k_h) * (B_QK_DIM ** -0.5), NEG_BIG)
            p = jnp.exp(logit - jnp.max(logit, axis=1, keepdims=True))
            l = jnp.sum(p, axis=1, keepdims=True)
            o_2 = _dot(p.astype(BF16), v_h) / l
            o = o_2[0:tq] - lam * o_2[tq:2 * tq]
            o = o * lax.rsqrt(jnp.mean(o * o, axis=1, keepdims=True) + NORM_EPS) * gain
            o_ref[rows, h * B_V_DIM:(h + 1) * B_V_DIM] = o.astype(o_ref.dtype)


def _diff(q, k, v, lam_vecs, g_diff, past, *, bsz, t, s_pad, n_keys, q_pos0, tq, layer):
    nq = t // tq
    lam_init = 0.8 - 0.6 * math.exp(-0.3 * layer)
    if past is None:
        nb = 1
        body = functools.partial(_diff_causal_body, tq=tq, kc=_row_tile(t, ATTN_KEY_CHUNK), lam_init=lam_init)
    else:
        assert nq == 1
        nb = _seqs_per_step(bsz, DIFF_SEQS_PER_STEP)
        body = functools.partial(_diff_hist_body, nb=nb, tq=tq, s_pad=s_pad, n_keys=n_keys, q_pos0=q_pos0,
                                 lam_init=lam_init)
    in_specs = [
        _blk(q, nb * tq, B_QK_W, "q", nq),
        _blk(k, nb * t, B_QK_W, "kv", nq),
        _blk(v, nb * t, B_V_W, "kv", nq),
        pl.BlockSpec((None, 4, B_QK_DIM), lambda b, i: (layer, 0, 0)),
        pl.BlockSpec((None, 1, B_V_DIM), lambda b, i: (layer, 0, 0)),
    ]
    args = [q[0], k[0], v[0], lam_vecs, g_diff]
    if past is not None:
        in_specs += [_past_blk(c, layer, nb) for c in past]
        args += list(past)
    return pl.pallas_call(
        body,
        grid=(bsz // nb, nq),
        in_specs=in_specs,
        out_specs=pl.BlockSpec((nb * tq, B_V_W), lambda b, i: (b * nq + i, 0)),
        out_shape=jax.ShapeDtypeStruct((bsz * t, B_V_W), BF16),
        compiler_params=_cparams(("parallel", "parallel")),
        name="diff_attn",
    )(*args)


def _cumsum_rows(x):
    n = x.shape[0]
    row = lax.broadcasted_iota(jnp.int32, (n, 1), 0)
    d = 1
    while d < n:
        x = x + jnp.where(row >= d, pltpu.roll(x, d, axis=0), 0.0)
        d *= 2
    return x


def _anchor_rows(b, c):
    n, w = b.shape
    if 2 * c >= 8:
        pieces = [jnp.broadcast_to(b[j * 2 * c + c - 1:j * 2 * c + c, :], (2 * c, w)) for j in range(n // (2 * c))]
        return pieces[0] if len(pieces) == 1 else jnp.concatenate(pieces, axis=0)
    rmod = lax.broadcasted_iota(jnp.int32, (n, 1), 0) & (2 * c - 1)
    m = b
    for s in range(-c, c):
        if s != 0:
            m = jnp.where(rmod == c - 1 - s, pltpu.roll(b, (-s) % n, axis=0), m)
    return m


def _hgrn_body(*refs, chunk, layer, has_s0):
    if has_s0:
        q_ref, f_ref, i_ref, g_ref, glb_ref, gn_ref, s0_ref, o_ref, s_ref, st_ref = refs
    else:
        q_ref, f_ref, i_ref, g_ref, glb_ref, gn_ref, o_ref, s_ref, st_ref = refs
    ci = pl.program_id(1)

    @pl.when(ci == 0)
    def _():
        for h in range(C_HEADS):
            st_ref[h] = s0_ref[h].T if has_s0 else jnp.zeros((C_V_DIM, C_K_DIM), F32)

    glb = glb_ref[...]
    pe = jnp.exp(glb - jnp.max(glb, axis=0, keepdims=True))
    pn = pe / jnp.sum(pe, axis=0, keepdims=True)
    lb = jnp.sum(pn[0:layer + 1], axis=0, keepdims=True) - pn[0:1]

    zf = f_ref[...]
    log_sig = jnp.minimum(zf, 0.0) - jnp.log1p(jnp.exp(-jnp.abs(zf)))
    la = jnp.log(jnp.maximum(lb, LB_FLOOR))
    lc = jnp.log1p(-lb) + log_sig
    hi = jnp.maximum(la, lc)
    log_f = hi + jnp.log1p(jnp.exp(jnp.minimum(la, lc) - hi))
    kc = (1.0 - lb) * jax.nn.sigmoid(-zf)
    b_all = _cumsum_rows(log_f)

    row = lax.broadcasted_iota(jnp.int32, (chunk, 1), 0)
    rr = lax.broadcasted_iota(jnp.int32, (chunk, chunk), 0)
    cc = lax.broadcasted_iota(jnp.int32, (chunk, chunk), 1)
    heads = range(C_HEADS)
    head = lambda x, h: x[:, h * C_K_DIM:(h + 1) * C_K_DIM]

    q_all = q_ref[...].astype(F32)
    vb_all = i_ref[...]
    v_all = vb_all.astype(F32)
    b_last = b_all[chunk - 1:chunk, :]
    qe_all = (q_all * jnp.exp(b_all)).astype(BF16)
    kdec_all = (kc * jnp.exp(b_last - b_all)).astype(BF16)
    diag_all = q_all * kc
    levels = []
    c = chunk // 2
    while c >= 1:
        lg = c.bit_length() - 1
        upper = ((row >> lg) & 1) == 1
        d = b_all - _anchor_rows(b_all, c)
        e = jnp.exp(jnp.where(upper, d, -d))
        x = (jnp.where(upper, q_all, kc) * e).astype(BF16)
        pairs = ((rr >> (lg + 1)) == (cc >> (lg + 1))) & (((rr >> lg) & 1) == 1) & (((cc >> lg) & 1) == 0)
        levels.append((x, pairs))
        c //= 2

    st = [st_ref[h] for h in heads]
    o_inter = [_dot_nt(head(qe_all, h), st[h].astype(BF16)) for h in heads]
    a = []
    for h in heads:
        a_h = jnp.zeros((chunk, chunk), F32)
        for x, pairs in levels:
            a_h = jnp.where(pairs, _dot_nt(head(x, h), head(x, h)), a_h)
        a.append(a_h)
    for h in heads:
        st_new = st[h] * jnp.exp(head(b_last, h)) + _dot_tn(head(vb_all, h), head(kdec_all, h))
        st_ref[h] = st_new

        @pl.when(ci == pl.num_programs(1) - 1)
        def _():
            s_ref[0, h] = st_new.T

    gn = gn_ref[...]
    for h in heads:
        o = o_inter[h] + jnp.sum(head(diag_all, h), axis=1, keepdims=True) * head(v_all, h)
        o = o + _dot(a[h].astype(BF16), head(vb_all, h))
        o = o * lax.rsqrt(jnp.mean(o * o, axis=1, keepdims=True) + NORM_EPS) * gn
        zg = head(g_ref, h).astype(F32)
        o_ref[:, h * C_V_DIM:(h + 1) * C_V_DIM] = (o * (zg * jax.nn.sigmoid(zg))).astype(o_ref.dtype)


def _hgrn(cq, cf, cin, cg, gamma_lb, g_hgrn, s0, *, bsz, t, layer):
    chunk = _row_tile(t, HGRN_CHUNK)
    nc = t // chunk
    depth = gamma_lb.shape[0]
    has_s0 = s0 is not None

    def seq_blk(arr_col):
        cb = arr_col[1] // C_W
        return pl.BlockSpec((chunk, C_W), lambda b, c: (b * nc + c, cb))

    in_specs = [seq_blk(cq), seq_blk(cf), seq_blk(cin), seq_blk(cg),
                pl.BlockSpec((depth, C_W), lambda b, c: (0, 0)),
                pl.BlockSpec((None, 1, C_V_DIM), lambda b, c: (layer, 0, 0))]
    args = [cq[0], cf[0], cin[0], cg[0], gamma_lb, g_hgrn]
    if has_s0:
        in_specs.append(pl.BlockSpec((None, None, C_HEADS, C_K_DIM, C_V_DIM), lambda b, c: (layer, b, 0, 0, 0)))
        args.append(s0)
    return pl.pallas_call(
        functools.partial(_hgrn_body, chunk=chunk, layer=layer, has_s0=has_s0),
        grid=(bsz, nc),
        in_specs=in_specs,
        out_specs=[pl.BlockSpec((chunk, C_W), lambda b, c: (b * nc + c, 0)),
                   pl.BlockSpec((1, C_HEADS, C_K_DIM, C_V_DIM), lambda b, c: (b, 0, 0, 0))],
        out_shape=[jax.ShapeDtypeStruct((bsz * t, C_W), BF16),
                   jax.ShapeDtypeStruct((bsz, C_HEADS, C_K_DIM, C_V_DIM), F32)],
        scratch_shapes=[pltpu.VMEM((C_HEADS, C_V_DIM, C_K_DIM), F32)],
        compiler_params=_cparams(("parallel", "arbitrary")),
        name="hgrn2",
    )(*args)


def _outproj_body(*refs, emit):
    h_ref, oa_ref, ob_ref, oc_ref, wa_ref, wb_ref, wc_ref, o_ref = refs[:8]
    wa, wb, wc = wa_ref[...].astype(BF16), wb_ref[...].astype(BF16), wc_ref[...].astype(BF16)
    if emit:
        refs[8][...] = wa
        refs[9][...] = wb
        refs[10][...] = wc
    o_ref[...] = h_ref[...] + (_dot(oa_ref[...], wa) + _dot(ob_ref[...], wb) + _dot(oc_ref[...], wc))


def _outproj(h, oa, ob, oc, w):
    n, d = h.shape
    emit = w[0] == "f32"
    tm = _row_tile(n, 512)
    assert not emit or n == tm, "the weight-emitting variant expects a single row tile"
    row = lambda wd: pl.BlockSpec((tm, wd), lambda i: (i, 0))
    whole = lambda r: pl.BlockSpec((r, d), lambda i: (0, 0))
    seg_rows = (A_Q_W, B_V_W, C_W)
    seg_first = (0, A_Q_W, A_Q_W + B_V_W)
    if emit:
        _, w_out, layer = w
        w_specs = [pl.BlockSpec((None, r, d), functools.partial(lambda i, blk: (layer, blk, 0), blk=f // r))
                   for r, f in zip(seg_rows, seg_first)]
        w_args = [w_out] * 3
    else:
        w_specs = [whole(r) for r in seg_rows]
        w_args = list(w[1:])
    out_specs = [row(d)]
    out_shape = [jax.ShapeDtypeStruct((n, d), F32)]
    if emit:
        out_specs += [whole(r) for r in seg_rows]
        out_shape += [jax.ShapeDtypeStruct((r, d), BF16) for r in seg_rows]
    res = pl.pallas_call(
        functools.partial(_outproj_body, emit=emit),
        grid=(n // tm,),
        in_specs=[row(d), row(A_Q_W), row(B_V_W), row(C_W)] + w_specs,
        out_specs=out_specs,
        out_shape=out_shape,
        compiler_params=_cparams(("parallel",)),
        name="out_proj_emit" if emit else "out_proj",
    )(h, oa, ob, oc, *w_args)
    return (res[0], ("bf16", res[1], res[2], res[3])) if emit else (res[0], w)


def _layer(x, layer, w, p, caches, past, geom):
    bsz, t = geom["bsz"], geom["t"]
    depth = p["g_ffn1"].shape[0]
    wb = {}
    h, wb["ffn1"] = _ffn(x, p["g_ffn1"][layer], w["ffn1"])
    main, small, c_f, caches, wb["proj"] = _proj(h, p["g_mix"][layer], w["proj"], caches, layer, depth)
    if past is None:
        past_a = past_b = s0 = None
    else:
        past_a, past_b, s0 = past[0:3], past[3:5], past[5]
    o_a = _dsa((main, COL_A_Q), (main, COL_A_QI), (small, 0), (main, COL_A_K), (main, COL_A_V), (small, 0), past_a,
               layer=layer, **geom)
    o_b = _diff((main, COL_B_Q), (main, COL_B_K), (main, COL_B_V), p["diff_lambda"], p["g_diff"], past_b,
                layer=layer, **geom)
    o_c, s_c = _hgrn((main, COL_C_Q), (c_f, 0), (main, COL_C_I), (main, COL_C_G),
                     p["gamma_lb"], p["g_hgrn"], s0, bsz=bsz, t=t, layer=layer)
    h, wb["out"] = _outproj(h, o_a, o_b, o_c, w["out"])
    x, wb["ffn2"] = _ffn(h, p["g_ffn2"][layer], w["ffn2"], p["g_final"] if layer == depth - 1 else None)
    return x, caches, s_c, wb


def kernel(x_prompt, x_sample, cache_a_k, cache_a_v, cache_a_kidx, cache_b_k, cache_b_v, state_c,
           g_ffn1, w_ffn1_in, w_ffn1_out, g_mix, w_in, diff_lambda, g_diff, gamma_lb, g_hgrn, w_out,
           g_ffn2, w_ffn2_in, w_ffn2_out, g_final):
    depth, d = w_in.shape[0], w_in.shape[1]
    bp, tp, _ = x_prompt.shape
    bs, ts, _ = x_sample.shape
    p_len = cache_a_k.shape[2]
    p = dict(g_ffn1=g_ffn1, g_mix=g_mix, diff_lambda=diff_lambda, g_diff=g_diff.reshape(depth, 1, B_V_DIM),
             gamma_lb=gamma_lb, g_hgrn=g_hgrn.reshape(depth, 1, C_V_DIM), g_ffn2=g_ffn2, g_final=g_final)
    geom_p = dict(bsz=bp, t=tp, s_pad=tp, n_keys=tp, q_pos0=0, tq=_row_tile(tp, ATTN_QUERY_TILE))
    n_keys_s = p_len + ts
    geom_s = dict(bsz=bs, t=ts, s_pad=-(-n_keys_s // V7X_LANES) * V7X_LANES, n_keys=n_keys_s, q_pos0=p_len, tq=ts)
    past = (cache_a_k.reshape(depth, bs, p_len * A_KV_HEADS, A_HEAD_DIM),
            cache_a_v.reshape(depth, bs, p_len * A_KV_HEADS, A_HEAD_DIM),
            cache_a_kidx,
            cache_b_k.reshape(depth, bs, p_len * B_HEADS, 2 * B_QK_DIM),
            cache_b_v.reshape(depth, bs, p_len * B_HEADS, B_V_DIM),
            state_c)

    w_in_t = jnp.swapaxes(w_in, 1, 2)
    xp = x_prompt.reshape(bp * tp, d)
    xs = x_sample.reshape(bs * ts, d)
    caches_p = caches_s = None
    states_p, states_s = [], []
    for layer in range(depth):
        w_f32 = dict(ffn1=("f32", w_ffn1_in, w_ffn1_out, layer), proj=("f32", w_in_t), out=("f32", w_out, layer),
                     ffn2=("f32", w_ffn2_in, w_ffn2_out, layer))
        xs, caches_s, s_c, w_bf = _layer(xs, layer, w_f32, p, caches_s, past, geom_s)
        states_s.append(s_c)
        xp, caches_p, s_c, _ = _layer(xp, layer, w_bf, p, caches_p, None, geom_p)
        states_p.append(s_c)

    def group_out(x, caches, states, bsz, t):
        a_k, a_v, a_ki, b_k, b_v = caches
        return (x.reshape(bsz, t, d),
                a_k.reshape(depth, bsz, t, A_KV_HEADS, A_HEAD_DIM), a_v.reshape(depth, bsz, t, A_KV_HEADS, A_HEAD_DIM),
                a_ki.reshape(depth, bsz, t, IDX_DIM), b_k.reshape(depth, bsz, t, B_HEADS, 2 * B_QK_DIM),
                b_v.reshape(depth, bsz, t, B_HEADS, B_V_DIM), jnp.stack(states))

    out_p = group_out(xp, caches_p, states_p, bp, tp)
    out_s = group_out(xs, caches_s, states_s, bs, ts)
    return (out_p[0], out_s[0], *out_p[1:], *out_s[1:])
```

```python
import functools
import math

import jax
import jax.numpy as jnp
from jax import lax
from jax.experimental import pallas as pl
from jax.experimental.pallas import tpu as pltpu

F32 = jnp.float32
BF16 = jnp.bfloat16

CHUNK = 64
A_HEADS, A_KV_HEADS, A_HEAD_DIM = 4, 2, 128
IDX_HEADS, IDX_DIM = 8, 64
TOPK_MAX = 256
B_HEADS, B_QK_DIM, B_V_DIM = 4, 64, 128
C_HEADS, C_K_DIM, C_V_DIM = 8, 128, 128
IDX_SCALE = IDX_DIM ** -0.5
DIFF_SCALE = B_QK_DIM ** -0.5
assert math.frexp(IDX_SCALE)[0] == 0.5 and math.frexp(DIFF_SCALE)[0] == 0.5
NORM_EPS = 1e-6
NEG_BIG = -1e30
LB_FLOOR = 1e-30

A_Q_W = A_HEADS * A_HEAD_DIM
A_KV_W = A_KV_HEADS * A_HEAD_DIM
A_QI_W = IDX_HEADS * IDX_DIM
B_QK_W = B_HEADS * 2 * B_QK_DIM
B_V_W = B_HEADS * B_V_DIM
C_W = C_HEADS * C_K_DIM
MIX_W = A_Q_W + B_V_W + C_W
MAIN_W = A_Q_W + 2 * A_KV_W + A_QI_W + 2 * B_QK_W + B_V_W + 4 * C_W
NARROW_W = IDX_DIM + IDX_HEADS
SMALL_W = 128
PROJ_TN = 1024
ATTN_QUERY_TILE = 256
DSA_KEY_CHUNK = 256
DIFF_KEY_CHUNK = 512
COUNT_ROWS = 64
DIFF_HEADS_PER_LOOP = 4
DSA_SEQS_PER_STEP = 4
DIFF_SEQS_PER_STEP = 2
HGRN_CHUNK = 256
COL_A_Q, COL_A_K, COL_A_V, COL_A_QI = 0, 512, 768, 1024
COL_B_Q, COL_B_K, COL_B_V = 1536, 2048, 2560
COL_C_Q, COL_C_F, COL_C_I, COL_C_G = 3072, 4096, 5120, 6144

V7X_LANES = 128
V7X_VMEM_LIMIT = 56 * 1024 * 1024


def _cparams(semantics):
    return pltpu.CompilerParams(dimension_semantics=semantics, vmem_limit_bytes=V7X_VMEM_LIMIT)


def _rmsnorm(x, g):
    return x * lax.rsqrt(jnp.mean(x * x, axis=-1, keepdims=True) + NORM_EPS) * g


def _dot(a, b):
    return jnp.dot(a, b, preferred_element_type=F32)


def _dot_nt(a, b):
    return lax.dot_general(a, b, (((1,), (1,)), ((), ())), preferred_element_type=F32)


def _dot_tn(a, b):
    return lax.dot_general(a, b, (((0,), (0,)), ((), ())), preferred_element_type=F32)


def _row_tile(n, pref):
    t = min(n, pref)
    while n % t:
        t //= 2
    return t


def _ffn_body(*refs, n_ff_steps, final_norm, emit):
    x_ref, g_ref, wg_ref, wu_ref, wo_ref = refs[:5]
    gf_ref = refs[5] if final_norm else None
    outs = refs[5 + final_norm:]
    o_ref, xn_ref = outs[0], outs[-1]
    j = pl.program_id(1)

    @pl.when(j == 0)
    def _():
        x = x_ref[...]
        xn_ref[...] = _rmsnorm(x, g_ref[...]).astype(BF16)
        o_ref[...] = x

    wg, wu, wo = wg_ref[...].astype(BF16), wu_ref[...].astype(BF16), wo_ref[...].astype(BF16)
    if emit:
        outs[1][...] = wg
        outs[2][...] = wu
        outs[3][...] = wo
    xn = xn_ref[...]
    gate = _dot(xn, wg)
    up = _dot(xn, wu)
    act = (gate * jax.nn.sigmoid(gate)) * up * 0.5
    o_ref[...] += _dot(act.astype(BF16), wo)

    if final_norm:
        @pl.when(j == n_ff_steps - 1)
        def _():
            o_ref[...] = _rmsnorm(o_ref[...], gf_ref[...])


def _ffn(x, g, w, g_final=None):
    n, d = x.shape
    emit = w[0] == "f32"
    if emit:
        _, w_in, w_out, layer = w
        f = w_out.shape[1]
    else:
        _, wg, wu, wo = w
        f = wo.shape[0]
    tm = _row_tile(n, 512)
    tf = 512 if f % 512 == 0 else f
    nj = f // tf
    assert not emit or n == tm, "the weight-emitting variant expects a single row tile"
    final_norm = g_final is not None
    in_specs = [pl.BlockSpec((tm, d), lambda i, j: (i, 0)), pl.BlockSpec((1, d), lambda i, j: (0, 0))]
    if emit:
        in_specs += [pl.BlockSpec((None, d, tf), lambda i, j: (layer, 0, j)),
                     pl.BlockSpec((None, d, tf), lambda i, j: (layer, 0, j + nj)),
                     pl.BlockSpec((None, tf, d), lambda i, j: (layer, j, 0))]
        args = [x, g.reshape(1, d), w_in, w_in, w_out]
    else:
        in_specs += [pl.BlockSpec((d, tf), lambda i, j: (0, j)), pl.BlockSpec((d, tf), lambda i, j: (0, j)),
                     pl.BlockSpec((tf, d), lambda i, j: (j, 0))]
        args = [x, g.reshape(1, d), wg, wu, wo]
    if final_norm:
        in_specs.append(pl.BlockSpec((1, d), lambda i, j: (0, 0)))
        args.append(g_final.reshape(1, d))
    out_specs = [pl.BlockSpec((tm, d), lambda i, j: (i, 0))]
    out_shape = [jax.ShapeDtypeStruct((n, d), F32)]
    if emit:
        out_specs += [pl.BlockSpec((d, tf), lambda i, j: (0, j)), pl.BlockSpec((d, tf), lambda i, j: (0, j)),
                      pl.BlockSpec((tf, d), lambda i, j: (j, 0))]
        out_shape += [jax.ShapeDtypeStruct((d, f), BF16), jax.ShapeDtypeStruct((d, f), BF16),
                      jax.ShapeDtypeStruct((f, d), BF16)]
    res = pl.pallas_call(
        functools.partial(_ffn_body, n_ff_steps=nj, final_norm=final_norm, emit=emit),
        grid=(n // tm, nj),
        in_specs=in_specs,
        out_specs=out_specs,
        out_shape=out_shape,
        scratch_shapes=[pltpu.VMEM((tm, d), BF16)],
        compiler_params=_cparams(("parallel", "arbitrary")),
        name="ffn_emit" if emit else "ffn",
    )(*args)
    return (res[0], ("bf16", res[1], res[2], res[3])) if emit else (res[0], w)


def _proj_body(*refs, emit, tn, tm, n_alias, layer):
    n_w = 3 if emit else 2
    x_ref, g_ref = refs[:2]
    w_refs = refs[2:2 + n_w]
    outs = refs[2 + n_w + n_alias:-1]
    om_ref, os_ref, ak_ref, av_ref, aki_ref, bk_ref, bv_ref = outs[:7]
    cf_ref = outs[-1]
    xn_ref = refs[-1]
    j = pl.program_id(1)

    def this_layer(ref):
        if n_alias:
            return ref
        for other in range(ref.shape[0]):
            if other != layer:
                ref[other] = jnp.zeros(ref.shape[1:], F32)
        return ref.at[layer]

    if emit:
        wa_ref, wx_ref, wsm_ref = w_refs
        wmo_ref, wso_ref = outs[7:9]
        n_aligned = COL_B_Q // tn

        @pl.when(j < n_aligned)
        def _():
            wmo_ref[...] = wa_ref[...].astype(BF16)

        @pl.when(j >= n_aligned)
        def _():
            tall = jnp.concatenate([wa_ref[...], wx_ref[...]], axis=0)
            wmo_ref[...] = tall[NARROW_W:NARROW_W + tn, :].astype(BF16)

        wm = wmo_ref[...]
    else:
        wm = w_refs[0][...]

    @pl.when(j == 0)
    def _():
        xn = _rmsnorm(x_ref[...], g_ref[...]).astype(BF16)
        xn_ref[...] = xn
        if emit:
            out_col = lax.broadcasted_iota(jnp.int32, wsm_ref.shape, 0)
            ws = jnp.where(out_col < NARROW_W, wsm_ref[...], 0.0).astype(BF16)
            wso_ref[...] = ws
        else:
            ws = w_refs[1][...]
        small = _dot_nt(xn, ws)
        os_ref[...] = small
        this_layer(aki_ref)[...] = small[:, 0:IDX_DIM]

    acc = _dot_nt(xn_ref[...], wm)
    om_ref[...] = acc.astype(BF16)

    @pl.when((j >= COL_C_F // tn) & (j < COL_C_I // tn))
    def _():
        cf_ref[...] = acc

    def put_heads(ref, col, heads):
        off = col % tn
        assert off + heads * V7X_LANES <= tn

        @pl.when(j == col // tn)
        def _():
            dst = this_layer(ref)
            for hd in range(heads):
                dst[pl.ds(hd, tm, stride=heads), :] = acc[:, off + hd * V7X_LANES:off + (hd + 1) * V7X_LANES]

    put_heads(ak_ref, COL_A_K, A_KV_HEADS)
    put_heads(av_ref, COL_A_V, A_KV_HEADS)
    put_heads(bk_ref, COL_B_K, B_HEADS)
    put_heads(bv_ref, COL_B_V, B_HEADS)


def _proj(x, g, w, caches, layer, depth):
    n, d = x.shape
    emit = w[0] == "f32"
    tm = _row_tile(n, 512)
    tn = 512 if emit else PROJ_TN
    assert not emit or n == tm, "the weight-emitting variant expects a single row tile"
    lanes_per_tn = tn // V7X_LANES
    if emit:
        assert COL_B_Q % tn == 0 and NARROW_W % 8 == 0
        w_in_t = w[1]
        w_specs = [pl.BlockSpec((None, tn, d), lambda i, j: (layer, j, 0)),
                   pl.BlockSpec((None, V7X_LANES, d), lambda i, j: (layer, lanes_per_tn * (j + 1), 0)),
                   pl.BlockSpec((None, SMALL_W, d), lambda i, j: (layer, COL_B_Q // SMALL_W, 0))]
        w_args = [w_in_t, w_in_t, w_in_t]
    else:
        w_specs = [pl.BlockSpec((tn, d), lambda i, j: (j, 0)), pl.BlockSpec((SMALL_W, d), lambda i, j: (0, 0))]
        w_args = list(w[1:])
    cache_rows = (A_KV_HEADS, A_KV_HEADS, 1, B_HEADS, B_HEADS)
    cache_lanes = (V7X_LANES, V7X_LANES, IDX_DIM, V7X_LANES, V7X_LANES)
    alias_args = [] if caches is None else list(caches)
    out_specs = [pl.BlockSpec((tm, tn), lambda i, j: (i, j)), pl.BlockSpec((tm, SMALL_W), lambda i, j: (i, 0))]
    if alias_args:
        cache_spec = lambda r, wd: pl.BlockSpec((None, r * tm, wd), lambda i, j: (layer, i, 0))
    else:
        cache_spec = lambda r, wd: pl.BlockSpec((depth, r * tm, wd), lambda i, j: (0, i, 0))
    out_specs += [cache_spec(r, wd) for r, wd in zip(cache_rows, cache_lanes)]
    out_shape = [jax.ShapeDtypeStruct((n, MAIN_W), BF16), jax.ShapeDtypeStruct((n, SMALL_W), F32)]
    out_shape += [jax.ShapeDtypeStruct((depth, r * n, wd), F32) for r, wd in zip(cache_rows, cache_lanes)]
    if emit:
        out_specs += [pl.BlockSpec((tn, d), lambda i, j: (j, 0)), pl.BlockSpec((SMALL_W, d), lambda i, j: (0, 0))]
        out_shape += [jax.ShapeDtypeStruct((MAIN_W, d), BF16), jax.ShapeDtypeStruct((SMALL_W, d), BF16)]
    assert COL_C_F % tn == 0 and C_W % tn == 0
    cf_first, cf_tiles = COL_C_F // tn, C_W // tn
    out_specs.append(pl.BlockSpec((tm, tn), lambda i, j: (i, jnp.clip(j - cf_first, 0, cf_tiles - 1))))
    out_shape.append(jax.ShapeDtypeStruct((n, C_W), F32))
    n_in = 2 + len(w_args)
    res = pl.pallas_call(
        functools.partial(_proj_body, emit=emit, tn=tn, tm=tm, n_alias=len(alias_args), layer=layer),
        grid=(n // tm, MAIN_W // tn),
        in_specs=[pl.BlockSpec((tm, d), lambda i, j: (i, 0)), pl.BlockSpec((1, d), lambda i, j: (0, 0))]
        + w_specs + [pl.BlockSpec(memory_space=pl.ANY)] * len(alias_args),
        out_specs=out_specs,
        out_shape=out_shape,
        input_output_aliases={n_in + k: 2 + k for k in range(len(alias_args))},
        scratch_shapes=[pltpu.VMEM((tm, d), BF16)],
        compiler_params=_cparams(("parallel", "arbitrary")),
        name="in_proj_emit" if emit else "in_proj",
    )(x, g.reshape(1, d), *w_args, *alias_args)
    w_bf = ("bf16", res[7], res[8]) if emit else w
    return res[0], res[1], res[-1], tuple(res[2:7]), w_bf


def _sortable_key(x):
    bits = lax.bitcast_convert_type(x, jnp.int32)
    return jnp.where(bits < 0, bits ^ jnp.int32(0x7FFFFFFF), bits)


def _kth_largest(key, k):
    rows = key.shape[0]

    def body(it, t):
        cand = t + lax.shift_left(jnp.int32(1), jnp.int32(31) - it)
        cnt = jnp.sum(jnp.where(key >= cand, 1.0, 0.0), axis=1, keepdims=True)
        return jnp.where(cnt >= k, cand, t)

    t0 = jnp.full((rows, 1), jnp.iinfo(jnp.int32).min, jnp.int32)
    return lax.fori_loop(0, 32, body, t0)


def _tie_cut(tie, kpos, need, nbits):
    rows = tie.shape[0]

    def body(it, t):
        cand = t + lax.shift_left(jnp.int32(1), jnp.int32(nbits - 1) - it)
        cnt = jnp.sum(jnp.where(tie & (kpos < cand), 1.0, 0.0), axis=1, keepdims=True)
        return jnp.where(cnt < need, cand, t)

    return lax.fori_loop(0, nbits, body, jnp.zeros((rows, 1), jnp.int32))


def _visible(tq, s_pad, q_pos0, n_keys):
    qpos = q_pos0 + pl.program_id(1) * tq + lax.broadcasted_iota(jnp.int32, (tq, 1), 0)
    limit = jnp.minimum(((qpos >> 6) + 1) << 6, n_keys)
    kpos = lax.broadcasted_iota(jnp.int32, (tq, s_pad), 1)
    return kpos < limit, kpos


def _head_keys(new_ref, rows, lo, width, past_ref, head, heads, s_pad):
    new = new_ref[rows, lo:lo + width]
    p_len = past_ref.shape[0] // heads
    hist = past_ref[...] if heads == 1 else past_ref[pl.ds(head, p_len, stride=heads), :]
    parts = [hist.astype(BF16), new.astype(BF16)]
    pad = s_pad - p_len - new.shape[0]
    if pad:
        parts.append(jnp.zeros((pad, width), BF16))
    return jnp.concatenate(parts, axis=0)


def _stack_rows(pieces):
    return pieces[0] if len(pieces) == 1 else jnp.concatenate(pieces, axis=0)


def _dsa_hist_body(q_ref, qi_ref, wq_ref, k_ref, v_ref, ki_ref, pk_ref, pv_ref, pki_ref, o_ref,
                   *, nb, tq, s_pad, n_keys, q_pos0, topk):
    vis1, kpos1 = _visible(tq, s_pad, q_pos0, n_keys)
    rep = A_HEADS // A_KV_HEADS
    scores = []
    for bb in range(nb):
        rows = slice(bb * tq, (bb + 1) * tq)
        kidx = _head_keys(ki_ref, rows, 0, IDX_DIM, pki_ref.at[bb], 0, 1, s_pad)
        qi_all = _stack_rows([qi_ref[rows, h * IDX_DIM:(h + 1) * IDX_DIM] for h in range(IDX_HEADS)]).astype(BF16)
        wq = wq_ref[rows, IDX_DIM:IDX_DIM + IDX_HEADS] * (IDX_HEADS ** -0.5) * IDX_SCALE
        w_col = _stack_rows([wq[:, h:h + 1] for h in range(IDX_HEADS)])
        x = jnp.maximum(_dot_nt(qi_all, kidx), 0.0) * w_col
        score = jnp.sum(x.reshape(IDX_HEADS, tq, s_pad), axis=0)
        scores.append(jnp.where(vis1, score + 0.0, NEG_BIG))

    key = _sortable_key(_stack_rows(scores))
    kpos = _stack_rows([kpos1] * nb)
    thr = _kth_largest(key, float(topk))
    n_gt = jnp.sum(jnp.where(key > thr, 1.0, 0.0), axis=1, keepdims=True)
    tie = key == thr
    cut = _tie_cut(tie, kpos, float(topk) - n_gt, max(1, (s_pad - 1).bit_length()))
    attend = jnp.where((key > thr) | (tie & (kpos <= cut)), 1.0, 0.0)

    for bb in range(nb):
        rows = slice(bb * tq, (bb + 1) * tq)
        sel = _stack_rows([jnp.where(vis1, attend[rows], 0.0)] * rep) > 0.5
        for g in range(A_KV_HEADS):
            k_g = _head_keys(k_ref, rows, g * A_HEAD_DIM, A_HEAD_DIM, pk_ref.at[bb], g, A_KV_HEADS, s_pad)
            v_g = _head_keys(v_ref, rows, g * A_HEAD_DIM, A_HEAD_DIM, pv_ref.at[bb], g, A_KV_HEADS, s_pad)
            heads = range(g * rep, (g + 1) * rep)
            q_g = _stack_rows([q_ref[rows, h * A_HEAD_DIM:(h + 1) * A_HEAD_DIM] for h in heads]).astype(BF16)
            att = jnp.where(sel, _dot_nt(q_g, k_g) * (A_HEAD_DIM ** -0.5), NEG_BIG)
            p = jnp.exp(att - jnp.max(att, axis=1, keepdims=True))
            l = jnp.sum(p, axis=1, keepdims=True)
            o = _dot(p.astype(BF16), v_g) / l
            for r, h in enumerate(heads):
                o_ref[rows, h * A_HEAD_DIM:(h + 1) * A_HEAD_DIM] = o[r * tq:(r + 1) * tq].astype(o_ref.dtype)


def _softmax_step_t(carry, s, v_c):
    m, l, acc = carry
    m_new = jnp.maximum(m, _col_reduce(s, jnp.max))
    alpha = jnp.exp(m - m_new)
    p = jnp.exp(s - m_new)
    return m_new, alpha * l + _col_reduce(p, jnp.sum), alpha * acc + _dot_tn(v_c, p.astype(BF16))


def _col_reduce(x, op):
    rows, tq = x.shape
    return op(op(x.reshape(rows // COUNT_ROWS, COUNT_ROWS, tq), axis=0), axis=0, keepdims=True)


def _softmax_init_t(tq, width):
    return (jnp.full((1, tq), -jnp.inf, F32), jnp.zeros((1, tq), F32), jnp.zeros((width, tq), F32))


def _causal_extent(tq, kc):
    i = pl.program_id(1)
    qpos = i * tq + lax.broadcasted_iota(jnp.int32, (1, tq), 1)
    limit = ((qpos >> 6) + 1) << 6
    return (i * tq + tq + kc - 1) // kc, limit, lax.broadcasted_iota(jnp.int32, (kc, 1), 0)


def _dsa_causal_body(q_ref, qi_ref, wq_ref, k_ref, v_ref, ki_ref, o_ref, key_ref, cut_ref, *, tq, kc, topk, nbits):
    n_chunks, limit, krow = _causal_extent(tq, kc)
    wq_t = wq_ref[...].T[IDX_DIM:IDX_DIM + IDX_HEADS, :] * (IDX_HEADS ** -0.5) * IDX_SCALE
    qi_all = jnp.concatenate([qi_ref[:, h * IDX_DIM:(h + 1) * IDX_DIM] for h in range(IDX_HEADS)], axis=0).astype(BF16)

    def score_chunk(c, carry):
        start = pl.multiple_of(c * kc, kc)
        kidx = ki_ref[pl.ds(start, kc), 0:IDX_DIM].astype(BF16)
        logits = _dot_nt(kidx, qi_all)
        acc = jnp.zeros((kc, tq), F32)
        for h in range(IDX_HEADS):
            acc = acc + jnp.maximum(logits[:, h * tq:(h + 1) * tq], 0.0) * wq_t[h:h + 1, :]
        score = jnp.where(start + krow < limit, acc + 0.0, NEG_BIG)
        key_ref[c] = _sortable_key(score)
        return carry

    lax.fori_loop(0, n_chunks, score_chunk, 0)

    def count(pred):
        def body(c, part):
            ones = jnp.where(pred(key_ref[c], c), 1.0, 0.0)
            return part + jnp.sum(ones.reshape(kc // COUNT_ROWS, COUNT_ROWS, tq), axis=0)

        part = lax.fori_loop(0, n_chunks, body, jnp.zeros((COUNT_ROWS, tq), F32))
        return jnp.sum(part, axis=0, keepdims=True)

    def thr_step(it, t):
        cand = t + lax.shift_left(jnp.int32(1), jnp.int32(31) - it)
        return jnp.where(count(lambda key, c: key >= cand) >= float(topk), cand, t)

    thr = lax.fori_loop(0, 32, thr_step, jnp.full((1, tq), jnp.iinfo(jnp.int32).min, jnp.int32))
    need = float(topk) - count(lambda key, c: key > thr)
    n_tie = count(lambda key, c: key == thr)

    cut_ref[...] = jnp.full(cut_ref.shape, jnp.iinfo(jnp.int32).max, jnp.int32)

    @pl.when(jnp.max(jnp.where(n_tie != need, 1.0, 0.0)) > 0.0)
    def _():
        def cut_step(it, t):
            cand = t + lax.shift_left(jnp.int32(1), jnp.int32(nbits - 1) - it)
            below = count(lambda key, c: (key == thr) & (c * kc + krow < cand))
            return jnp.where(below < need, cand, t)

        cut = lax.fori_loop(0, nbits, cut_step, jnp.zeros((1, tq), jnp.int32))
        cut_ref[...] = jnp.broadcast_to(cut, cut_ref.shape)

    cut = cut_ref[0:1, :]

    def select_chunk(c, carry):
        key = key_ref[c]
        kpos = c * kc + krow
        tie_flag = jnp.where(key == thr, jnp.where(kpos <= cut, 0, 1), 1)
        key_ref[c] = jnp.where(kpos < limit, jnp.where(key > thr, 0, tie_flag), 1)
        return carry

    lax.fori_loop(0, n_chunks, select_chunk, 0)

    rep = A_HEADS // A_KV_HEADS
    q_g = [jnp.concatenate([q_ref[:, h * A_HEAD_DIM:(h + 1) * A_HEAD_DIM] for h in range(g * rep, (g + 1) * rep)],
                           axis=0).astype(BF16) for g in range(A_KV_HEADS)]

    def att_chunk(c, carry):
        start = pl.multiple_of(c * kc, kc)
        attend = key_ref[c] == 0
        attend = jnp.concatenate([attend] * rep, axis=1)
        logits = [_dot_nt(k_ref[pl.ds(start, kc), g * A_HEAD_DIM:(g + 1) * A_HEAD_DIM].astype(BF16), q_g[g])
                  for g in range(A_KV_HEADS)]
        v_c = [v_ref[pl.ds(start, kc), g * A_HEAD_DIM:(g + 1) * A_HEAD_DIM].astype(BF16) for g in range(A_KV_HEADS)]
        return tuple(
            _softmax_step_t(carry[g], jnp.where(attend, logits[g] * (A_HEAD_DIM ** -0.5), NEG_BIG), v_c[g])
            for g in range(A_KV_HEADS))

    res = lax.fori_loop(0, n_chunks, att_chunk, tuple(_softmax_init_t(rep * tq, A_HEAD_DIM) for _ in range(A_KV_HEADS)))
    for g in range(A_KV_HEADS):
        _, l, acc = res[g]
        o_t = acc / l
        for r in range(rep):
            h = g * rep + r
            o_ref[:, h * A_HEAD_DIM:(h + 1) * A_HEAD_DIM] = o_t[:, r * tq:(r + 1) * tq].T.astype(o_ref.dtype)


def _diff_causal_body(q_ref, k_ref, v_ref, lam_ref, g_ref, o_ref, *, tq, kc, lam_init):
    n_chunks, limit, krow = _causal_extent(tq, kc)
    lv = lam_ref[...]
    lam = (jnp.exp(jnp.sum(lv[0:1] * lv[1:2], axis=1, keepdims=True))
           - jnp.exp(jnp.sum(lv[2:3] * lv[3:4], axis=1, keepdims=True)) + lam_init)
    gain = g_ref[...] * (1.0 - lam_init)
    head_w = 2 * B_QK_DIM
    qlane = lax.broadcasted_iota(jnp.int32, (tq, head_w), 1)
    for h0 in range(0, B_HEADS, DIFF_HEADS_PER_LOOP):
        heads = range(h0, h0 + DIFF_HEADS_PER_LOOP)
        q_2 = [jnp.concatenate([jnp.where((qlane >= c * B_QK_DIM) & (qlane < (c + 1) * B_QK_DIM),
                                          q_ref[:, h * head_w:(h + 1) * head_w] * DIFF_SCALE, 0.0) for c in range(2)],
                               axis=0).astype(BF16) for h in heads]

        def chunk(c, carry, q_2=q_2, heads=heads):
            start = pl.multiple_of(c * kc, kc)
            vis = start + krow < jnp.concatenate([limit, limit], axis=1)
            logits = [_dot_nt(k_ref[pl.ds(start, kc), h * head_w:(h + 1) * head_w].astype(BF16), q_2[j])
                      for j, h in enumerate(heads)]
            v_c = [v_ref[pl.ds(start, kc), h * B_V_DIM:(h + 1) * B_V_DIM].astype(BF16) for h in heads]
            return tuple(
                _softmax_step_t(carry[j], jnp.where(vis, logits[j], NEG_BIG), v_c[j])
                for j in range(len(heads)))

        res = lax.fori_loop(0, n_chunks, chunk, tuple(_softmax_init_t(2 * tq, B_V_DIM) for _ in heads))
        for j, h in enumerate(heads):
            _, l, acc = res[j]
            o_t = acc / l
            o = (o_t[:, 0:tq] - lam * o_t[:, tq:2 * tq]).T
            o = o * lax.rsqrt(jnp.mean(o * o, axis=1, keepdims=True) + NORM_EPS) * gain
            o_ref[:, h * B_V_DIM:(h + 1) * B_V_DIM] = o.astype(o_ref.dtype)


def _blk(arr_col, rows, width, kind, nq):
    _, col = arr_col
    cb = col // width
    assert col % width == 0
    if kind == "q":
        return pl.BlockSpec((rows, width), lambda b, i: (b * nq + i, cb))
    return pl.BlockSpec((rows, width), lambda b, i: (b, cb))


def _past_blk(cache, layer, nb):
    return pl.BlockSpec((None, nb) + cache.shape[2:], lambda b, i: (layer, b, 0, 0))


def _seqs_per_step(bsz, limit):
    nb = min(bsz, limit)
    while bsz % nb:
        nb -= 1
    return nb


def _dsa(q, qi, wq, k, v, ki, past, *, bsz, t, s_pad, n_keys, q_pos0, tq, layer):
    nq = t // tq
    topk = min(TOPK_MAX, n_keys // 4)
    scratch = []
    if past is None:
        nb = 1
        kc = _row_tile(t, DSA_KEY_CHUNK)
        body = functools.partial(_dsa_causal_body, tq=tq, kc=kc, topk=topk, nbits=max(1, (t - 1).bit_length()))
        scratch = [pltpu.VMEM((t // kc, kc, tq), jnp.int32), pltpu.VMEM((8, tq), jnp.int32)]
    else:
        assert nq == 1
        nb = _seqs_per_step(bsz, DSA_SEQS_PER_STEP)
        body = functools.partial(_dsa_hist_body, nb=nb, tq=tq, s_pad=s_pad, n_keys=n_keys, q_pos0=q_pos0, topk=topk)
    in_specs = [
        _blk(q, nb * tq, A_Q_W, "q", nq),
        _blk(qi, nb * tq, A_QI_W, "q", nq),
        _blk(wq, nb * tq, SMALL_W, "q", nq),
        _blk(k, nb * t, A_KV_W, "kv", nq),
        _blk(v, nb * t, A_KV_W, "kv", nq),
        _blk(ki, nb * t, SMALL_W, "kv", nq),
    ]
    args = [q[0], qi[0], wq[0], k[0], v[0], ki[0]]
    if past is not None:
        in_specs += [_past_blk(c, layer, nb) for c in past]
        args += list(past)
    return pl.pallas_call(
        body,
        grid=(bsz // nb, nq),
        in_specs=in_specs,
        out_specs=pl.BlockSpec((nb * tq, A_Q_W), lambda b, i: (b * nq + i, 0)),
        out_shape=jax.ShapeDtypeStruct((bsz * t, A_Q_W), BF16),
        scratch_shapes=scratch,
        compiler_params=_cparams(("parallel", "parallel")),
        name="dsa",
    )(*args)


def _diff_hist_body(q_ref, k_ref, v_ref, lam_ref, g_ref, pk_ref, pv_ref, o_ref,
                    *, nb, tq, s_pad, n_keys, q_pos0, lam_init):
    vis1, _ = _visible(tq, s_pad, q_pos0, n_keys)
    vis = _stack_rows([jnp.where(vis1, 1.0, 0.0)] * 2) > 0.5
    lv = lam_ref[...]
    lam = (jnp.exp(jnp.sum(lv[0:1] * lv[1:2], axis=1, keepdims=True))
           - jnp.exp(jnp.sum(lv[2:3] * lv[3:4], axis=1, keepdims=True)) + lam_init)
    gain = g_ref[...] * (1.0 - lam_init)
    head_w = 2 * B_QK_DIM
    lane = lax.broadcasted_iota(jnp.int32, (tq, head_w), 1)
    for bb in range(nb):
        rows = slice(bb * tq, (bb + 1) * tq)
        for h in range(B_HEADS):
            k_h = _head_keys(k_ref, rows, h * head_w, head_w, pk_ref.at[bb], h, B_HEADS, s_pad)
            v_h = _head_keys(v_ref, rows, h * B_V_DIM, B_V_DIM, pv_ref.at[bb], h, B_HEADS, s_pad)
            q_h = q_ref[rows, h * head_w:(h + 1) * head_w]
            q_2 = _stack_rows([jnp.where((lane >= c * B_QK_DIM) & (lane < (c + 1) * B_QK_DIM), q_h * DIFF_SCALE, 0.0)
                               for c in range(2)]).astype(BF16)
            logit = jnp.where(vis, _dot_nt(q_2, k_h), NEG_BIG)
            p = jnp.exp(logit - jnp.max(logit, axis=1, keepdims=True))
            l = jnp.sum(p, axis=1, keepdims=True)
            o_2 = _dot(p.astype(BF16), v_h) / l
            o = o_2[0:tq] - lam * o_2[tq:2 * tq]
            o = o * lax.rsqrt(jnp.mean(o * o, axis=1, keepdims=True) + NORM_EPS) * gain
            o_ref[rows, h * B_V_DIM:(h + 1) * B_V_DIM] = o.astype(o_ref.dtype)


def _diff(q, k, v, lam_vecs, g_diff, past, *, bsz, t, s_pad, n_keys, q_pos0, tq, layer):
    nq = t // tq
    lam_init = 0.8 - 0.6 * math.exp(-0.3 * layer)
    if past is None:
        nb = 1
        body = functools.partial(_diff_causal_body, tq=tq, kc=_row_tile(t, DIFF_KEY_CHUNK), lam_init=lam_init)
    else:
        assert nq == 1
        nb = _seqs_per_step(bsz, DIFF_SEQS_PER_STEP)
        body = functools.partial(_diff_hist_body, nb=nb, tq=tq, s_pad=s_pad, n_keys=n_keys, q_pos0=q_pos0,
                                 lam_init=lam_init)
    in_specs = [
        _blk(q, nb * tq, B_QK_W, "q", nq),
        _blk(k, nb * t, B_QK_W, "kv", nq),
        _blk(v, nb * t, B_V_W, "kv", nq),
        pl.BlockSpec((None, 4, B_QK_DIM), lambda b, i: (layer, 0, 0)),
        pl.BlockSpec((None, 1, B_V_DIM), lambda b, i: (layer, 0, 0)),
    ]
    args = [q[0], k[0], v[0], lam_vecs, g_diff]
    if past is not None:
        in_specs += [_past_blk(c, layer, nb) for c in past]
        args += list(past)
    return pl.pallas_call(
        body,
        grid=(bsz // nb, nq),
        in_specs=in_specs,
        out_specs=pl.BlockSpec((nb * tq, B_V_W), lambda b, i: (b * nq + i, 0)),
        out_shape=jax.ShapeDtypeStruct((bsz * t, B_V_W), BF16),
        compiler_params=_cparams(("parallel", "parallel")),
        name="diff_attn",
    )(*args)


def _cumsum_rows(x):
    n = x.shape[0]
    row = lax.broadcasted_iota(jnp.int32, (n, 1), 0)
    d = 1
    while d < n:
        x = x + jnp.where(row >= d, pltpu.roll(x, d, axis=0), 0.0)
        d *= 2
    return x


def _anchor_rows(b, c):
    n, w = b.shape
    if 2 * c >= 8:
        pieces = [jnp.broadcast_to(b[j * 2 * c + c - 1:j * 2 * c + c, :], (2 * c, w)) for j in range(n // (2 * c))]
        return pieces[0] if len(pieces) == 1 else jnp.concatenate(pieces, axis=0)
    rmod = lax.broadcasted_iota(jnp.int32, (n, 1), 0) & (2 * c - 1)
    m = b
    for s in range(-c, c):
        if s != 0:
            m = jnp.where(rmod == c - 1 - s, pltpu.roll(b, (-s) % n, axis=0), m)
    return m


def _hgrn_body(*refs, chunk, layer, has_s0):
    if has_s0:
        q_ref, f_ref, i_ref, g_ref, glb_ref, gn_ref, s0_ref, o_ref, s_ref, st_ref = refs
    else:
        q_ref, f_ref, i_ref, g_ref, glb_ref, gn_ref, o_ref, s_ref, st_ref = refs
    ci = pl.program_id(1)

    @pl.when(ci == 0)
    def _():
        for h in range(C_HEADS):
            st_ref[h] = s0_ref[h].T if has_s0 else jnp.zeros((C_V_DIM, C_K_DIM), F32)

    glb = glb_ref[...]
    pe = jnp.exp(glb - jnp.max(glb, axis=0, keepdims=True))
    pn = pe / jnp.sum(pe, axis=0, keepdims=True)
    lb = jnp.sum(pn[0:layer + 1], axis=0, keepdims=True) - pn[0:1]

    zf = f_ref[...]
    log_sig = jnp.minimum(zf, 0.0) - jnp.log1p(jnp.exp(-jnp.abs(zf)))
    la = jnp.log(jnp.maximum(lb, LB_FLOOR))
    lc = jnp.log1p(-lb) + log_sig
    hi = jnp.maximum(la, lc)
    log_f = hi + jnp.log1p(jnp.exp(jnp.minimum(la, lc) - hi))
    kc = (1.0 - lb) * jax.nn.sigmoid(-zf)
    b_all = _cumsum_rows(log_f)

    row = lax.broadcasted_iota(jnp.int32, (chunk, 1), 0)
    rr = lax.broadcasted_iota(jnp.int32, (chunk, chunk), 0)
    cc = lax.broadcasted_iota(jnp.int32, (chunk, chunk), 1)
    heads = range(C_HEADS)
    head = lambda x, h: x[:, h * C_K_DIM:(h + 1) * C_K_DIM]

    q_all = q_ref[...].astype(F32)
    vb_all = i_ref[...]
    v_all = vb_all.astype(F32)
    b_last = b_all[chunk - 1:chunk, :]
    qe_all = (q_all * jnp.exp(b_all)).astype(BF16)
    kdec_all = (kc * jnp.exp(b_last - b_all)).astype(BF16)
    diag_all = q_all * kc
    levels = []
    c = chunk // 2
    while c >= 1:
        lg = c.bit_length() - 1
        upper = ((row >> lg) & 1) == 1
        d = b_all - _anchor_rows(b_all, c)
        e = jnp.exp(jnp.where(upper, d, -d))
        x = (jnp.where(upper, q_all, kc) * e).astype(BF16)
        pairs = ((rr >> (lg + 1)) == (cc >> (lg + 1))) & (((rr >> lg) & 1) == 1) & (((cc >> lg) & 1) == 0)
        levels.append((x, pairs))
        c //= 2

    st = [st_ref[h] for h in heads]
    o_inter = [_dot_nt(head(qe_all, h), st[h].astype(BF16)) for h in heads]
    a = []
    for h in heads:
        a_h = jnp.zeros((chunk, chunk), F32)
        for x, pairs in levels:
            a_h = jnp.where(pairs, _dot_nt(head(x, h), head(x, h)), a_h)
        a.append(a_h)
    for h in heads:
        st_new = st[h] * jnp.exp(head(b_last, h)) + _dot_tn(head(vb_all, h), head(kdec_all, h))
        st_ref[h] = st_new

        @pl.when(ci == pl.num_programs(1) - 1)
        def _():
            s_ref[0, h] = st_new.T

    gn = gn_ref[...]
    for h in heads:
        o = o_inter[h] + jnp.sum(head(diag_all, h), axis=1, keepdims=True) * head(v_all, h)
        o = o + _dot(a[h].astype(BF16), head(vb_all, h))
        o = o * lax.rsqrt(jnp.mean(o * o, axis=1, keepdims=True) + NORM_EPS) * gn
        zg = head(g_ref, h).astype(F32)
        o_ref[:, h * C_V_DIM:(h + 1) * C_V_DIM] = (o * (zg * jax.nn.sigmoid(zg))).astype(o_ref.dtype)


def _hgrn(cq, cf, cin, cg, gamma_lb, g_hgrn, s0, *, bsz, t, layer):
    chunk = _row_tile(t, HGRN_CHUNK)
    nc = t // chunk
    depth = gamma_lb.shape[0]
    has_s0 = s0 is not None

    def seq_blk(arr_col):
        cb = arr_col[1] // C_W
        return pl.BlockSpec((chunk, C_W), lambda b, c: (b * nc + c, cb))

    in_specs = [seq_blk(cq), seq_blk(cf), seq_blk(cin), seq_blk(cg),
                pl.BlockSpec((depth, C_W), lambda b, c: (0, 0)),
                pl.BlockSpec((None, 1, C_V_DIM), lambda b, c: (layer, 0, 0))]
    args = [cq[0], cf[0], cin[0], cg[0], gamma_lb, g_hgrn]
    if has_s0:
        in_specs.append(pl.BlockSpec((None, None, C_HEADS, C_K_DIM, C_V_DIM), lambda b, c: (layer, b, 0, 0, 0)))
        args.append(s0)
    return pl.pallas_call(
        functools.partial(_hgrn_body, chunk=chunk, layer=layer, has_s0=has_s0),
        grid=(bsz, nc),
        in_specs=in_specs,
        out_specs=[pl.BlockSpec((chunk, C_W), lambda b, c: (b * nc + c, 0)),
                   pl.BlockSpec((1, C_HEADS, C_K_DIM, C_V_DIM), lambda b, c: (b, 0, 0, 0))],
        out_shape=[jax.ShapeDtypeStruct((bsz * t, C_W), BF16),
                   jax.ShapeDtypeStruct((bsz, C_HEADS, C_K_DIM, C_V_DIM), F32)],
        scratch_shapes=[pltpu.VMEM((C_HEADS, C_V_DIM, C_K_DIM), F32)],
        compiler_params=_cparams(("parallel", "arbitrary")),
        name="hgrn2",
    )(*args)


def _outproj_body(*refs, emit):
    h_ref, oa_ref, ob_ref, oc_ref, wa_ref, wb_ref, wc_ref, o_ref = refs[:8]
    wa, wb, wc = wa_ref[...].astype(BF16), wb_ref[...].astype(BF16), wc_ref[...].astype(BF16)
    if emit:
        refs[8][...] = wa
        refs[9][...] = wb
        refs[10][...] = wc
    o_ref[...] = h_ref[...] + (_dot(oa_ref[...], wa) + _dot(ob_ref[...], wb) + _dot(oc_ref[...], wc))


def _outproj(h, oa, ob, oc, w):
    n, d = h.shape
    emit = w[0] == "f32"
    tm = _row_tile(n, 512)
    assert not emit or n == tm, "the weight-emitting variant expects a single row tile"
    row = lambda wd: pl.BlockSpec((tm, wd), lambda i: (i, 0))
    whole = lambda r: pl.BlockSpec((r, d), lambda i: (0, 0))
    seg_rows = (A_Q_W, B_V_W, C_W)
    seg_first = (0, A_Q_W, A_Q_W + B_V_W)
    if emit:
        _, w_out, layer = w
        w_specs = [pl.BlockSpec((None, r, d), functools.partial(lambda i, blk: (layer, blk, 0), blk=f // r))
                   for r, f in zip(seg_rows, seg_first)]
        w_args = [w_out] * 3
    else:
        w_specs = [whole(r) for r in seg_rows]
        w_args = list(w[1:])
    out_specs = [row(d)]
    out_shape = [jax.ShapeDtypeStruct((n, d), F32)]
    if emit:
        out_specs += [whole(r) for r in seg_rows]
        out_shape += [jax.ShapeDtypeStruct((r, d), BF16) for r in seg_rows]
    res = pl.pallas_call(
        functools.partial(_outproj_body, emit=emit),
        grid=(n // tm,),
        in_specs=[row(d), row(A_Q_W), row(B_V_W), row(C_W)] + w_specs,
        out_specs=out_specs,
        out_shape=out_shape,
        compiler_params=_cparams(("parallel",)),
        name="out_proj_emit" if emit else "out_proj",
    )(h, oa, ob, oc, *w_args)
    return (res[0], ("bf16", res[1], res[2], res[3])) if emit else (res[0], w)


def _layer(x, layer, w, p, caches, past, geom):
    bsz, t = geom["bsz"], geom["t"]
    depth = p["g_ffn1"].shape[0]
    wb = {}
    h, wb["ffn1"] = _ffn(x, p["g_ffn1"][layer], w["ffn1"])
    main, small, c_f, caches, wb["proj"] = _proj(h, p["g_mix"][layer], w["proj"], caches, layer, depth)
    if past is None:
        past_a = past_b = s0 = None
    else:
        past_a, past_b, s0 = past[0:3], past[3:5], past[5]
    o_a = _dsa((main, COL_A_Q), (main, COL_A_QI), (small, 0), (main, COL_A_K), (main, COL_A_V), (small, 0), past_a,
               layer=layer, **geom)
    o_b = _diff((main, COL_B_Q), (main, COL_B_K), (main, COL_B_V), p["diff_lambda"], p["g_diff"], past_b,
                layer=layer, **geom)
    o_c, s_c = _hgrn((main, COL_C_Q), (c_f, 0), (main, COL_C_I), (main, COL_C_G),
                     p["gamma_lb"], p["g_hgrn"], s0, bsz=bsz, t=t, layer=layer)
    h, wb["out"] = _outproj(h, o_a, o_b, o_c, w["out"])
    x, wb["ffn2"] = _ffn(h, p["g_ffn2"][layer], w["ffn2"], p["g_final"] if layer == depth - 1 else None)
    return x, caches, s_c, wb


def kernel(x_prompt, x_sample, cache_a_k, cache_a_v, cache_a_kidx, cache_b_k, cache_b_v, state_c,
           g_ffn1, w_ffn1_in, w_ffn1_out, g_mix, w_in, diff_lambda, g_diff, gamma_lb, g_hgrn, w_out,
           g_ffn2, w_ffn2_in, w_ffn2_out, g_final):
    depth, d = w_in.shape[0], w_in.shape[1]
    bp, tp, _ = x_prompt.shape
    bs, ts, _ = x_sample.shape
    p_len = cache_a_k.shape[2]
    p = dict(g_ffn1=g_ffn1, g_mix=g_mix, diff_lambda=diff_lambda, g_diff=g_diff.reshape(depth, 1, B_V_DIM),
             gamma_lb=gamma_lb, g_hgrn=g_hgrn.reshape(depth, 1, C_V_DIM), g_ffn2=g_ffn2, g_final=g_final)
    geom_p = dict(bsz=bp, t=tp, s_pad=tp, n_keys=tp, q_pos0=0, tq=_row_tile(tp, ATTN_QUERY_TILE))
    n_keys_s = p_len + ts
    geom_s = dict(bsz=bs, t=ts, s_pad=-(-n_keys_s // V7X_LANES) * V7X_LANES, n_keys=n_keys_s, q_pos0=p_len, tq=ts)
    past = (cache_a_k.reshape(depth, bs, p_len * A_KV_HEADS, A_HEAD_DIM),
            cache_a_v.reshape(depth, bs, p_len * A_KV_HEADS, A_HEAD_DIM),
            cache_a_kidx,
            cache_b_k.reshape(depth, bs, p_len * B_HEADS, 2 * B_QK_DIM),
            cache_b_v.reshape(depth, bs, p_len * B_HEADS, B_V_DIM),
            state_c)

    w_in_t = jnp.swapaxes(w_in, 1, 2)
    xp = x_prompt.reshape(bp * tp, d)
    xs = x_sample.reshape(bs * ts, d)
    caches_p = caches_s = None
    states_p, states_s = [], []
    for layer in range(depth):
        w_f32 = dict(ffn1=("f32", w_ffn1_in, w_ffn1_out, layer), proj=("f32", w_in_t), out=("f32", w_out, layer),
                     ffn2=("f32", w_ffn2_in, w_ffn2_out, layer))
        xs, caches_s, s_c, w_bf = _layer(xs, layer, w_f32, p, caches_s, past, geom_s)
        states_s.append(s_c)
        xp, caches_p, s_c, _ = _layer(xp, layer, w_bf, p, caches_p, None, geom_p)
        states_p.append(s_c)

    def group_out(x, caches, states, bsz, t):
        a_k, a_v, a_ki, b_k, b_v = caches
        return (x.reshape(bsz, t, d),
                a_k.reshape(depth, bsz, t, A_KV_HEADS, A_HEAD_DIM), a_v.reshape(depth, bsz, t, A_KV_HEADS, A_HEAD_DIM),
                a_ki.reshape(depth, bsz, t, IDX_DIM), b_k.reshape(depth, bsz, t, B_HEADS, 2 * B_QK_DIM),
                b_v.reshape(depth, bsz, t, B_HEADS, B_V_DIM), jnp.stack(states))

    out_p = group_out(xp, caches_p, states_p, bp, tp)
    out_s = group_out(xs, caches_s, states_s, bs, ts)
    return (out_p[0], out_s[0], *out_p[1:], *out_s[1:])
```

```python
import functools
import math

import jax
import jax.numpy as jnp
from jax import lax
from jax.experimental import pallas as pl
from jax.experimental.pallas import tpu as pltpu

F32 = jnp.float32
BF16 = jnp.bfloat16

CHUNK = 64
A_HEADS, A_KV_HEADS, A_HEAD_DIM = 4, 2, 128
IDX_HEADS, IDX_DIM = 8, 64
TOPK_MAX = 256
B_HEADS, B_QK_DIM, B_V_DIM = 4, 64, 128
C_HEADS, C_K_DIM, C_V_DIM = 8, 128, 128
IDX_SCALE = IDX_DIM ** -0.5
DIFF_SCALE = B_QK_DIM ** -0.5
assert math.frexp(IDX_SCALE)[0] == 0.5 and math.frexp(DIFF_SCALE)[0] == 0.5
NORM_EPS = 1e-6
NEG_BIG = -1e30
LB_FLOOR = 1e-30

A_Q_W = A_HEADS * A_HEAD_DIM
A_KV_W = A_KV_HEADS * A_HEAD_DIM
A_QI_W = IDX_HEADS * IDX_DIM
B_QK_W = B_HEADS * 2 * B_QK_DIM
B_V_W = B_HEADS * B_V_DIM
C_W = C_HEADS * C_K_DIM
MIX_W = A_Q_W + B_V_W + C_W
MAIN_W = A_Q_W + 2 * A_KV_W + A_QI_W + 2 * B_QK_W + B_V_W + 4 * C_W
NARROW_W = IDX_DIM + IDX_HEADS
SMALL_W = 128
PROJ_TN = 1792
ATTN_QUERY_TILE = 256
DSA_KEY_CHUNK = 256
DIFF_KEY_CHUNK = 512
COUNT_ROWS = 64
DIFF_HEADS_PER_LOOP = 4
DSA_SEQS_PER_STEP = 8
DIFF_SEQS_PER_STEP = 2
HGRN_CHUNK = 256
COL_A_Q, COL_A_K, COL_A_V, COL_A_QI = 0, 512, 768, 1024
COL_B_Q, COL_B_K, COL_B_V = 1536, 2048, 2560
COL_C_Q, COL_C_F, COL_C_I, COL_C_G = 3072, 4096, 5120, 6144

V7X_LANES = 128
V7X_VMEM_LIMIT = 56 * 1024 * 1024


def _cparams(semantics):
    return pltpu.CompilerParams(dimension_semantics=semantics, vmem_limit_bytes=V7X_VMEM_LIMIT)


def _rmsnorm(x, g):
    return x * lax.rsqrt(jnp.mean(x * x, axis=-1, keepdims=True) + NORM_EPS) * g


def _dot(a, b):
    return jnp.dot(a, b, preferred_element_type=F32)


def _dot_nt(a, b):
    return lax.dot_general(a, b, (((1,), (1,)), ((), ())), preferred_element_type=F32)


def _dot_tn(a, b):
    return lax.dot_general(a, b, (((0,), (0,)), ((), ())), preferred_element_type=F32)


def _row_tile(n, pref):
    t = min(n, pref)
    while n % t:
        t //= 2
    return t


def _ffn_body(*refs, n_ff_steps, final_norm, emit):
    x_ref, g_ref, wg_ref, wu_ref, wo_ref = refs[:5]
    gf_ref = refs[5] if final_norm else None
    outs = refs[5 + final_norm:]
    o_ref, xn_ref = outs[0], outs[-1]
    j = pl.program_id(1)

    @pl.when(j == 0)
    def _():
        x = x_ref[...]
        xn_ref[...] = _rmsnorm(x, g_ref[...]).astype(BF16)
        o_ref[...] = x

    wg, wu, wo = wg_ref[...].astype(BF16), wu_ref[...].astype(BF16), wo_ref[...].astype(BF16)
    if emit:
        outs[1][...] = wg
        outs[2][...] = wu
        outs[3][...] = wo
    xn = xn_ref[...]
    gate = _dot(xn, wg)
    up = _dot(xn, wu)
    act = (gate * jax.nn.sigmoid(gate)) * up * 0.5
    o_ref[...] += _dot(act.astype(BF16), wo)

    if final_norm:
        @pl.when(j == n_ff_steps - 1)
        def _():
            o_ref[...] = _rmsnorm(o_ref[...], gf_ref[...])


def _ffn(x, g, w, g_final=None):
    n, d = x.shape
    emit = w[0] == "f32"
    if emit:
        _, w_in, w_out, layer = w
        f = w_out.shape[1]
    else:
        _, wg, wu, wo = w
        f = wo.shape[0]
    tm = _row_tile(n, 512)
    tf = 512 if f % 512 == 0 else f
    nj = f // tf
    assert not emit or n == tm, "the weight-emitting variant expects a single row tile"
    final_norm = g_final is not None
    in_specs = [pl.BlockSpec((tm, d), lambda i, j: (i, 0)), pl.BlockSpec((1, d), lambda i, j: (0, 0))]
    if emit:
        in_specs += [pl.BlockSpec((None, d, tf), lambda i, j: (layer, 0, j)),
                     pl.BlockSpec((None, d, tf), lambda i, j: (layer, 0, j + nj)),
                     pl.BlockSpec((None, tf, d), lambda i, j: (layer, j, 0))]
        args = [x, g.reshape(1, d), w_in, w_in, w_out]
    else:
        in_specs += [pl.BlockSpec((d, tf), lambda i, j: (0, j)), pl.BlockSpec((d, tf), lambda i, j: (0, j)),
                     pl.BlockSpec((tf, d), lambda i, j: (j, 0))]
        args = [x, g.reshape(1, d), wg, wu, wo]
    if final_norm:
        in_specs.append(pl.BlockSpec((1, d), lambda i, j: (0, 0)))
        args.append(g_final.reshape(1, d))
    out_specs = [pl.BlockSpec((tm, d), lambda i, j: (i, 0))]
    out_shape = [jax.ShapeDtypeStruct((n, d), F32)]
    if emit:
        out_specs += [pl.BlockSpec((d, tf), lambda i, j: (0, j)), pl.BlockSpec((d, tf), lambda i, j: (0, j)),
                      pl.BlockSpec((tf, d), lambda i, j: (j, 0))]
        out_shape += [jax.ShapeDtypeStruct((d, f), BF16), jax.ShapeDtypeStruct((d, f), BF16),
                      jax.ShapeDtypeStruct((f, d), BF16)]
    res = pl.pallas_call(
        functools.partial(_ffn_body, n_ff_steps=nj, final_norm=final_norm, emit=emit),
        grid=(n // tm, nj),
        in_specs=in_specs,
        out_specs=out_specs,
        out_shape=out_shape,
        scratch_shapes=[pltpu.VMEM((tm, d), BF16)],
        compiler_params=_cparams(("parallel", "arbitrary")),
        name="ffn_emit" if emit else "ffn",
    )(*args)
    return (res[0], ("bf16", res[1], res[2], res[3])) if emit else (res[0], w)


def _proj_body(*refs, emit, tn, tm, n_alias, layer):
    n_w = 3 if emit else 2
    x_ref, g_ref = refs[:2]
    w_refs = refs[2:2 + n_w]
    outs = refs[2 + n_w + n_alias:-1]
    om_ref, os_ref, ak_ref, av_ref, aki_ref, bk_ref, bv_ref = outs[:7]
    cf_ref = outs[-1]
    xn_ref = refs[-1]
    j = pl.program_id(1)

    def this_layer(ref):
        if n_alias:
            return ref
        for other in range(ref.shape[0]):
            if other != layer:
                ref[other] = jnp.zeros(ref.shape[1:], F32)
        return ref.at[layer]

    if emit:
        wa_ref, wx_ref, wsm_ref = w_refs
        wmo_ref, wso_ref = outs[7:9]
        n_aligned = COL_B_Q // tn

        @pl.when(j < n_aligned)
        def _():
            wmo_ref[...] = wa_ref[...].astype(BF16)

        @pl.when(j >= n_aligned)
        def _():
            tall = jnp.concatenate([wa_ref[...], wx_ref[...]], axis=0)
            wmo_ref[...] = tall[NARROW_W:NARROW_W + tn, :].astype(BF16)

        wm = wmo_ref[...]
    else:
        wm = w_refs[0][...]

    @pl.when(j == 0)
    def _():
        xn = _rmsnorm(x_ref[...], g_ref[...]).astype(BF16)
        xn_ref[...] = xn
        if emit:
            out_col = lax.broadcasted_iota(jnp.int32, wsm_ref.shape, 0)
            ws = jnp.where(out_col < NARROW_W, wsm_ref[...], 0.0).astype(BF16)
            wso_ref[...] = ws
        else:
            ws = w_refs[1][...]
        small = _dot_nt(xn, ws)
        os_ref[...] = small
        this_layer(aki_ref)[...] = small[:, 0:IDX_DIM]

    acc = _dot_nt(xn_ref[...], wm)
    om_ref[...] = acc.astype(BF16)

    if tn <= C_W:
        @pl.when((j >= COL_C_F // tn) & (j < COL_C_I // tn))
        def _():
            cf_ref[...] = acc
    else:
        @pl.when(j == COL_C_F // tn)
        def _():
            cf_ref[...] = acc[:, COL_C_F % tn:COL_C_F % tn + C_W]

    def put_heads(ref, col, heads):
        off = col % tn
        assert off + heads * V7X_LANES <= tn

        @pl.when(j == col // tn)
        def _():
            dst = this_layer(ref)
            for hd in range(heads):
                dst[pl.ds(hd, tm, stride=heads), :] = acc[:, off + hd * V7X_LANES:off + (hd + 1) * V7X_LANES]

    put_heads(ak_ref, COL_A_K, A_KV_HEADS)
    put_heads(av_ref, COL_A_V, A_KV_HEADS)
    put_heads(bk_ref, COL_B_K, B_HEADS)
    put_heads(bv_ref, COL_B_V, B_HEADS)


def _proj(x, g, w, caches, layer, depth):
    n, d = x.shape
    emit = w[0] == "f32"
    tm = _row_tile(n, 512)
    tn = 512 if emit else PROJ_TN
    assert not emit or n == tm, "the weight-emitting variant expects a single row tile"
    lanes_per_tn = tn // V7X_LANES
    if emit:
        assert COL_B_Q % tn == 0 and NARROW_W % 8 == 0
        w_in_t = w[1]
        w_specs = [pl.BlockSpec((None, tn, d), lambda i, j: (layer, j, 0)),
                   pl.BlockSpec((None, V7X_LANES, d), lambda i, j: (layer, lanes_per_tn * (j + 1), 0)),
                   pl.BlockSpec((None, SMALL_W, d), lambda i, j: (layer, COL_B_Q // SMALL_W, 0))]
        w_args = [w_in_t, w_in_t, w_in_t]
    else:
        w_specs = [pl.BlockSpec((tn, d), lambda i, j: (j, 0)), pl.BlockSpec((SMALL_W, d), lambda i, j: (0, 0))]
        w_args = list(w[1:])
    cache_rows = (A_KV_HEADS, A_KV_HEADS, 1, B_HEADS, B_HEADS)
    cache_lanes = (V7X_LANES, V7X_LANES, IDX_DIM, V7X_LANES, V7X_LANES)
    alias_args = [] if caches is None else list(caches)
    out_specs = [pl.BlockSpec((tm, tn), lambda i, j: (i, j)), pl.BlockSpec((tm, SMALL_W), lambda i, j: (i, 0))]
    if alias_args:
        cache_spec = lambda r, wd: pl.BlockSpec((None, r * tm, wd), lambda i, j: (layer, i, 0))
    else:
        cache_spec = lambda r, wd: pl.BlockSpec((depth, r * tm, wd), lambda i, j: (0, i, 0))
    out_specs += [cache_spec(r, wd) for r, wd in zip(cache_rows, cache_lanes)]
    out_shape = [jax.ShapeDtypeStruct((n, MAIN_W), BF16), jax.ShapeDtypeStruct((n, SMALL_W), F32)]
    out_shape += [jax.ShapeDtypeStruct((depth, r * n, wd), F32) for r, wd in zip(cache_rows, cache_lanes)]
    if emit:
        out_specs += [pl.BlockSpec((tn, d), lambda i, j: (j, 0)), pl.BlockSpec((SMALL_W, d), lambda i, j: (0, 0))]
        out_shape += [jax.ShapeDtypeStruct((MAIN_W, d), BF16), jax.ShapeDtypeStruct((SMALL_W, d), BF16)]
    if tn <= C_W:
        assert COL_C_F % tn == 0 and C_W % tn == 0
        cf_first, cf_tiles = COL_C_F // tn, C_W // tn
        out_specs.append(pl.BlockSpec((tm, tn), lambda i, j: (i, jnp.clip(j - cf_first, 0, cf_tiles - 1))))
    else:
        assert COL_C_F % tn + C_W <= tn
        out_specs.append(pl.BlockSpec((tm, C_W), lambda i, j: (i, 0)))
    out_shape.append(jax.ShapeDtypeStruct((n, C_W), F32))
    n_in = 2 + len(w_args)
    res = pl.pallas_call(
        functools.partial(_proj_body, emit=emit, tn=tn, tm=tm, n_alias=len(alias_args), layer=layer),
        grid=(n // tm, MAIN_W // tn),
        in_specs=[pl.BlockSpec((tm, d), lambda i, j: (i, 0)), pl.BlockSpec((1, d), lambda i, j: (0, 0))]
        + w_specs + [pl.BlockSpec(memory_space=pl.ANY)] * len(alias_args),
        out_specs=out_specs,
        out_shape=out_shape,
        input_output_aliases={n_in + k: 2 + k for k in range(len(alias_args))},
        scratch_shapes=[pltpu.VMEM((tm, d), BF16)],
        compiler_params=_cparams(("parallel", "arbitrary")),
        name="in_proj_emit" if emit else "in_proj",
    )(x, g.reshape(1, d), *w_args, *alias_args)
    w_bf = ("bf16", res[7], res[8]) if emit else w
    return res[0], res[1], res[-1], tuple(res[2:7]), w_bf


def _sortable_key(x):
    bits = lax.bitcast_convert_type(x, jnp.int32)
    return jnp.where(bits < 0, bits ^ jnp.int32(0x7FFFFFFF), bits)


def _kth_largest(key, k):
    rows = key.shape[0]

    def body(it, t):
        cand = t + lax.shift_left(jnp.int32(1), jnp.int32(31) - it)
        cnt = jnp.sum(jnp.where(key >= cand, 1.0, 0.0), axis=1, keepdims=True)
        return jnp.where(cnt >= k, cand, t)

    t0 = jnp.full((rows, 1), jnp.iinfo(jnp.int32).min, jnp.int32)
    return lax.fori_loop(0, 32, body, t0)


def _tie_cut(tie, kpos, need, nbits):
    rows = tie.shape[0]

    def body(it, t):
        cand = t + lax.shift_left(jnp.int32(1), jnp.int32(nbits - 1) - it)
        cnt = jnp.sum(jnp.where(tie & (kpos < cand), 1.0, 0.0), axis=1, keepdims=True)
        return jnp.where(cnt < need, cand, t)

    return lax.fori_loop(0, nbits, body, jnp.zeros((rows, 1), jnp.int32))


def _visible(tq, s_pad, q_pos0, n_keys):
    qpos = q_pos0 + pl.program_id(1) * tq + lax.broadcasted_iota(jnp.int32, (tq, 1), 0)
    limit = jnp.minimum(((qpos >> 6) + 1) << 6, n_keys)
    kpos = lax.broadcasted_iota(jnp.int32, (tq, s_pad), 1)
    return kpos < limit, kpos


def _head_keys(new_ref, rows, lo, width, past_ref, head, heads, s_pad):
    new = new_ref[rows, lo:lo + width]
    p_len = past_ref.shape[0] // heads
    hist = past_ref[...] if heads == 1 else past_ref[pl.ds(head, p_len, stride=heads), :]
    parts = [hist.astype(BF16), new.astype(BF16)]
    pad = s_pad - p_len - new.shape[0]
    if pad:
        parts.append(jnp.zeros((pad, width), BF16))
    return jnp.concatenate(parts, axis=0)


def _stack_rows(pieces):
    return pieces[0] if len(pieces) == 1 else jnp.concatenate(pieces, axis=0)


def _dsa_hist_body(q_ref, qi_ref, wq_ref, k_ref, v_ref, ki_ref, pk_ref, pv_ref, pki_ref, o_ref,
                   *, nb, tq, s_pad, n_keys, q_pos0, topk):
    vis1, kpos1 = _visible(tq, s_pad, q_pos0, n_keys)
    rep = A_HEADS // A_KV_HEADS
    scores = []
    for bb in range(nb):
        rows = slice(bb * tq, (bb + 1) * tq)
        kidx = _head_keys(ki_ref, rows, 0, IDX_DIM, pki_ref.at[bb], 0, 1, s_pad)
        qi_all = _stack_rows([qi_ref[rows, h * IDX_DIM:(h + 1) * IDX_DIM] for h in range(IDX_HEADS)]).astype(BF16)
        wq = wq_ref[rows, IDX_DIM:IDX_DIM + IDX_HEADS] * (IDX_HEADS ** -0.5) * IDX_SCALE
        w_col = _stack_rows([wq[:, h:h + 1] for h in range(IDX_HEADS)])
        x = jnp.maximum(_dot_nt(qi_all, kidx), 0.0) * w_col
        score = jnp.sum(x.reshape(IDX_HEADS, tq, s_pad), axis=0)
        scores.append(jnp.where(vis1, score + 0.0, NEG_BIG))

    key = _sortable_key(_stack_rows(scores))
    kpos = _stack_rows([kpos1] * nb)
    thr = _kth_largest(key, float(topk))
    n_gt = jnp.sum(jnp.where(key > thr, 1.0, 0.0), axis=1, keepdims=True)
    tie = key == thr
    cut = _tie_cut(tie, kpos, float(topk) - n_gt, max(1, (s_pad - 1).bit_length()))
    attend = jnp.where((key > thr) | (tie & (kpos <= cut)), 1.0, 0.0)

    for bb in range(nb):
        rows = slice(bb * tq, (bb + 1) * tq)
        sel = _stack_rows([jnp.where(vis1, attend[rows], 0.0)] * rep) > 0.5
        for g in range(A_KV_HEADS):
            k_g = _head_keys(k_ref, rows, g * A_HEAD_DIM, A_HEAD_DIM, pk_ref.at[bb], g, A_KV_HEADS, s_pad)
            v_g = _head_keys(v_ref, rows, g * A_HEAD_DIM, A_HEAD_DIM, pv_ref.at[bb], g, A_KV_HEADS, s_pad)
            heads = range(g * rep, (g + 1) * rep)
            q_g = _stack_rows([q_ref[rows, h * A_HEAD_DIM:(h + 1) * A_HEAD_DIM] for h in heads]).astype(BF16)
            att = jnp.where(sel, _dot_nt(q_g, k_g) * (A_HEAD_DIM ** -0.5), NEG_BIG)
            p = jnp.exp(att - jnp.max(att, axis=1, keepdims=True))
            l = jnp.sum(p, axis=1, keepdims=True)
            o = _dot(p.astype(BF16), v_g) / l
            for r, h in enumerate(heads):
                o_ref[rows, h * A_HEAD_DIM:(h + 1) * A_HEAD_DIM] = o[r * tq:(r + 1) * tq].astype(o_ref.dtype)


def _softmax_step_t(carry, s, v_c):
    m, l, acc = carry
    m_new = jnp.maximum(m, _col_reduce(s, jnp.max))
    alpha = jnp.exp(m - m_new)
    p = jnp.exp(s - m_new)
    return m_new, alpha * l + _col_reduce(p, jnp.sum), alpha * acc + _dot_tn(v_c, p.astype(BF16))


def _col_reduce(x, op):
    rows, tq = x.shape
    return op(op(x.reshape(rows // COUNT_ROWS, COUNT_ROWS, tq), axis=0), axis=0, keepdims=True)


def _softmax_init_t(tq, width):
    return (jnp.full((1, tq), -jnp.inf, F32), jnp.zeros((1, tq), F32), jnp.zeros((width, tq), F32))


def _causal_extent(tq, kc):
    i = pl.program_id(1)
    qpos = i * tq + lax.broadcasted_iota(jnp.int32, (1, tq), 1)
    limit = ((qpos >> 6) + 1) << 6
    return (i * tq + tq + kc - 1) // kc, limit, lax.broadcasted_iota(jnp.int32, (kc, 1), 0)


def _dsa_causal_body(q_ref, qi_ref, wq_ref, k_ref, v_ref, ki_ref, o_ref, key_ref, cut_ref, *, tq, kc, topk, nbits):
    n_chunks, limit, krow = _causal_extent(tq, kc)
    wq_t = wq_ref[...].T[IDX_DIM:IDX_DIM + IDX_HEADS, :] * (IDX_HEADS ** -0.5) * IDX_SCALE
    qi_all = jnp.concatenate([qi_ref[:, h * IDX_DIM:(h + 1) * IDX_DIM] for h in range(IDX_HEADS)], axis=0).astype(BF16)

    def score_chunk(c, carry):
        start = pl.multiple_of(c * kc, kc)
        kidx = ki_ref[pl.ds(start, kc), 0:IDX_DIM].astype(BF16)
        logits = _dot_nt(kidx, qi_all)
        acc = jnp.zeros((kc, tq), F32)
        for h in range(IDX_HEADS):
            acc = acc + jnp.maximum(logits[:, h * tq:(h + 1) * tq], 0.0) * wq_t[h:h + 1, :]
        score = jnp.where(start + krow < limit, acc + 0.0, NEG_BIG)
        key_ref[c] = _sortable_key(score)
        return carry

    lax.fori_loop(0, n_chunks, score_chunk, 0)

    def count(pred):
        def body(c, part):
            ones = jnp.where(pred(key_ref[c], c), 1.0, 0.0)
            return part + jnp.sum(ones.reshape(kc // COUNT_ROWS, COUNT_ROWS, tq), axis=0)

        part = lax.fori_loop(0, n_chunks, body, jnp.zeros((COUNT_ROWS, tq), F32))
        return jnp.sum(part, axis=0, keepdims=True)

    def thr_step(it, t):
        cand = t + lax.shift_left(jnp.int32(1), jnp.int32(31) - it)
        return jnp.where(count(lambda key, c: key >= cand) >= float(topk), cand, t)

    thr = lax.fori_loop(0, 32, thr_step, jnp.full((1, tq), jnp.iinfo(jnp.int32).min, jnp.int32))
    need = float(topk) - count(lambda key, c: key > thr)
    n_tie = count(lambda key, c: key == thr)

    cut_ref[...] = jnp.full(cut_ref.shape, jnp.iinfo(jnp.int32).max, jnp.int32)

    @pl.when(jnp.max(jnp.where(n_tie != need, 1.0, 0.0)) > 0.0)
    def _():
        def cut_step(it, t):
            cand = t + lax.shift_left(jnp.int32(1), jnp.int32(nbits - 1) - it)
            below = count(lambda key, c: (key == thr) & (c * kc + krow < cand))
            return jnp.where(below < need, cand, t)

        cut = lax.fori_loop(0, nbits, cut_step, jnp.zeros((1, tq), jnp.int32))
        cut_ref[...] = jnp.broadcast_to(cut, cut_ref.shape)

    cut = cut_ref[0:1, :]

    def select_chunk(c, carry):
        key = key_ref[c]
        kpos = c * kc + krow
        tie_flag = jnp.where(key == thr, jnp.where(kpos <= cut, 0, 1), 1)
        key_ref[c] = jnp.where(kpos < limit, jnp.where(key > thr, 0, tie_flag), 1)
        return carry

    lax.fori_loop(0, n_chunks, select_chunk, 0)

    rep = A_HEADS // A_KV_HEADS
    q_g = [jnp.concatenate([q_ref[:, h * A_HEAD_DIM:(h + 1) * A_HEAD_DIM] for h in range(g * rep, (g + 1) * rep)],
                           axis=0).astype(BF16) for g in range(A_KV_HEADS)]

    def att_chunk(c, carry):
        start = pl.multiple_of(c * kc, kc)
        attend = key_ref[c] == 0
        attend = jnp.concatenate([attend] * rep, axis=1)
        logits = [_dot_nt(k_ref[pl.ds(start, kc), g * A_HEAD_DIM:(g + 1) * A_HEAD_DIM].astype(BF16), q_g[g])
                  for g in range(A_KV_HEADS)]
        v_c = [v_ref[pl.ds(start, kc), g * A_HEAD_DIM:(g + 1) * A_HEAD_DIM].astype(BF16) for g in range(A_KV_HEADS)]
        return tuple(
            _softmax_step_t(carry[g], jnp.where(attend, logits[g] * (A_HEAD_DIM ** -0.5), NEG_BIG), v_c[g])
            for g in range(A_KV_HEADS))

    res = lax.fori_loop(0, n_chunks, att_chunk, tuple(_softmax_init_t(rep * tq, A_HEAD_DIM) for _ in range(A_KV_HEADS)))
    for g in range(A_KV_HEADS):
        _, l, acc = res[g]
        o_t = acc / l
        for r in range(rep):
            h = g * rep + r
            o_ref[:, h * A_HEAD_DIM:(h + 1) * A_HEAD_DIM] = o_t[:, r * tq:(r + 1) * tq].T.astype(o_ref.dtype)


def _diff_causal_body(q_ref, k_ref, v_ref, lam_ref, g_ref, o_ref, *, tq, kc, lam_init):
    n_chunks, limit, krow = _causal_extent(tq, kc)
    lv = lam_ref[...]
    lam = (jnp.exp(jnp.sum(lv[0:1] * lv[1:2], axis=1, keepdims=True))
           - jnp.exp(jnp.sum(lv[2:3] * lv[3:4], axis=1, keepdims=True)) + lam_init)
    gain = g_ref[...] * (1.0 - lam_init)
    head_w = 2 * B_QK_DIM
    qlane = lax.broadcasted_iota(jnp.int32, (tq, head_w), 1)
    for h0 in range(0, B_HEADS, DIFF_HEADS_PER_LOOP):
        heads = range(h0, h0 + DIFF_HEADS_PER_LOOP)
        q_2 = [jnp.concatenate([jnp.where((qlane >= c * B_QK_DIM) & (qlane < (c + 1) * B_QK_DIM),
                                          q_ref[:, h * head_w:(h + 1) * head_w] * DIFF_SCALE, 0.0) for c in range(2)],
                               axis=0).astype(BF16) for h in heads]

        def chunk(c, carry, q_2=q_2, heads=heads):
            start = pl.multiple_of(c * kc, kc)
            vis = start + krow < jnp.concatenate([limit, limit], axis=1)
            logits = [_dot_nt(k_ref[pl.ds(start, kc), h * head_w:(h + 1) * head_w].astype(BF16), q_2[j])
                      for j, h in enumerate(heads)]
            v_c = [v_ref[pl.ds(start, kc), h * B_V_DIM:(h + 1) * B_V_DIM].astype(BF16) for h in heads]
            return tuple(
                _softmax_step_t(carry[j], jnp.where(vis, logits[j], NEG_BIG), v_c[j])
                for j in range(len(heads)))

        res = lax.fori_loop(0, n_chunks, chunk, tuple(_softmax_init_t(2 * tq, B_V_DIM) for _ in heads))
        for j, h in enumerate(heads):
            _, l, acc = res[j]
            o_t = acc / l
            o = (o_t[:, 0:tq] - lam * o_t[:, tq:2 * tq]).T
            o = o * lax.rsqrt(jnp.mean(o * o, axis=1, keepdims=True) + NORM_EPS) * gain
            o_ref[:, h * B_V_DIM:(h + 1) * B_V_DIM] = o.astype(o_ref.dtype)


def _blk(arr_col, rows, width, kind, nq):
    _, col = arr_col
    cb = col // width
    assert col % width == 0
    if kind == "q":
        return pl.BlockSpec((rows, width), lambda b, i: (b * nq + i, cb))
    return pl.BlockSpec((rows, width), lambda b, i: (b, cb))


def _past_blk(cache, layer, nb):
    return pl.BlockSpec((None, nb) + cache.shape[2:], lambda b, i: (layer, b, 0, 0))


def _seqs_per_step(bsz, limit):
    nb = min(bsz, limit)
    while bsz % nb:
        nb -= 1
    return nb


def _dsa(q, qi, wq, k, v, ki, past, *, bsz, t, s_pad, n_keys, q_pos0, tq, layer):
    nq = t // tq
    topk = min(TOPK_MAX, n_keys // 4)
    scratch = []
    if past is None:
        nb = 1
        kc = _row_tile(t, DSA_KEY_CHUNK)
        body = functools.partial(_dsa_causal_body, tq=tq, kc=kc, topk=topk, nbits=max(1, (t - 1).bit_length()))
        scratch = [pltpu.VMEM((t // kc, kc, tq), jnp.int32), pltpu.VMEM((8, tq), jnp.int32)]
    else:
        assert nq == 1
        nb = _seqs_per_step(bsz, DSA_SEQS_PER_STEP)
        body = functools.partial(_dsa_hist_body, nb=nb, tq=tq, s_pad=s_pad, n_keys=n_keys, q_pos0=q_pos0, topk=topk)
    in_specs = [
        _blk(q, nb * tq, A_Q_W, "q", nq),
        _blk(qi, nb * tq, A_QI_W, "q", nq),
        _blk(wq, nb * tq, SMALL_W, "q", nq),
        _blk(k, nb * t, A_KV_W, "kv", nq),
        _blk(v, nb * t, A_KV_W, "kv", nq),
        _blk(ki, nb * t, SMALL_W, "kv", nq),
    ]
    args = [q[0], qi[0], wq[0], k[0], v[0], ki[0]]
    if past is not None:
        in_specs += [_past_blk(c, layer, nb) for c in past]
        args += list(past)
    return pl.pallas_call(
        body,
        grid=(bsz // nb, nq),
        in_specs=in_specs,
        out_specs=pl.BlockSpec((nb * tq, A_Q_W), lambda b, i: (b * nq + i, 0)),
        out_shape=jax.ShapeDtypeStruct((bsz * t, A_Q_W), BF16),
        scratch_shapes=scratch,
        compiler_params=_cparams(("parallel", "parallel")),
        name="dsa",
    )(*args)


def _diff_hist_body(q_ref, k_ref, v_ref, lam_ref, g_ref, pk_ref, pv_ref, o_ref,
                    *, nb, tq, s_pad, n_keys, q_pos0, lam_init):
    vis1, _ = _visible(tq, s_pad, q_pos0, n_keys)
    vis = _stack_rows([jnp.where(vis1, 1.0, 0.0)] * 2) > 0.5
    lv = lam_ref[...]
    lam = (jnp.exp(jnp.sum(lv[0:1] * lv[1:2], axis=1, keepdims=True))
           - jnp.exp(jnp.sum(lv[2:3] * lv[3:4], axis=1, keepdims=True)) + lam_init)
    gain = g_ref[...] * (1.0 - lam_init)
    head_w = 2 * B_QK_DIM
    lane = lax.broadcasted_iota(jnp.int32, (tq, head_w), 1)
    for bb in range(nb):
        rows = slice(bb * tq, (bb + 1) * tq)
        for h in range(B_HEADS):
            k_h = _head_keys(k_ref, rows, h * head_w, head_w, pk_ref.at[bb], h, B_HEADS, s_pad)
            v_h = _head_keys(v_ref, rows, h * B_V_DIM, B_V_DIM, pv_ref.at[bb], h, B_HEADS, s_pad)
            q_h = q_ref[rows, h * head_w:(h + 1) * head_w]
            q_2 = _stack_rows([jnp.where((lane >= c * B_QK_DIM) & (lane < (c + 1) * B_QK_DIM), q_h * DIFF_SCALE, 0.0)
                               for c in range(2)]).astype(BF16)
            logit = jnp.where(vis, _dot_nt(q_2, k_h), NEG_BIG)
            p = jnp.exp(logit - jnp.max(logit, axis=1, keepdims=True))
            l = jnp.sum(p, axis=1, keepdims=True)
            o_2 = _dot(p.astype(BF16), v_h) / l
            o = o_2[0:tq] - lam * o_2[tq:2 * tq]
            o = o * lax.rsqrt(jnp.mean(o * o, axis=1, keepdims=True) + NORM_EPS) * gain
            o_ref[rows, h * B_V_DIM:(h + 1) * B_V_DIM] = o.astype(o_ref.dtype)


def _diff(q, k, v, lam_vecs, g_diff, past, *, bsz, t, s_pad, n_keys, q_pos0, tq, layer):
    nq = t // tq
    lam_init = 0.8 - 0.6 * math.exp(-0.3 * layer)
    if past is None:
        nb = 1
        body = functools.partial(_diff_causal_body, tq=tq, kc=_row_tile(t, DIFF_KEY_CHUNK), lam_init=lam_init)
    else:
        assert nq == 1
        nb = _seqs_per_step(bsz, DIFF_SEQS_PER_STEP)
        body = functools.partial(_diff_hist_body, nb=nb, tq=tq, s_pad=s_pad, n_keys=n_keys, q_pos0=q_pos0,
                                 lam_init=lam_init)
    in_specs = [
        _blk(q, nb * tq, B_QK_W, "q", nq),
        _blk(k, nb * t, B_QK_W, "kv", nq),
        _blk(v, nb * t, B_V_W, "kv", nq),
        pl.BlockSpec((None, 4, B_QK_DIM), lambda b, i: (layer, 0, 0)),
        pl.BlockSpec((None, 1, B_V_DIM), lambda b, i: (layer, 0, 0)),
    ]
    args = [q[0], k[0], v[0], lam_vecs, g_diff]
    if past is not None:
        in_specs += [_past_blk(c, layer, nb) for c in past]
        args += list(past)
    return pl.pallas_call(
        body,
        grid=(bsz // nb, nq),
        in_specs=in_specs,
        out_specs=pl.BlockSpec((nb * tq, B_V_W), lambda b, i: (b * nq + i, 0)),
        out_shape=jax.ShapeDtypeStruct((bsz * t, B_V_W), BF16),
        compiler_params=_cparams(("parallel", "parallel")),
        name="diff_attn",
    )(*args)


def _cumsum_rows(x):
    n = x.shape[0]
    row = lax.broadcasted_iota(jnp.int32, (n, 1), 0)
    d = 1
    while d < n:
        x = x + jnp.where(row >= d, pltpu.roll(x, d, axis=0), 0.0)
        d *= 2
    return x


def _anchor_rows(b, c):
    n, w = b.shape
    if 2 * c >= 8:
        pieces = [jnp.broadcast_to(b[j * 2 * c + c - 1:j * 2 * c + c, :], (2 * c, w)) for j in range(n // (2 * c))]
        return pieces[0] if len(pieces) == 1 else jnp.concatenate(pieces, axis=0)
    rmod = lax.broadcasted_iota(jnp.int32, (n, 1), 0) & (2 * c - 1)
    m = b
    for s in range(-c, c):
        if s != 0:
            m = jnp.where(rmod == c - 1 - s, pltpu.roll(b, (-s) % n, axis=0), m)
    return m


def _hgrn_body(*refs, chunk, layer, has_s0):
    if has_s0:
        q_ref, f_ref, i_ref, g_ref, glb_ref, gn_ref, s0_ref, o_ref, s_ref, st_ref = refs
    else:
        q_ref, f_ref, i_ref, g_ref, glb_ref, gn_ref, o_ref, s_ref, st_ref = refs
    ci = pl.program_id(1)

    @pl.when(ci == 0)
    def _():
        for h in range(C_HEADS):
            st_ref[h] = s0_ref[h].T if has_s0 else jnp.zeros((C_V_DIM, C_K_DIM), F32)

    glb = glb_ref[...]
    pe = jnp.exp(glb - jnp.max(glb, axis=0, keepdims=True))
    pn = pe / jnp.sum(pe, axis=0, keepdims=True)
    lb = jnp.sum(pn[0:layer + 1], axis=0, keepdims=True) - pn[0:1]

    zf = f_ref[...]
    log_sig = jnp.minimum(zf, 0.0) - jnp.log1p(jnp.exp(-jnp.abs(zf)))
    la = jnp.log(jnp.maximum(lb, LB_FLOOR))
    lc = jnp.log1p(-lb) + log_sig
    hi = jnp.maximum(la, lc)
    log_f = hi + jnp.log1p(jnp.exp(jnp.minimum(la, lc) - hi))
    kc = (1.0 - lb) * jax.nn.sigmoid(-zf)
    b_all = _cumsum_rows(log_f)

    row = lax.broadcasted_iota(jnp.int32, (chunk, 1), 0)
    rr = lax.broadcasted_iota(jnp.int32, (chunk, chunk), 0)
    cc = lax.broadcasted_iota(jnp.int32, (chunk, chunk), 1)
    heads = range(C_HEADS)
    head = lambda x, h: x[:, h * C_K_DIM:(h + 1) * C_K_DIM]

    q_all = q_ref[...].astype(F32)
    vb_all = i_ref[...]
    v_all = vb_all.astype(F32)
    b_last = b_all[chunk - 1:chunk, :]
    qe_all = (q_all * jnp.exp(b_all)).astype(BF16)
    kdec_all = (kc * jnp.exp(b_last - b_all)).astype(BF16)
    diag_all = q_all * kc
    levels = []
    c = chunk // 2
    while c >= 1:
        lg = c.bit_length() - 1
        upper = ((row >> lg) & 1) == 1
        d = b_all - _anchor_rows(b_all, c)
        e = jnp.exp(jnp.where(upper, d, -d))
        x = (jnp.where(upper, q_all, kc) * e).astype(BF16)
        pairs = ((rr >> (lg + 1)) == (cc >> (lg + 1))) & (((rr >> lg) & 1) == 1) & (((cc >> lg) & 1) == 0)
        levels.append((x, pairs))
        c //= 2

    st = [st_ref[h] for h in heads]
    o_inter = [_dot_nt(head(qe_all, h), st[h].astype(BF16)) for h in heads]
    a = []
    for h in heads:
        a_h = jnp.zeros((chunk, chunk), F32)
        for x, pairs in levels:
            a_h = jnp.where(pairs, _dot_nt(head(x, h), head(x, h)), a_h)
        a.append(a_h)
    for h in heads:
        st_new = st[h] * jnp.exp(head(b_last, h)) + _dot_tn(head(vb_all, h), head(kdec_all, h))
        st_ref[h] = st_new

        @pl.when(ci == pl.num_programs(1) - 1)
        def _():
            s_ref[0, h] = st_new.T

    gn = gn_ref[...]
    for h in heads:
        o = o_inter[h] + jnp.sum(head(diag_all, h), axis=1, keepdims=True) * head(v_all, h)
        o = o + _dot(a[h].astype(BF16), head(vb_all, h))
        o = o * lax.rsqrt(jnp.mean(o * o, axis=1, keepdims=True) + NORM_EPS) * gn
        zg = head(g_ref, h).astype(F32)
        o_ref[:, h * C_V_DIM:(h + 1) * C_V_DIM] = (o * (zg * jax.nn.sigmoid(zg))).astype(o_ref.dtype)


def _hgrn(cq, cf, cin, cg, gamma_lb, g_hgrn, s0, *, bsz, t, layer):
    chunk = _row_tile(t, HGRN_CHUNK)
    nc = t // chunk
    depth = gamma_lb.shape[0]
    has_s0 = s0 is not None

    def seq_blk(arr_col):
        cb = arr_col[1] // C_W
        return pl.BlockSpec((chunk, C_W), lambda b, c: (b * nc + c, cb))

    in_specs = [seq_blk(cq), seq_blk(cf), seq_blk(cin), seq_blk(cg),
                pl.BlockSpec((depth, C_W), lambda b, c: (0, 0)),
                pl.BlockSpec((None, 1, C_V_DIM), lambda b, c: (layer, 0, 0))]
    args = [cq[0], cf[0], cin[0], cg[0], gamma_lb, g_hgrn]
    if has_s0:
        in_specs.append(pl.BlockSpec((None, None, C_HEADS, C_K_DIM, C_V_DIM), lambda b, c: (layer, b, 0, 0, 0)))
        args.append(s0)
    return pl.pallas_call(
        functools.partial(_hgrn_body, chunk=chunk, layer=layer, has_s0=has_s0),
        grid=(bsz, nc),
        in_specs=in_specs,
        out_specs=[pl.BlockSpec((chunk, C_W), lambda b, c: (b * nc + c, 0)),
                   pl.BlockSpec((1, C_HEADS, C_K_DIM, C_V_DIM), lambda b, c: (b, 0, 0, 0))],
        out_shape=[jax.ShapeDtypeStruct((bsz * t, C_W), BF16),
                   jax.ShapeDtypeStruct((bsz, C_HEADS, C_K_DIM, C_V_DIM), F32)],
        scratch_shapes=[pltpu.VMEM((C_HEADS, C_V_DIM, C_K_DIM), F32)],
        compiler_params=_cparams(("parallel", "arbitrary")),
        name="hgrn2",
    )(*args)


def _outproj_body(*refs, emit):
    h_ref, oa_ref, ob_ref, oc_ref, wa_ref, wb_ref, wc_ref, o_ref = refs[:8]
    wa, wb, wc = wa_ref[...].astype(BF16), wb_ref[...].astype(BF16), wc_ref[...].astype(BF16)
    if emit:
        refs[8][...] = wa
        refs[9][...] = wb
        refs[10][...] = wc
    o_ref[...] = h_ref[...] + (_dot(oa_ref[...], wa) + _dot(ob_ref[...], wb) + _dot(oc_ref[...], wc))


def _outproj(h, oa, ob, oc, w):
    n, d = h.shape
    emit = w[0] == "f32"
    tm = _row_tile(n, 512)
    assert not emit or n == tm, "the weight-emitting variant expects a single row tile"
    row = lambda wd: pl.BlockSpec((tm, wd), lambda i: (i, 0))
    whole = lambda r: pl.BlockSpec((r, d), lambda i: (0, 0))
    seg_rows = (A_Q_W, B_V_W, C_W)
    seg_first = (0, A_Q_W, A_Q_W + B_V_W)
    if emit:
        _, w_out, layer = w
        w_specs = [pl.BlockSpec((None, r, d), functools.partial(lambda i, blk: (layer, blk, 0), blk=f // r))
                   for r, f in zip(seg_rows, seg_first)]
        w_args = [w_out] * 3
    else:
        w_specs = [whole(r) for r in seg_rows]
        w_args = list(w[1:])
    out_specs = [row(d)]
    out_shape = [jax.ShapeDtypeStruct((n, d), F32)]
    if emit:
        out_specs += [whole(r) for r in seg_rows]
        out_shape += [jax.ShapeDtypeStruct((r, d), BF16) for r in seg_rows]
    res = pl.pallas_call(
        functools.partial(_outproj_body, emit=emit),
        grid=(n // tm,),
        in_specs=[row(d), row(A_Q_W), row(B_V_W), row(C_W)] + w_specs,
        out_specs=out_specs,
        out_shape=out_shape,
        compiler_params=_cparams(("parallel",)),
        name="out_proj_emit" if emit else "out_proj",
    )(h, oa, ob, oc, *w_args)
    return (res[0], ("bf16", res[1], res[2], res[3])) if emit else (res[0], w)


def _layer(x, layer, w, p, caches, past, geom):
    bsz, t = geom["bsz"], geom["t"]
    depth = p["g_ffn1"].shape[0]
    wb = {}
    h, wb["ffn1"] = _ffn(x, p["g_ffn1"][layer], w["ffn1"])
    main, small, c_f, caches, wb["proj"] = _proj(h, p["g_mix"][layer], w["proj"], caches, layer, depth)
    if past is None:
        past_a = past_b = s0 = None
    else:
        past_a, past_b, s0 = past[0:3], past[3:5], past[5]
    o_a = _dsa((main, COL_A_Q), (main, COL_A_QI), (small, 0), (main, COL_A_K), (main, COL_A_V), (small, 0), past_a,
               layer=layer, **geom)
    o_b = _diff((main, COL_B_Q), (main, COL_B_K), (main, COL_B_V), p["diff_lambda"], p["g_diff"], past_b,
                layer=layer, **geom)
    o_c, s_c = _hgrn((main, COL_C_Q), (c_f, 0), (main, COL_C_I), (main, COL_C_G),
                     p["gamma_lb"], p["g_hgrn"], s0, bsz=bsz, t=t, layer=layer)
    h, wb["out"] = _outproj(h, o_a, o_b, o_c, w["out"])
    x, wb["ffn2"] = _ffn(h, p["g_ffn2"][layer], w["ffn2"], p["g_final"] if layer == depth - 1 else None)
    return x, caches, s_c, wb


def kernel(x_prompt, x_sample, cache_a_k, cache_a_v, cache_a_kidx, cache_b_k, cache_b_v, state_c,
           g_ffn1, w_ffn1_in, w_ffn1_out, g_mix, w_in, diff_lambda, g_diff, gamma_lb, g_hgrn, w_out,
           g_ffn2, w_ffn2_in, w_ffn2_out, g_final):
    depth, d = w_in.shape[0], w_in.shape[1]
    bp, tp, _ = x_prompt.shape
    bs, ts, _ = x_sample.shape
    p_len = cache_a_k.shape[2]
    p = dict(g_ffn1=g_ffn1, g_mix=g_mix, diff_lambda=diff_lambda, g_diff=g_diff.reshape(depth, 1, B_V_DIM),
             gamma_lb=gamma_lb, g_hgrn=g_hgrn.reshape(depth, 1, C_V_DIM), g_ffn2=g_ffn2, g_final=g_final)
    geom_p = dict(bsz=bp, t=tp, s_pad=tp, n_keys=tp, q_pos0=0, tq=_row_tile(tp, ATTN_QUERY_TILE))
    n_keys_s = p_len + ts
    geom_s = dict(bsz=bs, t=ts, s_pad=-(-n_keys_s // V7X_LANES) * V7X_LANES, n_keys=n_keys_s, q_pos0=p_len, tq=ts)
    past = (cache_a_k.reshape(depth, bs, p_len * A_KV_HEADS, A_HEAD_DIM),
            cache_a_v.reshape(depth, bs, p_len * A_KV_HEADS, A_HEAD_DIM),
            cache_a_kidx,
            cache_b_k.reshape(depth, bs, p_len * B_HEADS, 2 * B_QK_DIM),
            cache_b_v.reshape(depth, bs, p_len * B_HEADS, B_V_DIM),
            state_c)

    w_in_t = jnp.swapaxes(w_in, 1, 2)
    xp = x_prompt.reshape(bp * tp, d)
    xs = x_sample.reshape(bs * ts, d)
    caches_p = caches_s = None
    states_p, states_s = [], []
    for layer in range(depth):
        w_f32 = dict(ffn1=("f32", w_ffn1_in, w_ffn1_out, layer), proj=("f32", w_in_t), out=("f32", w_out, layer),
                     ffn2=("f32", w_ffn2_in, w_ffn2_out, layer))
        xs, caches_s, s_c, w_bf = _layer(xs, layer, w_f32, p, caches_s, past, geom_s)
        states_s.append(s_c)
        xp, caches_p, s_c, _ = _layer(xp, layer, w_bf, p, caches_p, None, geom_p)
        states_p.append(s_c)

    def group_out(x, caches, states, bsz, t):
        a_k, a_v, a_ki, b_k, b_v = caches
        return (x.reshape(bsz, t, d),
                a_k.reshape(depth, bsz, t, A_KV_HEADS, A_HEAD_DIM), a_v.reshape(depth, bsz, t, A_KV_HEADS, A_HEAD_DIM),
                a_ki.reshape(depth, bsz, t, IDX_DIM), b_k.reshape(depth, bsz, t, B_HEADS, 2 * B_QK_DIM),
                b_v.reshape(depth, bsz, t, B_HEADS, B_V_DIM), jnp.stack(states))

    out_p = group_out(xp, caches_p, states_p, bp, tp)
    out_s = group_out(xs, caches_s, states_s, bs, ts)
    return (out_p[0], out_s[0], *out_p[1:], *out_s[1:])
```

```python
import functools
import math

import jax
import jax.numpy as jnp
from jax import lax
from jax.experimental import pallas as pl
from jax.experimental.pallas import tpu as pltpu

F32 = jnp.float32
BF16 = jnp.bfloat16

CHUNK = 64
A_HEADS, A_KV_HEADS, A_HEAD_DIM = 4, 2, 128
IDX_HEADS, IDX_DIM = 8, 64
TOPK_MAX = 256
B_HEADS, B_QK_DIM, B_V_DIM = 4, 64, 128
C_HEADS, C_K_DIM, C_V_DIM = 8, 128, 128
IDX_SCALE = IDX_DIM ** -0.5
DIFF_SCALE = B_QK_DIM ** -0.5
assert math.frexp(IDX_SCALE)[0] == 0.5 and math.frexp(DIFF_SCALE)[0] == 0.5
NORM_EPS = 1e-6
NEG_BIG = -1e30
LB_FLOOR = 1e-30

A_Q_W = A_HEADS * A_HEAD_DIM
A_KV_W = A_KV_HEADS * A_HEAD_DIM
A_QI_W = IDX_HEADS * IDX_DIM
B_QK_W = B_HEADS * 2 * B_QK_DIM
B_V_W = B_HEADS * B_V_DIM
C_W = C_HEADS * C_K_DIM
MIX_W = A_Q_W + B_V_W + C_W
MAIN_W = A_Q_W + 2 * A_KV_W + A_QI_W + 2 * B_QK_W + B_V_W + 4 * C_W
NARROW_W = IDX_DIM + IDX_HEADS
SMALL_W = 128
FFN_ROW_TILE = 1024
PROJ_TN = 1792
ATTN_QUERY_TILE = 256
DSA_KEY_CHUNK = 256
DIFF_KEY_CHUNK = 512
COUNT_ROWS = 64
DIFF_HEADS_PER_LOOP = 4
DSA_SEQS_PER_STEP = 8
DIFF_SEQS_PER_STEP = 2
HGRN_CHUNK = 256
COL_A_Q, COL_A_K, COL_A_V, COL_A_QI = 0, 512, 768, 1024
COL_B_Q, COL_B_K, COL_B_V = 1536, 2048, 2560
COL_C_Q, COL_C_F, COL_C_I, COL_C_G = 3072, 4096, 5120, 6144

V7X_LANES = 128
V7X_VMEM_LIMIT = 56 * 1024 * 1024


def _cparams(semantics):
    return pltpu.CompilerParams(dimension_semantics=semantics, vmem_limit_bytes=V7X_VMEM_LIMIT)


def _rmsnorm(x, g):
    return x * lax.rsqrt(jnp.mean(x * x, axis=-1, keepdims=True) + NORM_EPS) * g


def _dot(a, b):
    return jnp.dot(a, b, preferred_element_type=F32)


def _dot_nt(a, b):
    return lax.dot_general(a, b, (((1,), (1,)), ((), ())), preferred_element_type=F32)


def _dot_tn(a, b):
    return lax.dot_general(a, b, (((0,), (0,)), ((), ())), preferred_element_type=F32)


def _row_tile(n, pref):
    t = min(n, pref)
    while n % t:
        t //= 2
    return t


def _ffn_body(*refs, n_ff_steps, final_norm, emit):
    x_ref, g_ref, wg_ref, wu_ref, wo_ref = refs[:5]
    gf_ref = refs[5] if final_norm else None
    outs = refs[5 + final_norm:]
    o_ref, xn_ref = outs[0], outs[-1]
    j = pl.program_id(1)

    @pl.when(j == 0)
    def _():
        x = x_ref[...]
        xn_ref[...] = _rmsnorm(x, g_ref[...]).astype(BF16)
        o_ref[...] = x

    wg, wu, wo = wg_ref[...].astype(BF16), wu_ref[...].astype(BF16), wo_ref[...].astype(BF16)
    if emit:
        outs[1][...] = wg
        outs[2][...] = wu
        outs[3][...] = wo
    xn = xn_ref[...]
    gate = _dot(xn, wg)
    up = _dot(xn, wu)
    act = (gate * jax.nn.sigmoid(gate)) * up * 0.5
    o_ref[...] += _dot(act.astype(BF16), wo)

    if final_norm:
        @pl.when(j == n_ff_steps - 1)
        def _():
            o_ref[...] = _rmsnorm(o_ref[...], gf_ref[...])


def _ffn(x, g, w, g_final=None):
    n, d = x.shape
    emit = w[0] == "f32"
    if emit:
        _, w_in, w_out, layer = w
        f = w_out.shape[1]
    else:
        _, wg, wu, wo = w
        f = wo.shape[0]
    tm = _row_tile(n, FFN_ROW_TILE)
    tf = 512 if f % 512 == 0 else f
    nj = f // tf
    assert not emit or n == tm, "the weight-emitting variant expects a single row tile"
    final_norm = g_final is not None
    in_specs = [pl.BlockSpec((tm, d), lambda i, j: (i, 0)), pl.BlockSpec((1, d), lambda i, j: (0, 0))]
    if emit:
        in_specs += [pl.BlockSpec((None, d, tf), lambda i, j: (layer, 0, j)),
                     pl.BlockSpec((None, d, tf), lambda i, j: (layer, 0, j + nj)),
                     pl.BlockSpec((None, tf, d), lambda i, j: (layer, j, 0))]
        args = [x, g.reshape(1, d), w_in, w_in, w_out]
    else:
        in_specs += [pl.BlockSpec((d, tf), lambda i, j: (0, j)), pl.BlockSpec((d, tf), lambda i, j: (0, j)),
                     pl.BlockSpec((tf, d), lambda i, j: (j, 0))]
        args = [x, g.reshape(1, d), wg, wu, wo]
    if final_norm:
        in_specs.append(pl.BlockSpec((1, d), lambda i, j: (0, 0)))
        args.append(g_final.reshape(1, d))
    out_specs = [pl.BlockSpec((tm, d), lambda i, j: (i, 0))]
    out_shape = [jax.ShapeDtypeStruct((n, d), F32)]
    if emit:
        out_specs += [pl.BlockSpec((d, tf), lambda i, j: (0, j)), pl.BlockSpec((d, tf), lambda i, j: (0, j)),
                      pl.BlockSpec((tf, d), lambda i, j: (j, 0))]
        out_shape += [jax.ShapeDtypeStruct((d, f), BF16), jax.ShapeDtypeStruct((d, f), BF16),
                      jax.ShapeDtypeStruct((f, d), BF16)]
    res = pl.pallas_call(
        functools.partial(_ffn_body, n_ff_steps=nj, final_norm=final_norm, emit=emit),
        grid=(n // tm, nj),
        in_specs=in_specs,
        out_specs=out_specs,
        out_shape=out_shape,
        scratch_shapes=[pltpu.VMEM((tm, d), BF16)],
        compiler_params=_cparams(("parallel", "arbitrary")),
        name="ffn_emit" if emit else "ffn",
    )(*args)
    return (res[0], ("bf16", res[1], res[2], res[3])) if emit else (res[0], w)


def _proj_body(*refs, emit, tn, tm, n_alias, layer):
    n_w = 3 if emit else 2
    x_ref, g_ref = refs[:2]
    w_refs = refs[2:2 + n_w]
    outs = refs[2 + n_w + n_alias:-1]
    om_ref, os_ref, ak_ref, av_ref, aki_ref, bk_ref, bv_ref = outs[:7]
    cf_ref = outs[-1]
    xn_ref = refs[-1]
    j = pl.program_id(1)

    def this_layer(ref):
        if n_alias:
            return ref
        for other in range(ref.shape[0]):
            if other != layer:
                ref[other] = jnp.zeros(ref.shape[1:], F32)
        return ref.at[layer]

    if emit:
        wa_ref, wx_ref, wsm_ref = w_refs
        wmo_ref, wso_ref = outs[7:9]
        n_aligned = COL_B_Q // tn

        @pl.when(j < n_aligned)
        def _():
            wmo_ref[...] = wa_ref[...].astype(BF16)

        @pl.when(j >= n_aligned)
        def _():
            tall = jnp.concatenate([wa_ref[...], wx_ref[...]], axis=0)
            wmo_ref[...] = tall[NARROW_W:NARROW_W + tn, :].astype(BF16)

        wm = wmo_ref[...]
    else:
        wm = w_refs[0][...]

    @pl.when(j == 0)
    def _():
        xn = _rmsnorm(x_ref[...], g_ref[...]).astype(BF16)
        xn_ref[...] = xn
        if emit:
            out_col = lax.broadcasted_iota(jnp.int32, wsm_ref.shape, 0)
            ws = jnp.where(out_col < NARROW_W, wsm_ref[...], 0.0).astype(BF16)
            wso_ref[...] = ws
        else:
            ws = w_refs[1][...]
        small = _dot_nt(xn, ws)
        os_ref[...] = small
        this_layer(aki_ref)[...] = small[:, 0:IDX_DIM]

    acc = _dot_nt(xn_ref[...], wm)
    om_ref[...] = acc.astype(BF16)

    if tn <= C_W:
        @pl.when((j >= COL_C_F // tn) & (j < COL_C_I // tn))
        def _():
            cf_ref[...] = acc
    else:
        @pl.when(j == COL_C_F // tn)
        def _():
            cf_ref[...] = acc[:, COL_C_F % tn:COL_C_F % tn + C_W]

    def put_heads(ref, col, heads):
        off = col % tn
        assert off + heads * V7X_LANES <= tn

        @pl.when(j == col // tn)
        def _():
            dst = this_layer(ref)
            for hd in range(heads):
                dst[pl.ds(hd, tm, stride=heads), :] = acc[:, off + hd * V7X_LANES:off + (hd + 1) * V7X_LANES]

    put_heads(ak_ref, COL_A_K, A_KV_HEADS)
    put_heads(av_ref, COL_A_V, A_KV_HEADS)
    put_heads(bk_ref, COL_B_K, B_HEADS)
    put_heads(bv_ref, COL_B_V, B_HEADS)


def _proj(x, g, w, caches, layer, depth):
    n, d = x.shape
    emit = w[0] == "f32"
    tm = _row_tile(n, 512)
    tn = 512 if emit else PROJ_TN
    assert not emit or n == tm, "the weight-emitting variant expects a single row tile"
    lanes_per_tn = tn // V7X_LANES
    if emit:
        assert COL_B_Q % tn == 0 and NARROW_W % 8 == 0
        w_in_t = w[1]
        w_specs = [pl.BlockSpec((None, tn, d), lambda i, j: (layer, j, 0)),
                   pl.BlockSpec((None, V7X_LANES, d), lambda i, j: (layer, lanes_per_tn * (j + 1), 0)),
                   pl.BlockSpec((None, SMALL_W, d), lambda i, j: (layer, COL_B_Q // SMALL_W, 0))]
        w_args = [w_in_t, w_in_t, w_in_t]
    else:
        w_specs = [pl.BlockSpec((tn, d), lambda i, j: (j, 0)), pl.BlockSpec((SMALL_W, d), lambda i, j: (0, 0))]
        w_args = list(w[1:])
    cache_rows = (A_KV_HEADS, A_KV_HEADS, 1, B_HEADS, B_HEADS)
    cache_lanes = (V7X_LANES, V7X_LANES, IDX_DIM, V7X_LANES, V7X_LANES)
    alias_args = [] if caches is None else list(caches)
    out_specs = [pl.BlockSpec((tm, tn), lambda i, j: (i, j)), pl.BlockSpec((tm, SMALL_W), lambda i, j: (i, 0))]
    if alias_args:
        cache_spec = lambda r, wd: pl.BlockSpec((None, r * tm, wd), lambda i, j: (layer, i, 0))
    else:
        cache_spec = lambda r, wd: pl.BlockSpec((depth, r * tm, wd), lambda i, j: (0, i, 0))
    out_specs += [cache_spec(r, wd) for r, wd in zip(cache_rows, cache_lanes)]
    out_shape = [jax.ShapeDtypeStruct((n, MAIN_W), BF16), jax.ShapeDtypeStruct((n, SMALL_W), F32)]
    out_shape += [jax.ShapeDtypeStruct((depth, r * n, wd), F32) for r, wd in zip(cache_rows, cache_lanes)]
    if emit:
        out_specs += [pl.BlockSpec((tn, d), lambda i, j: (j, 0)), pl.BlockSpec((SMALL_W, d), lambda i, j: (0, 0))]
        out_shape += [jax.ShapeDtypeStruct((MAIN_W, d), BF16), jax.ShapeDtypeStruct((SMALL_W, d), BF16)]
    if tn <= C_W:
        assert COL_C_F % tn == 0 and C_W % tn == 0
        cf_first, cf_tiles = COL_C_F // tn, C_W // tn
        out_specs.append(pl.BlockSpec((tm, tn), lambda i, j: (i, jnp.clip(j - cf_first, 0, cf_tiles - 1))))
    else:
        assert COL_C_F % tn + C_W <= tn
        out_specs.append(pl.BlockSpec((tm, C_W), lambda i, j: (i, 0)))
    out_shape.append(jax.ShapeDtypeStruct((n, C_W), F32))
    n_in = 2 + len(w_args)
    res = pl.pallas_call(
        functools.partial(_proj_body, emit=emit, tn=tn, tm=tm, n_alias=len(alias_args), layer=layer),
        grid=(n // tm, MAIN_W // tn),
        in_specs=[pl.BlockSpec((tm, d), lambda i, j: (i, 0)), pl.BlockSpec((1, d), lambda i, j: (0, 0))]
        + w_specs + [pl.BlockSpec(memory_space=pl.ANY)] * len(alias_args),
        out_specs=out_specs,
        out_shape=out_shape,
        input_output_aliases={n_in + k: 2 + k for k in range(len(alias_args))},
        scratch_shapes=[pltpu.VMEM((tm, d), BF16)],
        compiler_params=_cparams(("parallel", "arbitrary")),
        name="in_proj_emit" if emit else "in_proj",
    )(x, g.reshape(1, d), *w_args, *alias_args)
    w_bf = ("bf16", res[7], res[8]) if emit else w
    return res[0], res[1], res[-1], tuple(res[2:7]), w_bf


def _sortable_key(x):
    bits = lax.bitcast_convert_type(x, jnp.int32)
    return jnp.where(bits < 0, bits ^ jnp.int32(0x7FFFFFFF), bits)


def _kth_largest(key, k):
    rows = key.shape[0]

    def body(it, t):
        cand = t + lax.shift_left(jnp.int32(1), jnp.int32(31) - it)
        cnt = jnp.sum(jnp.where(key >= cand, 1.0, 0.0), axis=1, keepdims=True)
        return jnp.where(cnt >= k, cand, t)

    t0 = jnp.full((rows, 1), jnp.iinfo(jnp.int32).min, jnp.int32)
    return lax.fori_loop(0, 32, body, t0)


def _tie_cut(tie, kpos, need, nbits):
    rows = tie.shape[0]

    def body(it, t):
        cand = t + lax.shift_left(jnp.int32(1), jnp.int32(nbits - 1) - it)
        cnt = jnp.sum(jnp.where(tie & (kpos < cand), 1.0, 0.0), axis=1, keepdims=True)
        return jnp.where(cnt < need, cand, t)

    return lax.fori_loop(0, nbits, body, jnp.zeros((rows, 1), jnp.int32))


def _visible(tq, s_pad, q_pos0, n_keys):
    qpos = q_pos0 + pl.program_id(1) * tq + lax.broadcasted_iota(jnp.int32, (tq, 1), 0)
    limit = jnp.minimum(((qpos >> 6) + 1) << 6, n_keys)
    kpos = lax.broadcasted_iota(jnp.int32, (tq, s_pad), 1)
    return kpos < limit, kpos


def _head_keys(new_ref, rows, lo, width, past_ref, head, heads, s_pad):
    new = new_ref[rows, lo:lo + width]
    p_len = past_ref.shape[0] // heads
    hist = past_ref[...] if heads == 1 else past_ref[pl.ds(head, p_len, stride=heads), :]
    parts = [hist.astype(BF16), new.astype(BF16)]
    pad = s_pad - p_len - new.shape[0]
    if pad:
        parts.append(jnp.zeros((pad, width), BF16))
    return jnp.concatenate(parts, axis=0)


def _stack_rows(pieces):
    return pieces[0] if len(pieces) == 1 else jnp.concatenate(pieces, axis=0)


def _dsa_hist_body(q_ref, qi_ref, wq_ref, k_ref, v_ref, ki_ref, pk_ref, pv_ref, pki_ref, o_ref,
                   *, nb, tq, s_pad, n_keys, q_pos0, topk):
    vis1, kpos1 = _visible(tq, s_pad, q_pos0, n_keys)
    rep = A_HEADS // A_KV_HEADS
    scores = []
    for bb in range(nb):
        rows = slice(bb * tq, (bb + 1) * tq)
        kidx = _head_keys(ki_ref, rows, 0, IDX_DIM, pki_ref.at[bb], 0, 1, s_pad)
        qi_all = _stack_rows([qi_ref[rows, h * IDX_DIM:(h + 1) * IDX_DIM] for h in range(IDX_HEADS)]).astype(BF16)
        wq = wq_ref[rows, IDX_DIM:IDX_DIM + IDX_HEADS] * (IDX_HEADS ** -0.5) * IDX_SCALE
        w_col = _stack_rows([wq[:, h:h + 1] for h in range(IDX_HEADS)])
        x = jnp.maximum(_dot_nt(qi_all, kidx), 0.0) * w_col
        score = jnp.sum(x.reshape(IDX_HEADS, tq, s_pad), axis=0)
        scores.append(jnp.where(vis1, score + 0.0, NEG_BIG))

    key = _sortable_key(_stack_rows(scores))
    kpos = _stack_rows([kpos1] * nb)
    thr = _kth_largest(key, float(topk))
    n_gt = jnp.sum(jnp.where(key > thr, 1.0, 0.0), axis=1, keepdims=True)
    tie = key == thr
    cut = _tie_cut(tie, kpos, float(topk) - n_gt, max(1, (s_pad - 1).bit_length()))
    attend = jnp.where((key > thr) | (tie & (kpos <= cut)), 1.0, 0.0)

    for bb in range(nb):
        rows = slice(bb * tq, (bb + 1) * tq)
        sel = _stack_rows([jnp.where(vis1, attend[rows], 0.0)] * rep) > 0.5
        for g in range(A_KV_HEADS):
            k_g = _head_keys(k_ref, rows, g * A_HEAD_DIM, A_HEAD_DIM, pk_ref.at[bb], g, A_KV_HEADS, s_pad)
            v_g = _head_keys(v_ref, rows, g * A_HEAD_DIM, A_HEAD_DIM, pv_ref.at[bb], g, A_KV_HEADS, s_pad)
            heads = range(g * rep, (g + 1) * rep)
            q_g = _stack_rows([q_ref[rows, h * A_HEAD_DIM:(h + 1) * A_HEAD_DIM] for h in heads]).astype(BF16)
            att = jnp.where(sel, _dot_nt(q_g, k_g) * (A_HEAD_DIM ** -0.5), NEG_BIG)
            p = jnp.exp(att - jnp.max(att, axis=1, keepdims=True))
            l = jnp.sum(p, axis=1, keepdims=True)
            o = _dot(p.astype(BF16), v_g) / l
            for r, h in enumerate(heads):
                o_ref[rows, h * A_HEAD_DIM:(h + 1) * A_HEAD_DIM] = o[r * tq:(r + 1) * tq].astype(o_ref.dtype)


def _softmax_step_t(carry, s, v_c):
    m, l, acc = carry
    m_new = jnp.maximum(m, _col_reduce(s, jnp.max))
    alpha = jnp.exp(m - m_new)
    p = jnp.exp(s - m_new)
    return m_new, alpha * l + _col_reduce(p, jnp.sum), alpha * acc + _dot_tn(v_c, p.astype(BF16))


def _col_reduce(x, op):
    rows, tq = x.shape
    return op(op(x.reshape(rows // COUNT_ROWS, COUNT_ROWS, tq), axis=0), axis=0, keepdims=True)


def _softmax_init_t(tq, width):
    return (jnp.full((1, tq), -jnp.inf, F32), jnp.zeros((1, tq), F32), jnp.zeros((width, tq), F32))


def _causal_extent(tq, kc):
    i = pl.program_id(1)
    qpos = i * tq + lax.broadcasted_iota(jnp.int32, (1, tq), 1)
    limit = ((qpos >> 6) + 1) << 6
    return (i * tq + tq + kc - 1) // kc, limit, lax.broadcasted_iota(jnp.int32, (kc, 1), 0)


def _dsa_causal_body(q_ref, qi_ref, wq_ref, k_ref, v_ref, ki_ref, o_ref, key_ref, cut_ref, *, tq, kc, topk, nbits):
    n_chunks, limit, krow = _causal_extent(tq, kc)
    wq_t = wq_ref[...].T[IDX_DIM:IDX_DIM + IDX_HEADS, :] * (IDX_HEADS ** -0.5) * IDX_SCALE
    qi_all = jnp.concatenate([qi_ref[:, h * IDX_DIM:(h + 1) * IDX_DIM] for h in range(IDX_HEADS)], axis=0).astype(BF16)

    def score_chunk(c, carry):
        start = pl.multiple_of(c * kc, kc)
        kidx = ki_ref[pl.ds(start, kc), 0:IDX_DIM].astype(BF16)
        logits = _dot_nt(kidx, qi_all)
        acc = jnp.zeros((kc, tq), F32)
        for h in range(IDX_HEADS):
            acc = acc + jnp.maximum(logits[:, h * tq:(h + 1) * tq], 0.0) * wq_t[h:h + 1, :]
        score = jnp.where(start + krow < limit, acc + 0.0, NEG_BIG)
        key_ref[c] = _sortable_key(score)
        return carry

    lax.fori_loop(0, n_chunks, score_chunk, 0)

    def count(pred):
        def body(c, part):
            ones = jnp.where(pred(key_ref[c], c), 1.0, 0.0)
            return part + jnp.sum(ones.reshape(kc // COUNT_ROWS, COUNT_ROWS, tq), axis=0)

        part = lax.fori_loop(0, n_chunks, body, jnp.zeros((COUNT_ROWS, tq), F32))
        return jnp.sum(part, axis=0, keepdims=True)

    def thr_step(it, t):
        cand = t + lax.shift_left(jnp.int32(1), jnp.int32(31) - it)
        return jnp.where(count(lambda key, c: key >= cand) >= float(topk), cand, t)

    thr = lax.fori_loop(0, 32, thr_step, jnp.full((1, tq), jnp.iinfo(jnp.int32).min, jnp.int32))
    need = float(topk) - count(lambda key, c: key > thr)
    n_tie = count(lambda key, c: key == thr)

    cut_ref[...] = jnp.full(cut_ref.shape, jnp.iinfo(jnp.int32).max, jnp.int32)

    @pl.when(jnp.max(jnp.where(n_tie != need, 1.0, 0.0)) > 0.0)
    def _():
        def cut_step(it, t):
            cand = t + lax.shift_left(jnp.int32(1), jnp.int32(nbits - 1) - it)
            below = count(lambda key, c: (key == thr) & (c * kc + krow < cand))
            return jnp.where(below < need, cand, t)

        cut = lax.fori_loop(0, nbits, cut_step, jnp.zeros((1, tq), jnp.int32))
        cut_ref[...] = jnp.broadcast_to(cut, cut_ref.shape)

    cut = cut_ref[0:1, :]

    def select_chunk(c, carry):
        key = key_ref[c]
        kpos = c * kc + krow
        tie_flag = jnp.where(key == thr, jnp.where(kpos <= cut, 0, 1), 1)
        key_ref[c] = jnp.where(kpos < limit, jnp.where(key > thr, 0, tie_flag), 1)
        return carry

    lax.fori_loop(0, n_chunks, select_chunk, 0)

    rep = A_HEADS // A_KV_HEADS
    q_g = [jnp.concatenate([q_ref[:, h * A_HEAD_DIM:(h + 1) * A_HEAD_DIM] for h in range(g * rep, (g + 1) * rep)],
                           axis=0).astype(BF16) for g in range(A_KV_HEADS)]

    def att_chunk(c, carry):
        start = pl.multiple_of(c * kc, kc)
        attend = key_ref[c] == 0
        attend = jnp.concatenate([attend] * rep, axis=1)
        logits = [_dot_nt(k_ref[pl.ds(start, kc), g * A_HEAD_DIM:(g + 1) * A_HEAD_DIM].astype(BF16), q_g[g])
                  for g in range(A_KV_HEADS)]
        v_c = [v_ref[pl.ds(start, kc), g * A_HEAD_DIM:(g + 1) * A_HEAD_DIM].astype(BF16) for g in range(A_KV_HEADS)]
        return tuple(
            _softmax_step_t(carry[g], jnp.where(attend, logits[g] * (A_HEAD_DIM ** -0.5), NEG_BIG), v_c[g])
            for g in range(A_KV_HEADS))

    res = lax.fori_loop(0, n_chunks, att_chunk, tuple(_softmax_init_t(rep * tq, A_HEAD_DIM) for _ in range(A_KV_HEADS)))
    for g in range(A_KV_HEADS):
        _, l, acc = res[g]
        o_t = acc / l
        for r in range(rep):
            h = g * rep + r
            o_ref[:, h * A_HEAD_DIM:(h + 1) * A_HEAD_DIM] = o_t[:, r * tq:(r + 1) * tq].T.astype(o_ref.dtype)


def _diff_causal_body(q_ref, k_ref, v_ref, lam_ref, g_ref, o_ref, *, tq, kc, lam_init):
    n_chunks, limit, krow = _causal_extent(tq, kc)
    lv = lam_ref[...]
    lam = (jnp.exp(jnp.sum(lv[0:1] * lv[1:2], axis=1, keepdims=True))
           - jnp.exp(jnp.sum(lv[2:3] * lv[3:4], axis=1, keepdims=True)) + lam_init)
    gain = g_ref[...] * (1.0 - lam_init)
    head_w = 2 * B_QK_DIM
    qlane = lax.broadcasted_iota(jnp.int32, (tq, head_w), 1)
    for h0 in range(0, B_HEADS, DIFF_HEADS_PER_LOOP):
        heads = range(h0, h0 + DIFF_HEADS_PER_LOOP)
        q_2 = [jnp.concatenate([jnp.where((qlane >= c * B_QK_DIM) & (qlane < (c + 1) * B_QK_DIM),
                                          q_ref[:, h * head_w:(h + 1) * head_w] * DIFF_SCALE, 0.0) for c in range(2)],
                               axis=0).astype(BF16) for h in heads]

        def chunk(c, carry, q_2=q_2, heads=heads):
            start = pl.multiple_of(c * kc, kc)
            vis = start + krow < jnp.concatenate([limit, limit], axis=1)
            logits = [_dot_nt(k_ref[pl.ds(start, kc), h * head_w:(h + 1) * head_w].astype(BF16), q_2[j])
                      for j, h in enumerate(heads)]
            v_c = [v_ref[pl.ds(start, kc), h * B_V_DIM:(h + 1) * B_V_DIM].astype(BF16) for h in heads]
            return tuple(
                _softmax_step_t(carry[j], jnp.where(vis, logits[j], NEG_BIG), v_c[j])
                for j in range(len(heads)))

        res = lax.fori_loop(0, n_chunks, chunk, tuple(_softmax_init_t(2 * tq, B_V_DIM) for _ in heads))
        for j, h in enumerate(heads):
            _, l, acc = res[j]
            o_t = acc / l
            o = (o_t[:, 0:tq] - lam * o_t[:, tq:2 * tq]).T
            o = o * lax.rsqrt(jnp.mean(o * o, axis=1, keepdims=True) + NORM_EPS) * gain
            o_ref[:, h * B_V_DIM:(h + 1) * B_V_DIM] = o.astype(o_ref.dtype)


def _blk(arr_col, rows, width, kind, nq):
    _, col = arr_col
    cb = col // width
    assert col % width == 0
    if kind == "q":
        return pl.BlockSpec((rows, width), lambda b, i: (b * nq + i, cb))
    return pl.BlockSpec((rows, width), lambda b, i: (b, cb))


def _past_blk(cache, layer, nb):
    return pl.BlockSpec((None, nb) + cache.shape[2:], lambda b, i: (layer, b, 0, 0))


def _seqs_per_step(bsz, limit):
    nb = min(bsz, limit)
    while bsz % nb:
        nb -= 1
    return nb


def _dsa(q, qi, wq, k, v, ki, past, *, bsz, t, s_pad, n_keys, q_pos0, tq, layer):
    nq = t // tq
    topk = min(TOPK_MAX, n_keys // 4)
    scratch = []
    if past is None:
        nb = 1
        kc = _row_tile(t, DSA_KEY_CHUNK)
        body = functools.partial(_dsa_causal_body, tq=tq, kc=kc, topk=topk, nbits=max(1, (t - 1).bit_length()))
        scratch = [pltpu.VMEM((t // kc, kc, tq), jnp.int32), pltpu.VMEM((8, tq), jnp.int32)]
    else:
        assert nq == 1
        nb = _seqs_per_step(bsz, DSA_SEQS_PER_STEP)
        body = functools.partial(_dsa_hist_body, nb=nb, tq=tq, s_pad=s_pad, n_keys=n_keys, q_pos0=q_pos0, topk=topk)
    in_specs = [
        _blk(q, nb * tq, A_Q_W, "q", nq),
        _blk(qi, nb * tq, A_QI_W, "q", nq),
        _blk(wq, nb * tq, SMALL_W, "q", nq),
        _blk(k, nb * t, A_KV_W, "kv", nq),
        _blk(v, nb * t, A_KV_W, "kv", nq),
        _blk(ki, nb * t, SMALL_W, "kv", nq),
    ]
    args = [q[0], qi[0], wq[0], k[0], v[0], ki[0]]
    if past is not None:
        in_specs += [_past_blk(c, layer, nb) for c in past]
        args += list(past)
    return pl.pallas_call(
        body,
        grid=(bsz // nb, nq),
        in_specs=in_specs,
        out_specs=pl.BlockSpec((nb * tq, A_Q_W), lambda b, i: (b * nq + i, 0)),
        out_shape=jax.ShapeDtypeStruct((bsz * t, A_Q_W), BF16),
        scratch_shapes=scratch,
        compiler_params=_cparams(("parallel", "parallel")),
        name="dsa",
    )(*args)


def _diff_hist_body(q_ref, k_ref, v_ref, lam_ref, g_ref, pk_ref, pv_ref, o_ref,
                    *, nb, tq, s_pad, n_keys, q_pos0, lam_init):
    vis1, _ = _visible(tq, s_pad, q_pos0, n_keys)
    vis = _stack_rows([jnp.where(vis1, 1.0, 0.0)] * 2) > 0.5
    lv = lam_ref[...]
    lam = (jnp.exp(jnp.sum(lv[0:1] * lv[1:2], axis=1, keepdims=True))
           - jnp.exp(jnp.sum(lv[2:3] * lv[3:4], axis=1, keepdims=True)) + lam_init)
    gain = g_ref[...] * (1.0 - lam_init)
    head_w = 2 * B_QK_DIM
    lane = lax.broadcasted_iota(jnp.int32, (tq, head_w), 1)
    for bb in range(nb):
        rows = slice(bb * tq, (bb + 1) * tq)
        for h in range(B_HEADS):
            k_h = _head_keys(k_ref, rows, h * head_w, head_w, pk_ref.at[bb], h, B_HEADS, s_pad)
            v_h = _head_keys(v_ref, rows, h * B_V_DIM, B_V_DIM, pv_ref.at[bb], h, B_HEADS, s_pad)
            q_h = q_ref[rows, h * head_w:(h + 1) * head_w]
            q_2 = _stack_rows([jnp.where((lane >= c * B_QK_DIM) & (lane < (c + 1) * B_QK_DIM), q_h * DIFF_SCALE, 0.0)
                               for c in range(2)]).astype(BF16)
            logit = jnp.where(vis, _dot_nt(q_2, k_h), NEG_BIG)
            p = jnp.exp(logit - jnp.max(logit, axis=1, keepdims=True))
            l = jnp.sum(p, axis=1, keepdims=True)
            o_2 = _dot(p.astype(BF16), v_h) / l
            o = o_2[0:tq] - lam * o_2[tq:2 * tq]
            o = o * lax.rsqrt(jnp.mean(o * o, axis=1, keepdims=True) + NORM_EPS) * gain
            o_ref[rows, h * B_V_DIM:(h + 1) * B_V_DIM] = o.astype(o_ref.dtype)


def _diff(q, k, v, lam_vecs, g_diff, past, *, bsz, t, s_pad, n_keys, q_pos0, tq, layer):
    nq = t // tq
    lam_init = 0.8 - 0.6 * math.exp(-0.3 * layer)
    if past is None:
        nb = 1
        body = functools.partial(_diff_causal_body, tq=tq, kc=_row_tile(t, DIFF_KEY_CHUNK), lam_init=lam_init)
    else:
        assert nq == 1
        nb = _seqs_per_step(bsz, DIFF_SEQS_PER_STEP)
        body = functools.partial(_diff_hist_body, nb=nb, tq=tq, s_pad=s_pad, n_keys=n_keys, q_pos0=q_pos0,
                                 lam_init=lam_init)
    in_specs = [
        _blk(q, nb * tq, B_QK_W, "q", nq),
        _blk(k, nb * t, B_QK_W, "kv", nq),
        _blk(v, nb * t, B_V_W, "kv", nq),
        pl.BlockSpec((None, 4, B_QK_DIM), lambda b, i: (layer, 0, 0)),
        pl.BlockSpec((None, 1, B_V_DIM), lambda b, i: (layer, 0, 0)),
    ]
    args = [q[0], k[0], v[0], lam_vecs, g_diff]
    if past is not None:
        in_specs += [_past_blk(c, layer, nb) for c in past]
        args += list(past)
    return pl.pallas_call(
        body,
        grid=(bsz // nb, nq),
        in_specs=in_specs,
        out_specs=pl.BlockSpec((nb * tq, B_V_W), lambda b, i: (b * nq + i, 0)),
        out_shape=jax.ShapeDtypeStruct((bsz * t, B_V_W), BF16),
        compiler_params=_cparams(("parallel", "parallel")),
        name="diff_attn",
    )(*args)


def _cumsum_rows(x):
    n = x.shape[0]
    row = lax.broadcasted_iota(jnp.int32, (n, 1), 0)
    d = 1
    while d < n:
        x = x + jnp.where(row >= d, pltpu.roll(x, d, axis=0), 0.0)
        d *= 2
    return x


def _anchor_rows(b, c):
    n, w = b.shape
    if 2 * c >= 8:
        pieces = [jnp.broadcast_to(b[j * 2 * c + c - 1:j * 2 * c + c, :], (2 * c, w)) for j in range(n // (2 * c))]
        return pieces[0] if len(pieces) == 1 else jnp.concatenate(pieces, axis=0)
    rmod = lax.broadcasted_iota(jnp.int32, (n, 1), 0) & (2 * c - 1)
    m = b
    for s in range(-c, c):
        if s != 0:
            m = jnp.where(rmod == c - 1 - s, pltpu.roll(b, (-s) % n, axis=0), m)
    return m


def _hgrn_body(*refs, chunk, layer, has_s0):
    if has_s0:
        q_ref, f_ref, i_ref, g_ref, glb_ref, gn_ref, s0_ref, o_ref, s_ref, st_ref = refs
    else:
        q_ref, f_ref, i_ref, g_ref, glb_ref, gn_ref, o_ref, s_ref, st_ref = refs
    ci = pl.program_id(1)

    @pl.when(ci == 0)
    def _():
        for h in range(C_HEADS):
            st_ref[h] = s0_ref[h].T if has_s0 else jnp.zeros((C_V_DIM, C_K_DIM), F32)

    glb = glb_ref[...]
    pe = jnp.exp(glb - jnp.max(glb, axis=0, keepdims=True))
    pn = pe / jnp.sum(pe, axis=0, keepdims=True)
    lb = jnp.sum(pn[0:layer + 1], axis=0, keepdims=True) - pn[0:1]

    zf = f_ref[...]
    log_sig = jnp.minimum(zf, 0.0) - jnp.log1p(jnp.exp(-jnp.abs(zf)))
    la = jnp.log(jnp.maximum(lb, LB_FLOOR))
    lc = jnp.log1p(-lb) + log_sig
    hi = jnp.maximum(la, lc)
    log_f = hi + jnp.log1p(jnp.exp(jnp.minimum(la, lc) - hi))
    kc = (1.0 - lb) * jax.nn.sigmoid(-zf)
    b_all = _cumsum_rows(log_f)

    row = lax.broadcasted_iota(jnp.int32, (chunk, 1), 0)
    rr = lax.broadcasted_iota(jnp.int32, (chunk, chunk), 0)
    cc = lax.broadcasted_iota(jnp.int32, (chunk, chunk), 1)
    heads = range(C_HEADS)
    head = lambda x, h: x[:, h * C_K_DIM:(h + 1) * C_K_DIM]

    q_all = q_ref[...].astype(F32)
    vb_all = i_ref[...]
    v_all = vb_all.astype(F32)
    b_last = b_all[chunk - 1:chunk, :]
    qe_all = (q_all * jnp.exp(b_all)).astype(BF16)
    kdec_all = (kc * jnp.exp(b_last - b_all)).astype(BF16)
    diag_all = q_all * kc
    levels = []
    c = chunk // 2
    while c >= 1:
        lg = c.bit_length() - 1
        upper = ((row >> lg) & 1) == 1
        d = b_all - _anchor_rows(b_all, c)
        e = jnp.exp(jnp.where(upper, d, -d))
        x = (jnp.where(upper, q_all, kc) * e).astype(BF16)
        pairs = ((rr >> (lg + 1)) == (cc >> (lg + 1))) & (((rr >> lg) & 1) == 1) & (((cc >> lg) & 1) == 0)
        levels.append((x, pairs))
        c //= 2

    st = [st_ref[h] for h in heads]
    o_inter = [_dot_nt(head(qe_all, h), st[h].astype(BF16)) for h in heads]
    a = []
    for h in heads:
        a_h = jnp.zeros((chunk, chunk), F32)
        for x, pairs in levels:
            a_h = jnp.where(pairs, _dot_nt(head(x, h), head(x, h)), a_h)
        a.append(a_h)
    for h in heads:
        st_new = st[h] * jnp.exp(head(b_last, h)) + _dot_tn(head(vb_all, h), head(kdec_all, h))
        st_ref[h] = st_new

        @pl.when(ci == pl.num_programs(1) - 1)
        def _():
            s_ref[0, h] = st_new.T

    gn = gn_ref[...]
    for h in heads:
        o = o_inter[h] + jnp.sum(head(diag_all, h), axis=1, keepdims=True) * head(v_all, h)
        o = o + _dot(a[h].astype(BF16), head(vb_all, h))
        o = o * lax.rsqrt(jnp.mean(o * o, axis=1, keepdims=True) + NORM_EPS) * gn
        zg = head(g_ref, h).astype(F32)
        o_ref[:, h * C_V_DIM:(h + 1) * C_V_DIM] = (o * (zg * jax.nn.sigmoid(zg))).astype(o_ref.dtype)


def _hgrn(cq, cf, cin, cg, gamma_lb, g_hgrn, s0, *, bsz, t, layer):
    chunk = _row_tile(t, HGRN_CHUNK)
    nc = t // chunk
    depth = gamma_lb.shape[0]
    has_s0 = s0 is not None

    def seq_blk(arr_col):
        cb = arr_col[1] // C_W
        return pl.BlockSpec((chunk, C_W), lambda b, c: (b * nc + c, cb))

    in_specs = [seq_blk(cq), seq_blk(cf), seq_blk(cin), seq_blk(cg),
                pl.BlockSpec((depth, C_W), lambda b, c: (0, 0)),
                pl.BlockSpec((None, 1, C_V_DIM), lambda b, c: (layer, 0, 0))]
    args = [cq[0], cf[0], cin[0], cg[0], gamma_lb, g_hgrn]
    if has_s0:
        in_specs.append(pl.BlockSpec((None, None, C_HEADS, C_K_DIM, C_V_DIM), lambda b, c: (layer, b, 0, 0, 0)))
        args.append(s0)
    return pl.pallas_call(
        functools.partial(_hgrn_body, chunk=chunk, layer=layer, has_s0=has_s0),
        grid=(bsz, nc),
        in_specs=in_specs,
        out_specs=[pl.BlockSpec((chunk, C_W), lambda b, c: (b * nc + c, 0)),
                   pl.BlockSpec((1, C_HEADS, C_K_DIM, C_V_DIM), lambda b, c: (b, 0, 0, 0))],
        out_shape=[jax.ShapeDtypeStruct((bsz * t, C_W), BF16),
                   jax.ShapeDtypeStruct((bsz, C_HEADS, C_K_DIM, C_V_DIM), F32)],
        scratch_shapes=[pltpu.VMEM((C_HEADS, C_V_DIM, C_K_DIM), F32)],
        compiler_params=_cparams(("parallel", "arbitrary")),
        name="hgrn2",
    )(*args)


def _outproj_body(*refs, emit):
    h_ref, oa_ref, ob_ref, oc_ref, wa_ref, wb_ref, wc_ref, o_ref = refs[:8]
    wa, wb, wc = wa_ref[...].astype(BF16), wb_ref[...].astype(BF16), wc_ref[...].astype(BF16)
    if emit:
        refs[8][...] = wa
        refs[9][...] = wb
        refs[10][...] = wc
    o_ref[...] = h_ref[...] + (_dot(oa_ref[...], wa) + _dot(ob_ref[...], wb) + _dot(oc_ref[...], wc))


def _outproj(h, oa, ob, oc, w):
    n, d = h.shape
    emit = w[0] == "f32"
    tm = _row_tile(n, 512)
    assert not emit or n == tm, "the weight-emitting variant expects a single row tile"
    row = lambda wd: pl.BlockSpec((tm, wd), lambda i: (i, 0))
    whole = lambda r: pl.BlockSpec((r, d), lambda i: (0, 0))
    seg_rows = (A_Q_W, B_V_W, C_W)
    seg_first = (0, A_Q_W, A_Q_W + B_V_W)
    if emit:
        _, w_out, layer = w
        w_specs = [pl.BlockSpec((None, r, d), functools.partial(lambda i, blk: (layer, blk, 0), blk=f // r))
                   for r, f in zip(seg_rows, seg_first)]
        w_args = [w_out] * 3
    else:
        w_specs = [whole(r) for r in seg_rows]
        w_args = list(w[1:])
    out_specs = [row(d)]
    out_shape = [jax.ShapeDtypeStruct((n, d), F32)]
    if emit:
        out_specs += [whole(r) for r in seg_rows]
        out_shape += [jax.ShapeDtypeStruct((r, d), BF16) for r in seg_rows]
    res = pl.pallas_call(
        functools.partial(_outproj_body, emit=emit),
        grid=(n // tm,),
        in_specs=[row(d), row(A_Q_W), row(B_V_W), row(C_W)] + w_specs,
        out_specs=out_specs,
        out_shape=out_shape,
        compiler_params=_cparams(("parallel",)),
        name="out_proj_emit" if emit else "out_proj",
    )(h, oa, ob, oc, *w_args)
    return (res[0], ("bf16", res[1], res[2], res[3])) if emit else (res[0], w)


def _layer(x, layer, w, p, caches, past, geom):
    bsz, t = geom["bsz"], geom["t"]
    depth = p["g_ffn1"].shape[0]
    wb = {}
    h, wb["ffn1"] = _ffn(x, p["g_ffn1"][layer], w["ffn1"])
    main, small, c_f, caches, wb["proj"] = _proj(h, p["g_mix"][layer], w["proj"], caches, layer, depth)
    if past is None:
        past_a = past_b = s0 = None
    else:
        past_a, past_b, s0 = past[0:3], past[3:5], past[5]
    o_a = _dsa((main, COL_A_Q), (main, COL_A_QI), (small, 0), (main, COL_A_K), (main, COL_A_V), (small, 0), past_a,
               layer=layer, **geom)
    o_b = _diff((main, COL_B_Q), (main, COL_B_K), (main, COL_B_V), p["diff_lambda"], p["g_diff"], past_b,
                layer=layer, **geom)
    o_c, s_c = _hgrn((main, COL_C_Q), (c_f, 0), (main, COL_C_I), (main, COL_C_G),
                     p["gamma_lb"], p["g_hgrn"], s0, bsz=bsz, t=t, layer=layer)
    h, wb["out"] = _outproj(h, o_a, o_b, o_c, w["out"])
    x, wb["ffn2"] = _ffn(h, p["g_ffn2"][layer], w["ffn2"], p["g_final"] if layer == depth - 1 else None)
    return x, caches, s_c, wb


def kernel(x_prompt, x_sample, cache_a_k, cache_a_v, cache_a_kidx, cache_b_k, cache_b_v, state_c,
           g_ffn1, w_ffn1_in, w_ffn1_out, g_mix, w_in, diff_lambda, g_diff, gamma_lb, g_hgrn, w_out,
           g_ffn2, w_ffn2_in, w_ffn2_out, g_final):
    depth, d = w_in.shape[0], w_in.shape[1]
    bp, tp, _ = x_prompt.shape
    bs, ts, _ = x_sample.shape
    p_len = cache_a_k.shape[2]
    p = dict(g_ffn1=g_ffn1, g_mix=g_mix, diff_lambda=diff_lambda, g_diff=g_diff.reshape(depth, 1, B_V_DIM),
             gamma_lb=gamma_lb, g_hgrn=g_hgrn.reshape(depth, 1, C_V_DIM), g_ffn2=g_ffn2, g_final=g_final)
    geom_p = dict(bsz=bp, t=tp, s_pad=tp, n_keys=tp, q_pos0=0, tq=_row_tile(tp, ATTN_QUERY_TILE))
    n_keys_s = p_len + ts
    geom_s = dict(bsz=bs, t=ts, s_pad=-(-n_keys_s // V7X_LANES) * V7X_LANES, n_keys=n_keys_s, q_pos0=p_len, tq=ts)
    past = (cache_a_k.reshape(depth, bs, p_len * A_KV_HEADS, A_HEAD_DIM),
            cache_a_v.reshape(depth, bs, p_len * A_KV_HEADS, A_HEAD_DIM),
            cache_a_kidx,
            cache_b_k.reshape(depth, bs, p_len * B_HEADS, 2 * B_QK_DIM),
            cache_b_v.reshape(depth, bs, p_len * B_HEADS, B_V_DIM),
            state_c)

    w_in_t = jnp.swapaxes(w_in, 1, 2)
    xp = x_prompt.reshape(bp * tp, d)
    xs = x_sample.reshape(bs * ts, d)
    caches_p = caches_s = None
    states_p, states_s = [], []
    for layer in range(depth):
        w_f32 = dict(ffn1=("f32", w_ffn1_in, w_ffn1_out, layer), proj=("f32", w_in_t), out=("f32", w_out, layer),
                     ffn2=("f32", w_ffn2_in, w_ffn2_out, layer))
        xs, caches_s, s_c, w_bf = _layer(xs, layer, w_f32, p, caches_s, past, geom_s)
        states_s.append(s_c)
        xp, caches_p, s_c, _ = _layer(xp, layer, w_bf, p, caches_p, None, geom_p)
        states_p.append(s_c)

    def group_out(x, caches, states, bsz, t):
        a_k, a_v, a_ki, b_k, b_v = caches
        return (x.reshape(bsz, t, d),
                a_k.reshape(depth, bsz, t, A_KV_HEADS, A_HEAD_DIM), a_v.reshape(depth, bsz, t, A_KV_HEADS, A_HEAD_DIM),
                a_ki.reshape(depth, bsz, t, IDX_DIM), b_k.reshape(depth, bsz, t, B_HEADS, 2 * B_QK_DIM),
                b_v.reshape(depth, bsz, t, B_HEADS, B_V_DIM), jnp.stack(states))

    out_p = group_out(xp, caches_p, states_p, bp, tp)
    out_s = group_out(xs, caches_s, states_s, bs, ts)
    return (out_p[0], out_s[0], *out_p[1:], *out_s[1:])
```

```python
import functools
import math

import jax
import jax.numpy as jnp
from jax import lax
from jax.experimental import pallas as pl
from jax.experimental.pallas import tpu as pltpu

F32 = jnp.float32
BF16 = jnp.bfloat16

CHUNK = 64
CHUNK_SHIFT = CHUNK.bit_length() - 1
assert CHUNK == 1 << CHUNK_SHIFT
A_HEADS, A_KV_HEADS, A_HEAD_DIM = 4, 2, 128
IDX_HEADS, IDX_DIM = 8, 64
TOPK_MAX = 256
B_HEADS, B_QK_DIM, B_V_DIM = 4, 64, 128
C_HEADS, C_K_DIM, C_V_DIM = 8, 128, 128
IDX_SCALE = IDX_DIM ** -0.5
DIFF_SCALE = B_QK_DIM ** -0.5
assert math.frexp(IDX_SCALE)[0] == 0.5 and math.frexp(DIFF_SCALE)[0] == 0.5
NORM_EPS = 1e-6
NEG_BIG = -1e30
LB_FLOOR = 1e-30

A_Q_W = A_HEADS * A_HEAD_DIM
A_KV_W = A_KV_HEADS * A_HEAD_DIM
A_QI_W = IDX_HEADS * IDX_DIM
B_QK_W = B_HEADS * 2 * B_QK_DIM
B_V_W = B_HEADS * B_V_DIM
C_W = C_HEADS * C_K_DIM
MIX_W = A_Q_W + B_V_W + C_W
MAIN_W = A_Q_W + 2 * A_KV_W + A_QI_W + 2 * B_QK_W + B_V_W + 4 * C_W
NARROW_W = IDX_DIM + IDX_HEADS
SMALL_W = 128
FFN_ROW_TILE = 1024
PROJ_TN = 1792
ATTN_QUERY_TILE = 256
DSA_KEY_CHUNK = 256
DIFF_KEY_CHUNK = 512
COUNT_ROWS = 64
DIFF_HEADS_PER_LOOP = 4
DSA_SEQS_PER_STEP = 8
DIFF_SEQS_PER_STEP = 4
HGRN_CHUNK = 256
COL_A_Q, COL_A_K, COL_A_V, COL_A_QI = 0, 512, 768, 1024
COL_B_Q, COL_B_K, COL_B_V = 1536, 2048, 2560
COL_C_Q, COL_C_F, COL_C_I, COL_C_G = 3072, 4096, 5120, 6144

V7X_LANES = 128
V7X_VMEM_LIMIT = 56 * 1024 * 1024


def _cparams(semantics):
    return pltpu.CompilerParams(dimension_semantics=semantics, vmem_limit_bytes=V7X_VMEM_LIMIT)


def _rmsnorm(x, g):
    return x * lax.rsqrt(jnp.mean(x * x, axis=-1, keepdims=True) + NORM_EPS) * g


def _dot(a, b):
    return jnp.dot(a, b, preferred_element_type=F32)


def _dot_nt(a, b):
    return lax.dot_general(a, b, (((1,), (1,)), ((), ())), preferred_element_type=F32)


def _dot_tn(a, b):
    return lax.dot_general(a, b, (((0,), (0,)), ((), ())), preferred_element_type=F32)


def _row_tile(n, pref):
    t = min(n, pref)
    while n % t:
        t //= 2
    return t


def _ffn_body(*refs, n_ff_steps, final_norm, emit):
    x_ref, g_ref, wg_ref, wu_ref, wo_ref = refs[:5]
    gf_ref = refs[5] if final_norm else None
    outs = refs[5 + final_norm:]
    o_ref, xn_ref = outs[0], outs[-1]
    j = pl.program_id(1)

    @pl.when(j == 0)
    def _():
        x = x_ref[...]
        xn_ref[...] = _rmsnorm(x, g_ref[...]).astype(BF16)
        o_ref[...] = x

    wg, wu, wo = wg_ref[...].astype(BF16), wu_ref[...].astype(BF16), wo_ref[...].astype(BF16)
    if emit:
        outs[1][...] = wg
        outs[2][...] = wu
        outs[3][...] = wo
    xn = xn_ref[...]
    gate = _dot(xn, wg)
    up = _dot(xn, wu)
    act = (gate * jax.nn.sigmoid(gate)) * up * 0.5
    o_ref[...] += _dot(act.astype(BF16), wo)

    if final_norm:
        @pl.when(j == n_ff_steps - 1)
        def _():
            o_ref[...] = _rmsnorm(o_ref[...], gf_ref[...])


def _ffn(x, g, w, g_final=None):
    n, d = x.shape
    emit = w[0] == "f32"
    if emit:
        _, w_in, w_out, layer = w
        f = w_out.shape[1]
    else:
        _, wg, wu, wo = w
        f = wo.shape[0]
    tm = _row_tile(n, FFN_ROW_TILE)
    tf = 512 if f % 512 == 0 else f
    nj = f // tf
    assert not emit or n == tm, "the weight-emitting variant expects a single row tile"
    final_norm = g_final is not None
    in_specs = [pl.BlockSpec((tm, d), lambda i, j: (i, 0)), pl.BlockSpec((1, d), lambda i, j: (0, 0))]
    if emit:
        in_specs += [pl.BlockSpec((None, d, tf), lambda i, j: (layer, 0, j)),
                     pl.BlockSpec((None, d, tf), lambda i, j: (layer, 0, j + nj)),
                     pl.BlockSpec((None, tf, d), lambda i, j: (layer, j, 0))]
        args = [x, g.reshape(1, d), w_in, w_in, w_out]
    else:
        in_specs += [pl.BlockSpec((d, tf), lambda i, j: (0, j)), pl.BlockSpec((d, tf), lambda i, j: (0, j)),
                     pl.BlockSpec((tf, d), lambda i, j: (j, 0))]
        args = [x, g.reshape(1, d), wg, wu, wo]
    if final_norm:
        in_specs.append(pl.BlockSpec((1, d), lambda i, j: (0, 0)))
        args.append(g_final.reshape(1, d))
    out_specs = [pl.BlockSpec((tm, d), lambda i, j: (i, 0))]
    out_shape = [jax.ShapeDtypeStruct((n, d), F32)]
    if emit:
        out_specs += [pl.BlockSpec((d, tf), lambda i, j: (0, j)), pl.BlockSpec((d, tf), lambda i, j: (0, j)),
                      pl.BlockSpec((tf, d), lambda i, j: (j, 0))]
        out_shape += [jax.ShapeDtypeStruct((d, f), BF16), jax.ShapeDtypeStruct((d, f), BF16),
                      jax.ShapeDtypeStruct((f, d), BF16)]
    res = pl.pallas_call(
        functools.partial(_ffn_body, n_ff_steps=nj, final_norm=final_norm, emit=emit),
        grid=(n // tm, nj),
        in_specs=in_specs,
        out_specs=out_specs,
        out_shape=out_shape,
        scratch_shapes=[pltpu.VMEM((tm, d), BF16)],
        compiler_params=_cparams(("parallel", "arbitrary")),
        name="ffn_emit" if emit else "ffn",
    )(*args)
    return (res[0], ("bf16", res[1], res[2], res[3])) if emit else (res[0], w)


def _proj_body(*refs, emit, tn, tm, n_alias, layer):
    n_w = 3 if emit else 2
    x_ref, g_ref = refs[:2]
    w_refs = refs[2:2 + n_w]
    outs = refs[2 + n_w + n_alias:-1]
    om_ref, os_ref, ak_ref, av_ref, aki_ref, bk_ref, bv_ref = outs[:7]
    cf_ref = outs[-1]
    xn_ref = refs[-1]
    j = pl.program_id(1)

    def this_layer(ref):
        if n_alias:
            return ref
        for other in range(ref.shape[0]):
            if other != layer:
                ref[other] = jnp.zeros(ref.shape[1:], F32)
        return ref.at[layer]

    if emit:
        wa_ref, wx_ref, wsm_ref = w_refs
        wmo_ref, wso_ref = outs[7:9]
        n_aligned = COL_B_Q // tn

        @pl.when(j < n_aligned)
        def _():
            wmo_ref[...] = wa_ref[...].astype(BF16)

        @pl.when(j >= n_aligned)
        def _():
            tall = jnp.concatenate([wa_ref[...], wx_ref[...]], axis=0)
            wmo_ref[...] = tall[NARROW_W:NARROW_W + tn, :].astype(BF16)

        wm = wmo_ref[...]
    else:
        wm = w_refs[0][...]

    @pl.when(j == 0)
    def _():
        xn = _rmsnorm(x_ref[...], g_ref[...]).astype(BF16)
        xn_ref[...] = xn
        if emit:
            out_col = lax.broadcasted_iota(jnp.int32, wsm_ref.shape, 0)
            ws = jnp.where(out_col < NARROW_W, wsm_ref[...], 0.0).astype(BF16)
            wso_ref[...] = ws
        else:
            ws = w_refs[1][...]
        small = _dot_nt(xn, ws)
        os_ref[...] = small
        this_layer(aki_ref)[...] = small[:, 0:IDX_DIM]

    acc = _dot_nt(xn_ref[...], wm)
    om_ref[...] = acc.astype(BF16)

    if tn <= C_W:
        @pl.when((j >= COL_C_F // tn) & (j < COL_C_I // tn))
        def _():
            cf_ref[...] = acc
    else:
        @pl.when(j == COL_C_F // tn)
        def _():
            cf_ref[...] = acc[:, COL_C_F % tn:COL_C_F % tn + C_W]

    def put_heads(ref, col, heads):
        off = col % tn
        assert off + heads * V7X_LANES <= tn

        @pl.when(j == col // tn)
        def _():
            dst = this_layer(ref)
            for hd in range(heads):
                dst[pl.ds(hd, tm, stride=heads), :] = acc[:, off + hd * V7X_LANES:off + (hd + 1) * V7X_LANES]

    put_heads(ak_ref, COL_A_K, A_KV_HEADS)
    put_heads(av_ref, COL_A_V, A_KV_HEADS)
    put_heads(bk_ref, COL_B_K, B_HEADS)
    put_heads(bv_ref, COL_B_V, B_HEADS)


def _proj(x, g, w, caches, layer, depth):
    n, d = x.shape
    emit = w[0] == "f32"
    tm = _row_tile(n, 512)
    tn = 512 if emit else PROJ_TN
    assert not emit or n == tm, "the weight-emitting variant expects a single row tile"
    lanes_per_tn = tn // V7X_LANES
    if emit:
        assert COL_B_Q % tn == 0 and NARROW_W % 8 == 0
        w_in_t = w[1]
        w_specs = [pl.BlockSpec((None, tn, d), lambda i, j: (layer, j, 0)),
                   pl.BlockSpec((None, V7X_LANES, d), lambda i, j: (layer, lanes_per_tn * (j + 1), 0)),
                   pl.BlockSpec((None, SMALL_W, d), lambda i, j: (layer, COL_B_Q // SMALL_W, 0))]
        w_args = [w_in_t, w_in_t, w_in_t]
    else:
        w_specs = [pl.BlockSpec((tn, d), lambda i, j: (j, 0)), pl.BlockSpec((SMALL_W, d), lambda i, j: (0, 0))]
        w_args = list(w[1:])
    cache_rows = (A_KV_HEADS, A_KV_HEADS, 1, B_HEADS, B_HEADS)
    cache_lanes = (V7X_LANES, V7X_LANES, IDX_DIM, V7X_LANES, V7X_LANES)
    alias_args = [] if caches is None else list(caches)
    out_specs = [pl.BlockSpec((tm, tn), lambda i, j: (i, j)), pl.BlockSpec((tm, SMALL_W), lambda i, j: (i, 0))]
    if alias_args:
        cache_spec = lambda r, wd: pl.BlockSpec((None, r * tm, wd), lambda i, j: (layer, i, 0))
    else:
        cache_spec = lambda r, wd: pl.BlockSpec((depth, r * tm, wd), lambda i, j: (0, i, 0))
    out_specs += [cache_spec(r, wd) for r, wd in zip(cache_rows, cache_lanes)]
    out_shape = [jax.ShapeDtypeStruct((n, MAIN_W), BF16), jax.ShapeDtypeStruct((n, SMALL_W), F32)]
    out_shape += [jax.ShapeDtypeStruct((depth, r * n, wd), F32) for r, wd in zip(cache_rows, cache_lanes)]
    if emit:
        out_specs += [pl.BlockSpec((tn, d), lambda i, j: (j, 0)), pl.BlockSpec((SMALL_W, d), lambda i, j: (0, 0))]
        out_shape += [jax.ShapeDtypeStruct((MAIN_W, d), BF16), jax.ShapeDtypeStruct((SMALL_W, d), BF16)]
    if tn <= C_W:
        assert COL_C_F % tn == 0 and C_W % tn == 0
        cf_first, cf_tiles = COL_C_F // tn, C_W // tn
        out_specs.append(pl.BlockSpec((tm, tn), lambda i, j: (i, jnp.clip(j - cf_first, 0, cf_tiles - 1))))
    else:
        assert COL_C_F % tn + C_W <= tn
        out_specs.append(pl.BlockSpec((tm, C_W), lambda i, j: (i, 0)))
    out_shape.append(jax.ShapeDtypeStruct((n, C_W), F32))
    n_in = 2 + len(w_args)
    res = pl.pallas_call(
        functools.partial(_proj_body, emit=emit, tn=tn, tm=tm, n_alias=len(alias_args), layer=layer),
        grid=(n // tm, MAIN_W // tn),
        in_specs=[pl.BlockSpec((tm, d), lambda i, j: (i, 0)), pl.BlockSpec((1, d), lambda i, j: (0, 0))]
        + w_specs + [pl.BlockSpec(memory_space=pl.ANY)] * len(alias_args),
        out_specs=out_specs,
        out_shape=out_shape,
        input_output_aliases={n_in + k: 2 + k for k in range(len(alias_args))},
        scratch_shapes=[pltpu.VMEM((tm, d), BF16)],
        compiler_params=_cparams(("parallel", "arbitrary")),
        name="in_proj_emit" if emit else "in_proj",
    )(x, g.reshape(1, d), *w_args, *alias_args)
    w_bf = ("bf16", res[7], res[8]) if emit else w
    return res[0], res[1], res[-1], tuple(res[2:7]), w_bf


def _sortable_key(x):
    bits = lax.bitcast_convert_type(x, jnp.int32)
    return jnp.where(bits < 0, bits ^ jnp.int32(0x7FFFFFFF), bits)


def _kth_largest(key, k):
    rows = key.shape[0]

    def body(it, t):
        cand = t + lax.shift_left(jnp.int32(1), jnp.int32(31) - it)
        cnt = jnp.sum(jnp.where(key >= cand, 1.0, 0.0), axis=1, keepdims=True)
        return jnp.where(cnt >= k, cand, t)

    t0 = jnp.full((rows, 1), jnp.iinfo(jnp.int32).min, jnp.int32)
    return lax.fori_loop(0, 32, body, t0)


def _tie_cut(tie, kpos, need, nbits):
    rows = tie.shape[0]

    def body(it, t):
        cand = t + lax.shift_left(jnp.int32(1), jnp.int32(nbits - 1) - it)
        cnt = jnp.sum(jnp.where(tie & (kpos < cand), 1.0, 0.0), axis=1, keepdims=True)
        return jnp.where(cnt < need, cand, t)

    return lax.fori_loop(0, nbits, body, jnp.zeros((rows, 1), jnp.int32))


def _visible(tq, s_pad, q_pos0, n_keys):
    qpos = q_pos0 + pl.program_id(1) * tq + lax.broadcasted_iota(jnp.int32, (tq, 1), 0)
    limit = jnp.minimum(((qpos >> CHUNK_SHIFT) + 1) << CHUNK_SHIFT, n_keys)
    kpos = lax.broadcasted_iota(jnp.int32, (tq, s_pad), 1)
    return kpos < limit, kpos


def _head_keys(new_ref, rows, lo, width, past_ref, head, heads, s_pad):
    new = new_ref[rows, lo:lo + width]
    p_len = past_ref.shape[0] // heads
    hist = past_ref[...] if heads == 1 else past_ref[pl.ds(head, p_len, stride=heads), :]
    parts = [hist.astype(BF16), new.astype(BF16)]
    pad = s_pad - p_len - new.shape[0]
    if pad:
        parts.append(jnp.zeros((pad, width), BF16))
    return jnp.concatenate(parts, axis=0)


def _stack_rows(pieces):
    return pieces[0] if len(pieces) == 1 else jnp.concatenate(pieces, axis=0)


def _dsa_hist_body(q_ref, qi_ref, wq_ref, k_ref, v_ref, ki_ref, pk_ref, pv_ref, pki_ref, o_ref,
                   *, nb, tq, s_pad, n_keys, q_pos0, topk):
    vis1, kpos1 = _visible(tq, s_pad, q_pos0, n_keys)
    rep = A_HEADS // A_KV_HEADS
    scores = []
    for bb in range(nb):
        rows = slice(bb * tq, (bb + 1) * tq)
        kidx = _head_keys(ki_ref, rows, 0, IDX_DIM, pki_ref.at[bb], 0, 1, s_pad)
        qi_all = _stack_rows([qi_ref[rows, h * IDX_DIM:(h + 1) * IDX_DIM] for h in range(IDX_HEADS)]).astype(BF16)
        wq = wq_ref[rows, IDX_DIM:IDX_DIM + IDX_HEADS] * (IDX_HEADS ** -0.5) * IDX_SCALE
        w_col = _stack_rows([wq[:, h:h + 1] for h in range(IDX_HEADS)])
        x = jnp.maximum(_dot_nt(qi_all, kidx), 0.0) * w_col
        score = jnp.sum(x.reshape(IDX_HEADS, tq, s_pad), axis=0)
        scores.append(jnp.where(vis1, score + 0.0, NEG_BIG))

    key = _sortable_key(_stack_rows(scores))
    kpos = _stack_rows([kpos1] * nb)
    thr = _kth_largest(key, float(topk))
    n_gt = jnp.sum(jnp.where(key > thr, 1.0, 0.0), axis=1, keepdims=True)
    tie = key == thr
    cut = _tie_cut(tie, kpos, float(topk) - n_gt, max(1, (s_pad - 1).bit_length()))
    attend = jnp.where((key > thr) | (tie & (kpos <= cut)), 1.0, 0.0)

    for bb in range(nb):
        rows = slice(bb * tq, (bb + 1) * tq)
        sel = _stack_rows([jnp.where(vis1, attend[rows], 0.0)] * rep) > 0.5
        for g in range(A_KV_HEADS):
            k_g = _head_keys(k_ref, rows, g * A_HEAD_DIM, A_HEAD_DIM, pk_ref.at[bb], g, A_KV_HEADS, s_pad)
            v_g = _head_keys(v_ref, rows, g * A_HEAD_DIM, A_HEAD_DIM, pv_ref.at[bb], g, A_KV_HEADS, s_pad)
            heads = range(g * rep, (g + 1) * rep)
            q_g = _stack_rows([q_ref[rows, h * A_HEAD_DIM:(h + 1) * A_HEAD_DIM] for h in heads]).astype(BF16)
            att = jnp.where(sel, _dot_nt(q_g, k_g) * (A_HEAD_DIM ** -0.5), NEG_BIG)
            p = jnp.exp(att - jnp.max(att, axis=1, keepdims=True))
            l = jnp.sum(p, axis=1, keepdims=True)
            o = _dot(p.astype(BF16), v_g) / l
            for r, h in enumerate(heads):
                o_ref[rows, h * A_HEAD_DIM:(h + 1) * A_HEAD_DIM] = o[r * tq:(r + 1) * tq].astype(o_ref.dtype)


def _softmax_step_t(carry, s, v_c):
    m, l, acc = carry
    m_new = jnp.maximum(m, _col_reduce(s, jnp.max))
    alpha = jnp.exp(m - m_new)
    p = jnp.exp(s - m_new)
    return m_new, alpha * l + _col_reduce(p, jnp.sum), alpha * acc + _dot_tn(v_c, p.astype(BF16))


def _col_reduce(x, op):
    rows, tq = x.shape
    return op(op(x.reshape(rows // COUNT_ROWS, COUNT_ROWS, tq), axis=0), axis=0, keepdims=True)


def _softmax_init_t(tq, width):
    return (jnp.full((1, tq), -jnp.inf, F32), jnp.zeros((1, tq), F32), jnp.zeros((width, tq), F32))


def _causal_extent(tq, kc):
    i = pl.program_id(1)
    qpos = i * tq + lax.broadcasted_iota(jnp.int32, (1, tq), 1)
    limit = ((qpos >> CHUNK_SHIFT) + 1) << CHUNK_SHIFT
    return (i * tq + tq + kc - 1) // kc, limit, lax.broadcasted_iota(jnp.int32, (kc, 1), 0)


def _dsa_causal_body(q_ref, qi_ref, wq_ref, k_ref, v_ref, ki_ref, o_ref, key_ref, cut_ref, *, tq, kc, topk, nbits):
    n_chunks, limit, krow = _causal_extent(tq, kc)
    wq_t = wq_ref[...].T[IDX_DIM:IDX_DIM + IDX_HEADS, :] * (IDX_HEADS ** -0.5) * IDX_SCALE
    qi_all = jnp.concatenate([qi_ref[:, h * IDX_DIM:(h + 1) * IDX_DIM] for h in range(IDX_HEADS)], axis=0).astype(BF16)

    def score_chunk(c, carry):
        start = pl.multiple_of(c * kc, kc)
        kidx = ki_ref[pl.ds(start, kc), 0:IDX_DIM].astype(BF16)
        logits = _dot_nt(kidx, qi_all)
        acc = jnp.zeros((kc, tq), F32)
        for h in range(IDX_HEADS):
            acc = acc + jnp.maximum(logits[:, h * tq:(h + 1) * tq], 0.0) * wq_t[h:h + 1, :]
        score = jnp.where(start + krow < limit, acc + 0.0, NEG_BIG)
        key_ref[c] = _sortable_key(score)
        return carry

    lax.fori_loop(0, n_chunks, score_chunk, 0)

    def count(pred):
        def body(c, part):
            ones = jnp.where(pred(key_ref[c], c), 1.0, 0.0)
            return part + jnp.sum(ones.reshape(kc // COUNT_ROWS, COUNT_ROWS, tq), axis=0)

        part = lax.fori_loop(0, n_chunks, body, jnp.zeros((COUNT_ROWS, tq), F32))
        return jnp.sum(part, axis=0, keepdims=True)

    def thr_step(it, t):
        cand = t + lax.shift_left(jnp.int32(1), jnp.int32(31) - it)
        return jnp.where(count(lambda key, c: key >= cand) >= float(topk), cand, t)

    thr = lax.fori_loop(0, 32, thr_step, jnp.full((1, tq), jnp.iinfo(jnp.int32).min, jnp.int32))
    need = float(topk) - count(lambda key, c: key > thr)
    n_tie = count(lambda key, c: key == thr)

    cut_ref[...] = jnp.full(cut_ref.shape, jnp.iinfo(jnp.int32).max, jnp.int32)

    @pl.when(jnp.max(jnp.where(n_tie != need, 1.0, 0.0)) > 0.0)
    def _():
        def cut_step(it, t):
            cand = t + lax.shift_left(jnp.int32(1), jnp.int32(nbits - 1) - it)
            below = count(lambda key, c: (key == thr) & (c * kc + krow < cand))
            return jnp.where(below < need, cand, t)

        cut = lax.fori_loop(0, nbits, cut_step, jnp.zeros((1, tq), jnp.int32))
        cut_ref[...] = jnp.broadcast_to(cut, cut_ref.shape)

    cut = cut_ref[0:1, :]

    def select_chunk(c, carry):
        key = key_ref[c]
        kpos = c * kc + krow
        tie_flag = jnp.where(key == thr, jnp.where(kpos <= cut, 0, 1), 1)
        key_ref[c] = jnp.where(kpos < limit, jnp.where(key > thr, 0, tie_flag), 1)
        return carry

    lax.fori_loop(0, n_chunks, select_chunk, 0)

    rep = A_HEADS // A_KV_HEADS
    q_g = [jnp.concatenate([q_ref[:, h * A_HEAD_DIM:(h + 1) * A_HEAD_DIM] for h in range(g * rep, (g + 1) * rep)],
                           axis=0).astype(BF16) for g in range(A_KV_HEADS)]

    def att_chunk(c, carry):
        start = pl.multiple_of(c * kc, kc)
        attend = key_ref[c] == 0
        attend = jnp.concatenate([attend] * rep, axis=1)
        logits = [_dot_nt(k_ref[pl.ds(start, kc), g * A_HEAD_DIM:(g + 1) * A_HEAD_DIM].astype(BF16), q_g[g])
                  for g in range(A_KV_HEADS)]
        v_c = [v_ref[pl.ds(start, kc), g * A_HEAD_DIM:(g + 1) * A_HEAD_DIM].astype(BF16) for g in range(A_KV_HEADS)]
        return tuple(
            _softmax_step_t(carry[g], jnp.where(attend, logits[g] * (A_HEAD_DIM ** -0.5), NEG_BIG), v_c[g])
            for g in range(A_KV_HEADS))

    res = lax.fori_loop(0, n_chunks, att_chunk, tuple(_softmax_init_t(rep * tq, A_HEAD_DIM) for _ in range(A_KV_HEADS)))
    for g in range(A_KV_HEADS):
        _, l, acc = res[g]
        o_t = acc / l
        for r in range(rep):
            h = g * rep + r
            o_ref[:, h * A_HEAD_DIM:(h + 1) * A_HEAD_DIM] = o_t[:, r * tq:(r + 1) * tq].T.astype(o_ref.dtype)


def _diff_causal_body(q_ref, k_ref, v_ref, lam_ref, g_ref, o_ref, *, tq, kc, lam_init):
    n_chunks, limit, krow = _causal_extent(tq, kc)
    lv = lam_ref[...]
    lam = (jnp.exp(jnp.sum(lv[0:1] * lv[1:2], axis=1, keepdims=True))
           - jnp.exp(jnp.sum(lv[2:3] * lv[3:4], axis=1, keepdims=True)) + lam_init)
    gain = g_ref[...] * (1.0 - lam_init)
    head_w = 2 * B_QK_DIM
    qlane = lax.broadcasted_iota(jnp.int32, (tq, head_w), 1)
    for h0 in range(0, B_HEADS, DIFF_HEADS_PER_LOOP):
        heads = range(h0, h0 + DIFF_HEADS_PER_LOOP)
        q_2 = [jnp.concatenate([jnp.where((qlane >= c * B_QK_DIM) & (qlane < (c + 1) * B_QK_DIM),
                                          q_ref[:, h * head_w:(h + 1) * head_w] * DIFF_SCALE, 0.0) for c in range(2)],
                               axis=0).astype(BF16) for h in heads]

        def chunk(c, carry, q_2=q_2, heads=heads):
            start = pl.multiple_of(c * kc, kc)
            vis = start + krow < jnp.concatenate([limit, limit], axis=1)
            logits = [_dot_nt(k_ref[pl.ds(start, kc), h * head_w:(h + 1) * head_w].astype(BF16), q_2[j])
                      for j, h in enumerate(heads)]
            v_c = [v_ref[pl.ds(start, kc), h * B_V_DIM:(h + 1) * B_V_DIM].astype(BF16) for h in heads]
            return tuple(
                _softmax_step_t(carry[j], jnp.where(vis, logits[j], NEG_BIG), v_c[j])
                for j in range(len(heads)))

        res = lax.fori_loop(0, n_chunks, chunk, tuple(_softmax_init_t(2 * tq, B_V_DIM) for _ in heads))
        for j, h in enumerate(heads):
            _, l, acc = res[j]
            o_t = acc / l
            o = (o_t[:, 0:tq] - lam * o_t[:, tq:2 * tq]).T
            o = o * lax.rsqrt(jnp.mean(o * o, axis=1, keepdims=True) + NORM_EPS) * gain
            o_ref[:, h * B_V_DIM:(h + 1) * B_V_DIM] = o.astype(o_ref.dtype)


def _blk(arr_col, rows, width, kind, nq):
    _, col = arr_col
    cb = col // width
    assert col % width == 0
    if kind == "q":
        return pl.BlockSpec((rows, width), lambda b, i: (b * nq + i, cb))
    return pl.BlockSpec((rows, width), lambda b, i: (b, cb))


def _past_blk(cache, layer, nb):
    return pl.BlockSpec((None, nb) + cache.shape[2:], lambda b, i: (layer, b, 0, 0))


def _seqs_per_step(bsz, limit):
    nb = min(bsz, limit)
    while bsz % nb:
        nb -= 1
    return nb


def _dsa(q, qi, wq, k, v, ki, past, *, bsz, t, s_pad, n_keys, q_pos0, tq, layer):
    nq = t // tq
    topk = min(TOPK_MAX, n_keys // 4)
    scratch = []
    if past is None:
        nb = 1
        kc = _row_tile(t, DSA_KEY_CHUNK)
        body = functools.partial(_dsa_causal_body, tq=tq, kc=kc, topk=topk, nbits=max(1, (t - 1).bit_length()))
        scratch = [pltpu.VMEM((t // kc, kc, tq), jnp.int32), pltpu.VMEM((8, tq), jnp.int32)]
    else:
        assert nq == 1
        nb = _seqs_per_step(bsz, DSA_SEQS_PER_STEP)
        body = functools.partial(_dsa_hist_body, nb=nb, tq=tq, s_pad=s_pad, n_keys=n_keys, q_pos0=q_pos0, topk=topk)
    in_specs = [
        _blk(q, nb * tq, A_Q_W, "q", nq),
        _blk(qi, nb * tq, A_QI_W, "q", nq),
        _blk(wq, nb * tq, SMALL_W, "q", nq),
        _blk(k, nb * t, A_KV_W, "kv", nq),
        _blk(v, nb * t, A_KV_W, "kv", nq),
        _blk(ki, nb * t, SMALL_W, "kv", nq),
    ]
    args = [q[0], qi[0], wq[0], k[0], v[0], ki[0]]
    if past is not None:
        in_specs += [_past_blk(c, layer, nb) for c in past]
        args += list(past)
    return pl.pallas_call(
        body,
        grid=(bsz // nb, nq),
        in_specs=in_specs,
        out_specs=pl.BlockSpec((nb * tq, A_Q_W), lambda b, i: (b * nq + i, 0)),
        out_shape=jax.ShapeDtypeStruct((bsz * t, A_Q_W), BF16),
        scratch_shapes=scratch,
        compiler_params=_cparams(("parallel", "parallel")),
        name="dsa",
    )(*args)


def _diff_hist_body(q_ref, k_ref, v_ref, lam_ref, g_ref, pk_ref, pv_ref, o_ref,
                    *, nb, tq, s_pad, n_keys, q_pos0, lam_init):
    vis1, _ = _visible(tq, s_pad, q_pos0, n_keys)
    vis = _stack_rows([jnp.where(vis1, 1.0, 0.0)] * 2) > 0.5
    lv = lam_ref[...]
    lam = (jnp.exp(jnp.sum(lv[0:1] * lv[1:2], axis=1, keepdims=True))
           - jnp.exp(jnp.sum(lv[2:3] * lv[3:4], axis=1, keepdims=True)) + lam_init)
    gain = g_ref[...] * (1.0 - lam_init)
    head_w = 2 * B_QK_DIM
    lane = lax.broadcasted_iota(jnp.int32, (tq, head_w), 1)
    for bb in range(nb):
        rows = slice(bb * tq, (bb + 1) * tq)
        for h in range(B_HEADS):
            k_h = _head_keys(k_ref, rows, h * head_w, head_w, pk_ref.at[bb], h, B_HEADS, s_pad)
            v_h = _head_keys(v_ref, rows, h * B_V_DIM, B_V_DIM, pv_ref.at[bb], h, B_HEADS, s_pad)
            q_h = q_ref[rows, h * head_w:(h + 1) * head_w]
            q_2 = _stack_rows([jnp.where((lane >= c * B_QK_DIM) & (lane < (c + 1) * B_QK_DIM), q_h * DIFF_SCALE, 0.0)
                               for c in range(2)]).astype(BF16)
            logit = jnp.where(vis, _dot_nt(q_2, k_h), NEG_BIG)
            p = jnp.exp(logit - jnp.max(logit, axis=1, keepdims=True))
            l = jnp.sum(p, axis=1, keepdims=True)
            o_2 = _dot(p.astype(BF16), v_h) / l
            o = o_2[0:tq] - lam * o_2[tq:2 * tq]
            o = o * lax.rsqrt(jnp.mean(o * o, axis=1, keepdims=True) + NORM_EPS) * gain
            o_ref[rows, h * B_V_DIM:(h + 1) * B_V_DIM] = o.astype(o_ref.dtype)


def _diff(q, k, v, lam_vecs, g_diff, past, *, bsz, t, s_pad, n_keys, q_pos0, tq, layer):
    nq = t // tq
    lam_init = 0.8 - 0.6 * math.exp(-0.3 * layer)
    if past is None:
        nb = 1
        body = functools.partial(_diff_causal_body, tq=tq, kc=_row_tile(t, DIFF_KEY_CHUNK), lam_init=lam_init)
    else:
        assert nq == 1
        nb = _seqs_per_step(bsz, DIFF_SEQS_PER_STEP)
        body = functools.partial(_diff_hist_body, nb=nb, tq=tq, s_pad=s_pad, n_keys=n_keys, q_pos0=q_pos0,
                                 lam_init=lam_init)
    in_specs = [
        _blk(q, nb * tq, B_QK_W, "q", nq),
        _blk(k, nb * t, B_QK_W, "kv", nq),
        _blk(v, nb * t, B_V_W, "kv", nq),
        pl.BlockSpec((None, 4, B_QK_DIM), lambda b, i: (layer, 0, 0)),
        pl.BlockSpec((None, 1, B_V_DIM), lambda b, i: (layer, 0, 0)),
    ]
    args = [q[0], k[0], v[0], lam_vecs, g_diff]
    if past is not None:
        in_specs += [_past_blk(c, layer, nb) for c in past]
        args += list(past)
    return pl.pallas_call(
        body,
        grid=(bsz // nb, nq),
        in_specs=in_specs,
        out_specs=pl.BlockSpec((nb * tq, B_V_W), lambda b, i: (b * nq + i, 0)),
        out_shape=jax.ShapeDtypeStruct((bsz * t, B_V_W), BF16),
        compiler_params=_cparams(("parallel", "parallel")),
        name="diff_attn",
    )(*args)


def _cumsum_rows(x):
    n = x.shape[0]
    row = lax.broadcasted_iota(jnp.int32, (n, 1), 0)
    d = 1
    while d < n:
        x = x + jnp.where(row >= d, pltpu.roll(x, d, axis=0), 0.0)
        d *= 2
    return x


def _anchor_rows(b, c):
    n, w = b.shape
    if 2 * c >= 8:
        pieces = [jnp.broadcast_to(b[j * 2 * c + c - 1:j * 2 * c + c, :], (2 * c, w)) for j in range(n // (2 * c))]
        return pieces[0] if len(pieces) == 1 else jnp.concatenate(pieces, axis=0)
    rmod = lax.broadcasted_iota(jnp.int32, (n, 1), 0) & (2 * c - 1)
    m = b
    for s in range(-c, c):
        if s != 0:
            m = jnp.where(rmod == c - 1 - s, pltpu.roll(b, (-s) % n, axis=0), m)
    return m


def _hgrn_body(*refs, chunk, layer, has_s0):
    if has_s0:
        q_ref, f_ref, i_ref, g_ref, glb_ref, gn_ref, s0_ref, o_ref, s_ref, st_ref = refs
    else:
        q_ref, f_ref, i_ref, g_ref, glb_ref, gn_ref, o_ref, s_ref, st_ref = refs
    ci = pl.program_id(1)

    @pl.when(ci == 0)
    def _():
        for h in range(C_HEADS):
            st_ref[h] = s0_ref[h].T if has_s0 else jnp.zeros((C_V_DIM, C_K_DIM), F32)

    glb = glb_ref[...]
    pe = jnp.exp(glb - jnp.max(glb, axis=0, keepdims=True))
    pn = pe / jnp.sum(pe, axis=0, keepdims=True)
    lb = jnp.sum(pn[0:layer + 1], axis=0, keepdims=True) - pn[0:1]

    zf = f_ref[...]
    log_sig = jnp.minimum(zf, 0.0) - jnp.log1p(jnp.exp(-jnp.abs(zf)))
    la = jnp.log(jnp.maximum(lb, LB_FLOOR))
    lc = jnp.log1p(-lb) + log_sig
    hi = jnp.maximum(la, lc)
    log_f = hi + jnp.log1p(jnp.exp(jnp.minimum(la, lc) - hi))
    kc = (1.0 - lb) * jax.nn.sigmoid(-zf)
    b_all = _cumsum_rows(log_f)

    row = lax.broadcasted_iota(jnp.int32, (chunk, 1), 0)
    rr = lax.broadcasted_iota(jnp.int32, (chunk, chunk), 0)
    cc = lax.broadcasted_iota(jnp.int32, (chunk, chunk), 1)
    heads = range(C_HEADS)
    head = lambda x, h: x[:, h * C_K_DIM:(h + 1) * C_K_DIM]

    q_all = q_ref[...].astype(F32)
    vb_all = i_ref[...]
    v_all = vb_all.astype(F32)
    b_last = b_all[chunk - 1:chunk, :]
    qe_all = (q_all * jnp.exp(b_all)).astype(BF16)
    kdec_all = (kc * jnp.exp(b_last - b_all)).astype(BF16)
    diag_all = q_all * kc
    levels = []
    c = chunk // 2
    while c >= 1:
        lg = c.bit_length() - 1
        upper = ((row >> lg) & 1) == 1
        d = b_all - _anchor_rows(b_all, c)
        e = jnp.exp(jnp.where(upper, d, -d))
        x = (jnp.where(upper, q_all, kc) * e).astype(BF16)
        pairs = ((rr >> (lg + 1)) == (cc >> (lg + 1))) & (((rr >> lg) & 1) == 1) & (((cc >> lg) & 1) == 0)
        levels.append((x, pairs))
        c //= 2

    st = [st_ref[h] for h in heads]
    o_inter = [_dot_nt(head(qe_all, h), st[h].astype(BF16)) for h in heads]
    a = []
    for h in heads:
        a_h = jnp.zeros((chunk, chunk), F32)
        for x, pairs in levels:
            a_h = jnp.where(pairs, _dot_nt(head(x, h), head(x, h)), a_h)
        a.append(a_h)
    for h in heads:
        st_new = st[h] * jnp.exp(head(b_last, h)) + _dot_tn(head(vb_all, h), head(kdec_all, h))
        st_ref[h] = st_new

        @pl.when(ci == pl.num_programs(1) - 1)
        def _():
            s_ref[0, h] = st_new.T

    gn = gn_ref[...]
    for h in heads:
        o = o_inter[h] + jnp.sum(head(diag_all, h), axis=1, keepdims=True) * head(v_all, h)
        o = o + _dot(a[h].astype(BF16), head(vb_all, h))
        o = o * lax.rsqrt(jnp.mean(o * o, axis=1, keepdims=True) + NORM_EPS) * gn
        zg = head(g_ref, h).astype(F32)
        o_ref[:, h * C_V_DIM:(h + 1) * C_V_DIM] = (o * (zg * jax.nn.sigmoid(zg))).astype(o_ref.dtype)


def _hgrn(cq, cf, cin, cg, gamma_lb, g_hgrn, s0, *, bsz, t, layer):
    chunk = _row_tile(t, HGRN_CHUNK)
    nc = t // chunk
    depth = gamma_lb.shape[0]
    has_s0 = s0 is not None

    def seq_blk(arr_col):
        cb = arr_col[1] // C_W
        return pl.BlockSpec((chunk, C_W), lambda b, c: (b * nc + c, cb))

    in_specs = [seq_blk(cq), seq_blk(cf), seq_blk(cin), seq_blk(cg),
                pl.BlockSpec((depth, C_W), lambda b, c: (0, 0)),
                pl.BlockSpec((None, 1, C_V_DIM), lambda b, c: (layer, 0, 0))]
    args = [cq[0], cf[0], cin[0], cg[0], gamma_lb, g_hgrn]
    if has_s0:
        in_specs.append(pl.BlockSpec((None, None, C_HEADS, C_K_DIM, C_V_DIM), lambda b, c: (layer, b, 0, 0, 0)))
        args.append(s0)
    return pl.pallas_call(
        functools.partial(_hgrn_body, chunk=chunk, layer=layer, has_s0=has_s0),
        grid=(bsz, nc),
        in_specs=in_specs,
        out_specs=[pl.BlockSpec((chunk, C_W), lambda b, c: (b * nc + c, 0)),
                   pl.BlockSpec((1, C_HEADS, C_K_DIM, C_V_DIM), lambda b, c: (b, 0, 0, 0))],
        out_shape=[jax.ShapeDtypeStruct((bsz * t, C_W), BF16),
                   jax.ShapeDtypeStruct((bsz, C_HEADS, C_K_DIM, C_V_DIM), F32)],
        scratch_shapes=[pltpu.VMEM((C_HEADS, C_V_DIM, C_K_DIM), F32)],
        compiler_params=_cparams(("parallel", "arbitrary")),
        name="hgrn2",
    )(*args)


def _outproj_body(*refs, emit):
    h_ref, oa_ref, ob_ref, oc_ref, wa_ref, wb_ref, wc_ref, o_ref = refs[:8]
    wa, wb, wc = wa_ref[...].astype(BF16), wb_ref[...].astype(BF16), wc_ref[...].astype(BF16)
    if emit:
        refs[8][...] = wa
        refs[9][...] = wb
        refs[10][...] = wc
    o_ref[...] = h_ref[...] + (_dot(oa_ref[...], wa) + _dot(ob_ref[...], wb) + _dot(oc_ref[...], wc))


def _outproj(h, oa, ob, oc, w):
    n, d = h.shape
    emit = w[0] == "f32"
    tm = _row_tile(n, 512)
    assert not emit or n == tm, "the weight-emitting variant expects a single row tile"
    row = lambda wd: pl.BlockSpec((tm, wd), lambda i: (i, 0))
    whole = lambda r: pl.BlockSpec((r, d), lambda i: (0, 0))
    seg_rows = (A_Q_W, B_V_W, C_W)
    seg_first = (0, A_Q_W, A_Q_W + B_V_W)
    if emit:
        _, w_out, layer = w
        w_specs = [pl.BlockSpec((None, r, d), functools.partial(lambda i, blk: (layer, blk, 0), blk=f // r))
                   for r, f in zip(seg_rows, seg_first)]
        w_args = [w_out] * 3
    else:
        w_specs = [whole(r) for r in seg_rows]
        w_args = list(w[1:])
    out_specs = [row(d)]
    out_shape = [jax.ShapeDtypeStruct((n, d), F32)]
    if emit:
        out_specs += [whole(r) for r in seg_rows]
        out_shape += [jax.ShapeDtypeStruct((r, d), BF16) for r in seg_rows]
    res = pl.pallas_call(
        functools.partial(_outproj_body, emit=emit),
        grid=(n // tm,),
        in_specs=[row(d), row(A_Q_W), row(B_V_W), row(C_W)] + w_specs,
        out_specs=out_specs,
        out_shape=out_shape,
        compiler_params=_cparams(("parallel",)),
        name="out_proj_emit" if emit else "out_proj",
    )(h, oa, ob, oc, *w_args)
    return (res[0], ("bf16", res[1], res[2], res[3])) if emit else (res[0], w)


def _layer(x, layer, w, p, caches, past, geom):
    bsz, t = geom["bsz"], geom["t"]
    depth = p["g_ffn1"].shape[0]
    wb = {}
    h, wb["ffn1"] = _ffn(x, p["g_ffn1"][layer], w["ffn1"])
    main, small, c_f, caches, wb["proj"] = _proj(h, p["g_mix"][layer], w["proj"], caches, layer, depth)
    if past is None:
        past_a = past_b = s0 = None
    else:
        past_a, past_b, s0 = past[0:3], past[3:5], past[5]
    o_a = _dsa((main, COL_A_Q), (main, COL_A_QI), (small, 0), (main, COL_A_K), (main, COL_A_V), (small, 0), past_a,
               layer=layer, **geom)
    o_b = _diff((main, COL_B_Q), (main, COL_B_K), (main, COL_B_V), p["diff_lambda"], p["g_diff"], past_b,
                layer=layer, **geom)
    o_c, s_c = _hgrn((main, COL_C_Q), (c_f, 0), (main, COL_C_I), (main, COL_C_G),
                     p["gamma_lb"], p["g_hgrn"], s0, bsz=bsz, t=t, layer=layer)
    h, wb["out"] = _outproj(h, o_a, o_b, o_c, w["out"])
    x, wb["ffn2"] = _ffn(h, p["g_ffn2"][layer], w["ffn2"], p["g_final"] if layer == depth - 1 else None)
    return x, caches, s_c, wb


def kernel(x_prompt, x_sample, cache_a_k, cache_a_v, cache_a_kidx, cache_b_k, cache_b_v, state_c,
           g_ffn1, w_ffn1_in, w_ffn1_out, g_mix, w_in, diff_lambda, g_diff, gamma_lb, g_hgrn, w_out,
           g_ffn2, w_ffn2_in, w_ffn2_out, g_final):
    depth, d = w_in.shape[0], w_in.shape[1]
    bp, tp, _ = x_prompt.shape
    bs, ts, _ = x_sample.shape
    p_len = cache_a_k.shape[2]
    p = dict(g_ffn1=g_ffn1, g_mix=g_mix, diff_lambda=diff_lambda, g_diff=g_diff.reshape(depth, 1, B_V_DIM),
             gamma_lb=gamma_lb, g_hgrn=g_hgrn.reshape(depth, 1, C_V_DIM), g_ffn2=g_ffn2, g_final=g_final)
    geom_p = dict(bsz=bp, t=tp, s_pad=tp, n_keys=tp, q_pos0=0, tq=_row_tile(tp, ATTN_QUERY_TILE))
    n_keys_s = p_len + ts
    geom_s = dict(bsz=bs, t=ts, s_pad=-(-n_keys_s // V7X_LANES) * V7X_LANES, n_keys=n_keys_s, q_pos0=p_len, tq=ts)
    past = (cache_a_k.reshape(depth, bs, p_len * A_KV_HEADS, A_HEAD_DIM),
            cache_a_v.reshape(depth, bs, p_len * A_KV_HEADS, A_HEAD_DIM),
            cache_a_kidx,
            cache_b_k.reshape(depth, bs, p_len * B_HEADS, 2 * B_QK_DIM),
            cache_b_v.reshape(depth, bs, p_len * B_HEADS, B_V_DIM),
            state_c)

    w_in_t = jnp.swapaxes(w_in, 1, 2)
    xp = x_prompt.reshape(bp * tp, d)
    xs = x_sample.reshape(bs * ts, d)
    caches_p = caches_s = None
    states_p, states_s = [], []
    for layer in range(depth):
        w_f32 = dict(ffn1=("f32", w_ffn1_in, w_ffn1_out, layer), proj=("f32", w_in_t), out=("f32", w_out, layer),
                     ffn2=("f32", w_ffn2_in, w_ffn2_out, layer))
        xs, caches_s, s_c, w_bf = _layer(xs, layer, w_f32, p, caches_s, past, geom_s)
        states_s.append(s_c)
        xp, caches_p, s_c, _ = _layer(xp, layer, w_bf, p, caches_p, None, geom_p)
        states_p.append(s_c)

    def group_out(x, caches, states, bsz, t):
        a_k, a_v, a_ki, b_k, b_v = caches
        return (x.reshape(bsz, t, d),
                a_k.reshape(depth, bsz, t, A_KV_HEADS, A_HEAD_DIM), a_v.reshape(depth, bsz, t, A_KV_HEADS, A_HEAD_DIM),
                a_ki.reshape(depth, bsz, t, IDX_DIM), b_k.reshape(depth, bsz, t, B_HEADS, 2 * B_QK_DIM),
                b_v.reshape(depth, bsz, t, B_HEADS, B_V_DIM), jnp.stack(states))

    out_p = group_out(xp, caches_p, states_p, bp, tp)
    out_s = group_out(xs, caches_s, states_s, bs, ts)
    return (out_p[0], out_s[0], *out_p[1:], *out_s[1:])
```
